```python
import jax, jax.numpy as jnp
from jax import lax
import numpy as np

D_MODEL = 2048
BATCH = 8
SEQ = 2048
DEPTH = 1

HEAD_DIM = 64
N_Q_HEADS = 32
N_KV_HEADS = 4
GROUP = N_Q_HEADS // N_KV_HEADS
WINDOW = 128
ATT_BLOCK = 128
ATT_Q_W = N_Q_HEADS * HEAD_DIM
ATT_KV_W = N_KV_HEADS * HEAD_DIM
ATT_W = ATT_Q_W

DN_K_HEADS = 16
DN_V_HEADS = 32
DN_K_DIM = 128
DN_V_DIM = 128
V_PER_K = DN_V_HEADS // DN_K_HEADS
DN_QK_W = DN_K_HEADS * DN_K_DIM
DN_V_W = DN_V_HEADS * DN_V_DIM
DN_CONV_CH = 2 * DN_QK_W + DN_V_W
CONV_WIDTH = 4
CHUNK = 64

N_BRANCH = 2
EPS = 1e-6

OFF_AQ = 0
OFF_AK = OFF_AQ + ATT_Q_W
OFF_AV = OFF_AK + ATT_KV_W
OFF_AZ = OFF_AV + ATT_KV_W
OFF_DQKV = OFF_AZ + ATT_W
OFF_DZ = OFF_DQKV + DN_CONV_CH
OFF_DB = OFF_DZ + DN_V_W
OFF_DA = OFF_DB + DN_V_HEADS
OFF_G = OFF_DA + DN_V_HEADS
IN_W = OFF_G + N_BRANCH * D_MODEL
ATT_QKV_W = ATT_Q_W + 2 * ATT_KV_W

kernel_name = "hybrid_swa_sink_gated_deltanet_parallel"


def rms_norm(x, w):
    xf = x.astype(jnp.float32)
    y = xf * lax.rsqrt(jnp.mean(xf * xf, axis=-1, keepdims=True) + EPS)
    return (y * w.astype(jnp.float32)).astype(x.dtype)


def l2_norm(x):
    xf = x.astype(jnp.float32)
    return xf * lax.rsqrt(jnp.sum(xf * xf, axis=-1, keepdims=True) + EPS)


def sliding_window_attention(q, k, v, sinks):
    B, S = q.shape[0], q.shape[1]
    L = ATT_BLOCK
    nb = S // L
    scale = HEAD_DIM ** -0.5
    qb = q.reshape(B, nb, L, N_KV_HEADS, GROUP, HEAD_DIM).transpose(1, 0, 2, 3, 4, 5)
    pad = jnp.zeros((B, L, N_KV_HEADS, HEAD_DIM), k.dtype)
    kp = jnp.concatenate([pad, k], axis=1).reshape(B, nb + 1, L, N_KV_HEADS, HEAD_DIM)
    vp = jnp.concatenate([pad, v], axis=1).reshape(B, nb + 1, L, N_KV_HEADS, HEAD_DIM)
    kw = jnp.concatenate([kp[:, :-1], kp[:, 1:]], axis=2).transpose(1, 0, 2, 3, 4)
    vw = jnp.concatenate([vp[:, :-1], vp[:, 1:]], axis=2).transpose(1, 0, 2, 3, 4)
    qi = jnp.arange(L)[:, None]
    ki = jnp.arange(2 * L)[None, :]
    rel = qi + L - ki
    band = (rel >= 0) & (rel < WINDOW)
    key_valid = (jnp.arange(nb)[:, None] * L - L + jnp.arange(2 * L)[None, :]) >= 0
    sink = sinks.astype(jnp.float32).reshape(N_KV_HEADS, GROUP)[None, :, :, None, None]

    def one_block(args):
        q_blk, k_blk, v_blk, kv_ok = args
        s = jnp.einsum('bqhgd,bkhd->bhgqk', q_blk, k_blk).astype(jnp.float32) * scale
        s = jnp.where((band & kv_ok[None, :])[None, None, None], s, -jnp.inf)
        m = jnp.maximum(jnp.max(s, axis=-1, keepdims=True), sink)
        p = jnp.exp(s - m)
        p = p / (jnp.sum(p, axis=-1, keepdims=True) + jnp.exp(sink - m))
        return jnp.einsum('bhgqk,bkhd->bqhgd', p.astype(v_blk.dtype), v_blk)

    o = lax.map(one_block, (qb, kw, vw, key_valid))
    return o.transpose(1, 0, 2, 3, 4, 5).reshape(B, S, N_Q_HEADS * HEAD_DIM)


def causal_depthwise_conv(u, w):
    S = u.shape[1]
    up = jnp.pad(u, ((0, 0), (CONV_WIDTH - 1, 0), (0, 0)))
    y = up[:, 0:S] * w[0]
    for i in range(1, CONV_WIDTH):
        y = y + up[:, i:i + S] * w[i]
    return y


def gated_delta_rule(q, k, v, g, beta):
    B, S, H, DK = q.shape
    DV = v.shape[-1]
    N = S // CHUNK
    f32 = jnp.float32

    def chunks(t):
        return t.astype(f32).reshape(B, N, CHUNK, H, -1).transpose(0, 3, 1, 2, 4)

    q = chunks(q) * (DK ** -0.5)
    k = chunks(k)
    v = chunks(v)
    beta = beta.astype(f32).reshape(B, N, CHUNK, H).transpose(0, 3, 1, 2)
    g = jnp.cumsum(g.astype(f32).reshape(B, N, CHUNK, H).transpose(0, 3, 1, 2), axis=-1)
    k_beta = k * beta[..., None]
    v_beta = v * beta[..., None]
    tril = jnp.tril(jnp.ones((CHUNK, CHUNK), bool))
    strict = jnp.tril(jnp.ones((CHUNK, CHUNK), bool), -1)
    decay = jnp.exp(jnp.where(tril, g[..., :, None] - g[..., None, :], -jnp.inf))
    a_low = jnp.where(strict, jnp.einsum('bhncd,bhnkd->bhnck', k_beta, k) * decay, 0.0)
    lhs = a_low + jnp.eye(CHUNK, dtype=f32)
    rhs = jnp.concatenate([v_beta, k_beta * jnp.exp(g)[..., None]], axis=-1)
    sol = lax.linalg.triangular_solve(lhs, rhs, left_side=True, lower=True, unit_diagonal=True)
    u = sol[..., :DV]
    w = sol[..., DV:]
    qk = jnp.einsum('bhncd,bhnkd->bhnck', q, k) * decay

    def to_scan(t):
        return jnp.moveaxis(t, 2, 0)

    def step(state, inp):
        q_i, k_i, u_i, w_i, g_i, qk_i = inp
        v_new = u_i - jnp.einsum('bhcd,bhde->bhce', w_i, state)
        o = (jnp.einsum('bhcd,bhde->bhce', q_i * jnp.exp(g_i)[..., None], state)
             + jnp.einsum('bhck,bhke->bhce', qk_i, v_new))
        g_last = g_i[..., -1]
        k_dec = k_i * jnp.exp(g_last[..., None] - g_i)[..., None]
        state = state * jnp.exp(g_last)[..., None, None] + jnp.einsum('bhcd,bhce->bhde', k_dec, v_new)
        return state, o

    state0 = jnp.zeros((B, H, DK, DV), f32)
    _, o = lax.scan(step, state0, (to_scan(q), to_scan(k), to_scan(u), to_scan(w), to_scan(g), to_scan(qk)))
    return o.transpose(1, 0, 3, 2, 4).reshape(B, S, H, DV)


def setup_inputs(seed: int = 0) -> dict:
    key = jax.random.key(seed)
    ks = jax.random.split(key, 16)
    f32 = jnp.float32
    nrm = lambda k, shape, s: jax.random.normal(k, shape, f32) * s
    return {
        "x": jax.random.normal(ks[0], (BATCH, SEQ, D_MODEL), f32),
        "norm_w": 1.0 + nrm(ks[1], (DEPTH, D_MODEL), 0.02),
        "w_in": nrm(ks[2], (DEPTH, D_MODEL, IN_W), D_MODEL ** -0.5),
        "b_qkv": nrm(ks[3], (DEPTH, ATT_QKV_W), 0.02),
        "sinks": nrm(ks[4], (DEPTH, N_Q_HEADS), 0.5),
        "conv_w": nrm(ks[5], (DEPTH, CONV_WIDTH, DN_CONV_CH), CONV_WIDTH ** -0.5),
        "a_log": jnp.log(jax.random.uniform(ks[6], (DEPTH, DN_V_HEADS), f32, 1.0, 16.0)),
        "dt_bias": 1.0 + nrm(ks[7], (DEPTH, DN_V_HEADS), 0.1),
        "dn_norm_w": 1.0 + nrm(ks[8], (DEPTH, DN_V_DIM), 0.02),
        "w_att_branch": nrm(ks[9], (DEPTH, ATT_W, D_MODEL), ATT_W ** -0.5),
        "w_dn_branch": nrm(ks[10], (DEPTH, DN_V_W, D_MODEL), DN_V_W ** -0.5),
        "w_out": nrm(ks[11], (DEPTH, D_MODEL, D_MODEL), D_MODEL ** -0.5),
        "final_norm_w": 1.0 + nrm(ks[12], (D_MODEL,), 0.02),
    }


def reference(x, norm_w, w_in, b_qkv, sinks, conv_w, a_log, dt_bias, dn_norm_w,
              w_att_branch, w_dn_branch, w_out, final_norm_w):
    B, S, _ = x.shape
    for l in range(DEPTH):
        h = rms_norm(x, norm_w[l])
        proj = jnp.einsum('bsd,de->bse', h, w_in[l])

        qkv = proj[..., OFF_AQ:OFF_AZ] + b_qkv[l]
        q_a = qkv[..., :ATT_Q_W].reshape(B, S, N_Q_HEADS, HEAD_DIM)
        k_a = qkv[..., ATT_Q_W:ATT_Q_W + ATT_KV_W].reshape(B, S, N_KV_HEADS, HEAD_DIM)
        v_a = qkv[..., ATT_Q_W + ATT_KV_W:].reshape(B, S, N_KV_HEADS, HEAD_DIM)
        o_a = sliding_window_attention(q_a, k_a, v_a, sinks[l])
        o_a = o_a * jax.nn.silu(proj[..., OFF_AZ:OFF_DQKV])
        y_a = jnp.einsum('bse,ed->bsd', o_a, w_att_branch[l])

        mixed = jax.nn.silu(causal_depthwise_conv(proj[..., OFF_DQKV:OFF_DZ], conv_w[l]))
        q_d = l2_norm(mixed[..., :DN_QK_W].reshape(B, S, DN_K_HEADS, DN_K_DIM))
        k_d = l2_norm(mixed[..., DN_QK_W:2 * DN_QK_W].reshape(B, S, DN_K_HEADS, DN_K_DIM))
        v_d = mixed[..., 2 * DN_QK_W:].reshape(B, S, DN_V_HEADS, DN_V_DIM)
        q_d = jnp.repeat(q_d, V_PER_K, axis=2)
        k_d = jnp.repeat(k_d, V_PER_K, axis=2)
        beta = jax.nn.sigmoid(proj[..., OFF_DB:OFF_DA].astype(jnp.float32))
        g = -jnp.exp(a_log[l].astype(jnp.float32)) * jax.nn.softplus(
            proj[..., OFF_DA:OFF_G].astype(jnp.float32) + dt_bias[l].astype(jnp.float32))
        o_d = gated_delta_rule(q_d, k_d, v_d, g, beta).astype(x.dtype)
        z_d = proj[..., OFF_DZ:OFF_DB].reshape(B, S, DN_V_HEADS, DN_V_DIM)
        o_d = rms_norm(o_d, dn_norm_w[l]) * jax.nn.silu(z_d)
        y_d = jnp.einsum('bse,ed->bsd', o_d.reshape(B, S, DN_V_W), w_dn_branch[l])

        gates = jax.nn.sigmoid(proj[..., OFF_G:].astype(jnp.float32)).astype(x.dtype)
        merged = gates[..., :D_MODEL] * y_a + gates[..., D_MODEL:] * y_d
        x = x + jnp.einsum('bsd,de->bse', merged, w_out[l])
    return rms_norm(x, final_norm_w)
```

```python
import functools

import jax
import jax.numpy as jnp
from jax import lax
from jax.experimental import pallas as pl
from jax.experimental.pallas import tpu as pltpu

F32 = jnp.float32
BF16 = jnp.bfloat16

D_MODEL = 2048
HEAD_DIM = 64
N_Q_HEADS = 32
N_KV_HEADS = 4
GROUP = N_Q_HEADS // N_KV_HEADS
WINDOW = 128
ATT_BLOCK = 128
ATT_Q_W = N_Q_HEADS * HEAD_DIM
ATT_KV_W = N_KV_HEADS * HEAD_DIM
ATT_W = ATT_Q_W

DN_K_HEADS = 16
DN_V_HEADS = 32
DN_K_DIM = 128
DN_V_DIM = 128
DN_QK_W = DN_K_HEADS * DN_K_DIM
DN_V_W = DN_V_HEADS * DN_V_DIM
DN_CONV_CH = 2 * DN_QK_W + DN_V_W
CONV_WIDTH = 4
CHUNK = 64
EPS = 1e-6

OFF_AQ = 0
OFF_AK = OFF_AQ + ATT_Q_W
OFF_AV = OFF_AK + ATT_KV_W
OFF_AZ = OFF_AV + ATT_KV_W
OFF_DQKV = OFF_AZ + ATT_W
OFF_DZ = OFF_DQKV + DN_CONV_CH
OFF_DB = OFF_DZ + DN_V_W
OFF_DA = OFF_DB + DN_V_HEADS
OFF_G = OFF_DA + DN_V_HEADS
IN_W = OFF_G + 2 * D_MODEL

MAIN_W = IN_W - 2 * DN_V_HEADS
P_AQ = 0
P_KV = ATT_Q_W
P_AZ = P_KV + 2 * ATT_KV_W
P_DQ = P_AZ + ATT_W
P_DK = P_DQ + DN_QK_W
P_DV = P_DK + DN_QK_W
P_DZ = P_DV + DN_V_W
P_G = P_DZ + DN_V_W
SMALL_W = 128

V7X_VMEM_LIMIT = 56 * 1024 * 1024


def _silu(v):
    return v * jax.nn.sigmoid(v)


def _mm(a, b):
    return jnp.dot(a.astype(BF16), b.astype(BF16), preferred_element_type=F32)


INPROJ_TM = 1024
INPROJ_TN = 512
NORM_ROWS = 128


def _inproj_kernel(x_ref, nw_ref, w_ref, b_ref, ws_ref, o_ref, os_ref, h_ref):
    @pl.when(pl.program_id(1) == 0)
    def _():
        def body(r, carry):
            r0 = pl.multiple_of(r * NORM_ROWS, NORM_ROWS)
            xv = x_ref[pl.ds(r0, NORM_ROWS), :]
            ms = jnp.mean(xv * xv, axis=-1, keepdims=True)
            hv = (xv * lax.rsqrt(ms + EPS)) * nw_ref[...]
            h_ref[pl.ds(r0, NORM_ROWS), :] = hv.astype(BF16)
            return carry
        lax.fori_loop(0, INPROJ_TM // NORM_ROWS, body, 0)
        os_ref[...] = jnp.dot(h_ref[...], ws_ref[...], preferred_element_type=F32)

    acc = jnp.dot(h_ref[...], w_ref[...], preferred_element_type=F32)
    o_ref[...] = (acc + b_ref[...]).astype(o_ref.dtype)


def _inproj(x2, norm_w, w_main, bias, w_small):
    m = x2.shape[0]
    grid = (m // INPROJ_TM, MAIN_W // INPROJ_TN)
    return pl.pallas_call(
        _inproj_kernel,
        grid=grid,
        in_specs=[
            pl.BlockSpec((INPROJ_TM, D_MODEL), lambda i, j: (i, 0)),
            pl.BlockSpec((1, D_MODEL), lambda i, j: (0, 0)),
            pl.BlockSpec((D_MODEL, INPROJ_TN), lambda i, j: (0, j)),
            pl.BlockSpec((1, INPROJ_TN), lambda i, j: (0, j)),
            pl.BlockSpec((D_MODEL, SMALL_W), lambda i, j: (0, 0)),
        ],
        out_specs=[
            pl.BlockSpec((INPROJ_TM, INPROJ_TN), lambda i, j: (i, j)),
            pl.BlockSpec((INPROJ_TM, SMALL_W), lambda i, j: (i, 0)),
        ],
        out_shape=[
            jax.ShapeDtypeStruct((m, MAIN_W), BF16),
            jax.ShapeDtypeStruct((m, SMALL_W), F32),
        ],
        scratch_shapes=[pltpu.VMEM((INPROJ_TM, D_MODEL), BF16)],
        compiler_params=pltpu.CompilerParams(
            dimension_semantics=("parallel", "arbitrary"),
            vmem_limit_bytes=V7X_VMEM_LIMIT),
        name="inproj",
    )(x2, norm_w, w_main, bias, w_small)


def _gates_kernel(s_ref, alog_ref, dt_ref, o_ref):
    seq = s_ref.shape[1]
    row = lax.broadcasted_iota(jnp.int32, (CHUNK, SMALL_W), 0)
    lane = lax.broadcasted_iota(jnp.int32, (CHUNK, SMALL_W), 1)

    def body(c, carry):
        r0 = pl.multiple_of(c * CHUNK, CHUNK)
        xs = s_ref[0, pl.ds(r0, CHUNK), :]
        beta = jax.nn.sigmoid(xs)
        t = xs + dt_ref[...]
        softplus = jnp.maximum(t, 0.0) + jnp.log(1.0 + jnp.exp(-jnp.abs(t)))
        g = -jnp.exp(alog_ref[...]) * softplus
        shift = 1
        while shift < CHUNK:
            g = g + jnp.where(row >= shift, pltpu.roll(g, shift, axis=0), 0.0)
            shift *= 2
        o_ref[0, pl.ds(r0, CHUNK), :] = jnp.where(lane < DN_V_HEADS, beta, g)
        return carry

    lax.fori_loop(0, seq // CHUNK, body, 0)


def _gates(small3, alog_row, dt_row):
    b, s, _ = small3.shape
    return pl.pallas_call(
        _gates_kernel,
        grid=(b,),
        in_specs=[
            pl.BlockSpec((1, s, SMALL_W), lambda i: (i, 0, 0)),
            pl.BlockSpec((1, SMALL_W), lambda i: (0, 0)),
            pl.BlockSpec((1, SMALL_W), lambda i: (0, 0)),
        ],
        out_specs=pl.BlockSpec((1, s, SMALL_W), lambda i: (i, 0, 0)),
        out_shape=jax.ShapeDtypeStruct((b, s, SMALL_W), F32),
        compiler_params=pltpu.CompilerParams(dimension_semantics=("parallel",)),
        name="gates",
    )(small3, alog_row, dt_row)


PAIR_W = 2 * HEAD_DIM
KVH_W = GROUP * HEAD_DIM


def _attn_kernel(sinks_ref, q_ref, kvp_ref, kvc_ref, z_ref, o_ref):
    n = pl.program_id(1)
    h = pl.program_id(2)
    L = ATT_BLOCK
    lane = lax.broadcasted_iota(jnp.int32, (L, PAIR_W), 1)
    lo = lane < HEAD_DIM

    def prep(kv_ref):
        kv = kv_ref[0].astype(F32)
        sw = pltpu.roll(kv, HEAD_DIM, axis=1)
        kk = jnp.where(lo, kv, sw).astype(BF16)
        va = jnp.where(lo, sw, 0.0).astype(BF16)
        vb = jnp.where(lo, 0.0, kv).astype(BF16)
        return kk, va, vb

    kk_p, va_p, vb_p = prep(kvp_ref)
    kk_c, va_c, vb_c = prep(kvc_ref)
    kk = jnp.concatenate([kk_p, kk_c], axis=0)
    va = jnp.concatenate([va_p, va_c], axis=0)
    vb = jnp.concatenate([vb_p, vb_c], axis=0)

    qi = lax.broadcasted_iota(jnp.int32, (L, 2 * L), 0)
    ki = lax.broadcasted_iota(jnp.int32, (L, 2 * L), 1)
    rel = qi + L - ki
    valid = (rel >= 0) & (rel < WINDOW) & ((ki >= L) | (n > 0))

    def probs(qm, sink):
        s = lax.dot_general(qm, kk, (((1,), (1,)), ((), ())), preferred_element_type=F32)
        s = jnp.where(valid, s, -jnp.inf)
        m = jnp.maximum(jnp.max(s, axis=-1, keepdims=True), sink)
        p = jnp.exp(s - m)
        denom = jnp.sum(p, axis=-1, keepdims=True) + jnp.exp(sink - m)
        return p.astype(BF16), 1.0 / denom

    for pr in range(GROUP // 2):
        cols = slice(pr * PAIR_W, (pr + 1) * PAIR_W)
        q2 = q_ref[0, :, cols].astype(F32) * (HEAD_DIM ** -0.5)
        qa = jnp.where(lo, q2, 0.0).astype(BF16)
        qb = jnp.where(lo, 0.0, q2).astype(BF16)
        head = h * GROUP + 2 * pr
        pa, inv_a = probs(qa, sinks_ref[head])
        pb, inv_b = probs(qb, sinks_ref[head + 1])
        o2 = (jnp.dot(pa, va, preferred_element_type=F32)
              + jnp.dot(pb, vb, preferred_element_type=F32))
        o2 = o2 * jnp.where(lo, inv_a, inv_b)
        zz = z_ref[0, :, cols].astype(F32)
        o_ref[0, :, cols] = (o2 * _silu(zz)).astype(o_ref.dtype)


def _attention(proj3, sinks):
    b, s, _ = proj3.shape
    nb = s // ATT_BLOCK
    kv0 = P_KV // PAIR_W
    z0 = P_AZ // KVH_W
    return pl.pallas_call(
        _attn_kernel,
        grid=(b, nb, N_KV_HEADS),
        in_specs=[
            pl.BlockSpec(memory_space=pltpu.SMEM),
            pl.BlockSpec((1, ATT_BLOCK, KVH_W), lambda i, n, h: (i, n, h)),
            pl.BlockSpec((1, ATT_BLOCK, PAIR_W), lambda i, n, h: (i, jnp.maximum(n - 1, 0), kv0 + h)),
            pl.BlockSpec((1, ATT_BLOCK, PAIR_W), lambda i, n, h: (i, n, kv0 + h)),
            pl.BlockSpec((1, ATT_BLOCK, KVH_W), lambda i, n, h: (i, n, z0 + h)),
        ],
        out_specs=pl.BlockSpec((1, ATT_BLOCK, KVH_W), lambda i, n, h: (i, n, h)),
        out_shape=jax.ShapeDtypeStruct((b, s, ATT_W), BF16),
        compiler_params=pltpu.CompilerParams(
            dimension_semantics=("parallel", "parallel", "parallel")),
        name="attn",
    )(sinks, proj3, proj3, proj3, proj3)


V_PER_K = DN_V_HEADS // DN_K_HEADS
CONV_HALO = 8


def _dnet_kernel(q_ref, k_ref, v_ref, z_ref, col_ref, row_ref, cwq_ref, cwk_ref, cwv_ref,
                 nw_ref, o_ref):
    seq = q_ref.shape[1]
    C = CHUNK
    ri = lax.broadcasted_iota(jnp.int32, (C, C), 0)
    ci = lax.broadcasted_iota(jnp.int32, (C, C), 1)
    tril = ri >= ci
    strict = ri > ci
    eye = (ri == ci).astype(F32)

    def conv_silu(ref, w_ref, c, r0):
        width = ref.shape[2]
        cur = ref[0, pl.ds(r0, C), :].astype(F32)
        p0 = pl.multiple_of(jnp.maximum(r0 - CONV_HALO, 0), CONV_HALO)
        prev = ref[0, pl.ds(p0, CONV_HALO), :].astype(F32)
        prev = jnp.where(c > 0, prev, 0.0)
        row = lax.broadcasted_iota(jnp.int32, (C, width), 0)
        pad = jnp.zeros((C - CONV_HALO, width), F32)
        w = w_ref[...]
        y = None
        for i in range(CONV_WIDTH):
            sh = CONV_WIDTH - 1 - i
            if sh == 0:
                term = cur
            else:
                head = jnp.concatenate([pltpu.roll(prev, sh, axis=0), pad], axis=0)
                term = jnp.where(row < sh, head, pltpu.roll(cur, sh, axis=0))
            term = term * w[i:i + 1, :]
            y = term if y is None else y + term
        return _silu(y)

    def l2n(t):
        return t * lax.rsqrt(jnp.sum(t * t, axis=-1, keepdims=True) + EPS)

    def body(c, states):
        r0 = pl.multiple_of(c * C, C)
        q = l2n(conv_silu(q_ref, cwq_ref, c, r0)) * (DN_K_DIM ** -0.5)
        k = l2n(conv_silu(k_ref, cwk_ref, c, r0))
        v = conv_silu(v_ref, cwv_ref, c, r0)
        kb = k.astype(BF16)
        qk_kk = lax.dot_general(jnp.concatenate([q.astype(BF16), kb], axis=0), kb,
                                (((1,), (1,)), ((), ())), preferred_element_type=F32)
        qk = qk_kk[:C]
        kk = qk_kk[C:]
        colp = col_ref[0, 0, pl.ds(r0, C), :]
        rowp = row_ref[0, 0, c]
        new_states = []
        for j in range(V_PER_K):
            beta = colp[:, j:j + 1]
            gc = colp[:, V_PER_K + j:V_PER_K + j + 1]
            gr = rowp[j:j + 1, :]
            g_last = gr[:, C - 1:C]
            decay = jnp.exp(jnp.where(tril, gc - gr, -jnp.inf))
            a = jnp.where(strict, beta * kk * decay, 0.0)
            x = eye - a
            p = _mm(a, a)
            n_sq = 1
            while True:
                x = x + _mm(p, x)
                n_sq *= 2
                if 2 * n_sq >= C:
                    break
                p = _mm(p, p)
            eg = jnp.exp(gc)
            rhs = jnp.concatenate([v[:, j * DN_V_DIM:(j + 1) * DN_V_DIM] * beta,
                                   k * (beta * eg)], axis=1)
            sol = _mm(x, rhs)
            u = sol[:, :DN_V_DIM]
            w = sol[:, DN_V_DIM:]
            st = states[j]
            stb = st.astype(BF16)
            v_new = u - _mm(w, stb)
            o = _mm(q * eg, stb) + _mm(qk * decay, v_new)
            k_dec = k * jnp.exp(g_last - gc)
            st = st * jnp.exp(g_last) + lax.dot_general(
                k_dec.astype(BF16), v_new.astype(BF16), (((0,), (0,)), ((), ())),
                preferred_element_type=F32)
            new_states.append(st)
            on = o * lax.rsqrt(jnp.mean(o * o, axis=-1, keepdims=True) + EPS) * nw_ref[...]
            zz = z_ref[0, pl.ds(r0, C), j * DN_V_DIM:(j + 1) * DN_V_DIM].astype(F32)
            o_ref[0, pl.ds(r0, C), j * DN_V_DIM:(j + 1) * DN_V_DIM] = (
                on * _silu(zz)).astype(o_ref.dtype)
        return tuple(new_states)

    zero = jnp.zeros((DN_K_DIM, DN_V_DIM), F32)
    lax.fori_loop(0, seq // C, body, (zero,) * V_PER_K)


def _deltanet(proj3, colp, rowp, conv_w, dn_norm_w):
    b, s, _ = proj3.shape
    vw = V_PER_K * DN_V_DIM
    q0 = P_DQ // DN_K_DIM
    k0 = P_DK // DN_K_DIM
    v0 = P_DV // vw
    z0 = P_DZ // vw
    nchunk = s // CHUNK
    return pl.pallas_call(
        _dnet_kernel,
        grid=(b, DN_K_HEADS),
        in_specs=[
            pl.BlockSpec((1, s, DN_K_DIM), lambda i, h: (i, 0, q0 + h)),
            pl.BlockSpec((1, s, DN_K_DIM), lambda i, h: (i, 0, k0 + h)),
            pl.BlockSpec((1, s, vw), lambda i, h: (i, 0, v0 + h)),
            pl.BlockSpec((1, s, vw), lambda i, h: (i, 0, z0 + h)),
            pl.BlockSpec((1, 1, s, 2 * V_PER_K), lambda i, h: (i, h, 0, 0)),
            pl.BlockSpec((1, 1, nchunk, V_PER_K, CHUNK), lambda i, h: (i, h, 0, 0, 0)),
            pl.BlockSpec((CONV_WIDTH, DN_K_DIM), lambda i, h: (0, h)),
            pl.BlockSpec((CONV_WIDTH, DN_K_DIM), lambda i, h: (0, DN_K_HEADS + h)),
            pl.BlockSpec((CONV_WIDTH, vw), lambda i, h: (0, DN_K_HEADS + h)),
            pl.BlockSpec((1, DN_V_DIM), lambda i, h: (0, 0)),
        ],
        out_specs=pl.BlockSpec((1, s, vw), lambda i, h: (i, 0, h)),
        out_shape=jax.ShapeDtypeStruct((b, s, DN_V_W), BF16),
        compiler_params=pltpu.CompilerParams(
            dimension_semantics=("parallel", "parallel")),
        name="dnet",
    )(proj3, proj3, proj3, proj3, colp, rowp, conv_w, conv_w, conv_w, dn_norm_w)


MERGE_TM = 512
MERGE_TN = 512


def _merge_kernel(oa_ref, od_ref, wa_ref, wd_ref, ga_ref, gd_ref, o_ref):
    ya = jnp.dot(oa_ref[...], wa_ref[...], preferred_element_type=F32)
    yd = jnp.dot(od_ref[...], wd_ref[...], preferred_element_type=F32)
    ga = jax.nn.sigmoid(ga_ref[...].astype(F32))
    gd = jax.nn.sigmoid(gd_ref[...].astype(F32))
    o_ref[...] = (ga * ya + gd * yd).astype(o_ref.dtype)


def _merge(oa, od, wa, wd, proj2):
    m = oa.shape[0]
    g0 = P_G // MERGE_TN
    g1 = (P_G + D_MODEL) // MERGE_TN
    return pl.pallas_call(
        _merge_kernel,
        grid=(m // MERGE_TM, D_MODEL // MERGE_TN),
        in_specs=[
            pl.BlockSpec((MERGE_TM, ATT_W), lambda i, j: (i, 0)),
            pl.BlockSpec((MERGE_TM, DN_V_W), lambda i, j: (i, 0)),
            pl.BlockSpec((ATT_W, MERGE_TN), lambda i, j: (0, j)),
            pl.BlockSpec((DN_V_W, MERGE_TN), lambda i, j: (0, j)),
            pl.BlockSpec((MERGE_TM, MERGE_TN), lambda i, j: (i, g0 + j)),
            pl.BlockSpec((MERGE_TM, MERGE_TN), lambda i, j: (i, g1 + j)),
        ],
        out_specs=pl.BlockSpec((MERGE_TM, MERGE_TN), lambda i, j: (i, j)),
        out_shape=jax.ShapeDtypeStruct((m, D_MODEL), BF16),
        compiler_params=pltpu.CompilerParams(
            dimension_semantics=("parallel", "parallel"),
            vmem_limit_bytes=V7X_VMEM_LIMIT),
        name="merge",
    )(oa, od, wa, wd, proj2, proj2)


OUT_TM = 512


def _out_kernel(m_ref, w_ref, x_ref, nw_ref, o_ref):
    y = x_ref[...] + jnp.dot(m_ref[...], w_ref[...], preferred_element_type=F32)
    ms = jnp.mean(y * y, axis=-1, keepdims=True)
    o_ref[...] = (y * lax.rsqrt(ms + EPS)) * nw_ref[...]


def _outproj(merged, w_out, x2, final_norm_w):
    m = x2.shape[0]
    return pl.pallas_call(
        _out_kernel,
        grid=(m // OUT_TM,),
        in_specs=[
            pl.BlockSpec((OUT_TM, D_MODEL), lambda i: (i, 0)),
            pl.BlockSpec((D_MODEL, D_MODEL), lambda i: (0, 0)),
            pl.BlockSpec((OUT_TM, D_MODEL), lambda i: (i, 0)),
            pl.BlockSpec((1, D_MODEL), lambda i: (0, 0)),
        ],
        out_specs=pl.BlockSpec((OUT_TM, D_MODEL), lambda i: (i, 0)),
        out_shape=jax.ShapeDtypeStruct((m, D_MODEL), F32),
        compiler_params=pltpu.CompilerParams(
            dimension_semantics=("parallel",),
            vmem_limit_bytes=V7X_VMEM_LIMIT),
        name="outproj",
    )(merged, w_out, x2, final_norm_w)


def _layer(x, norm_w, w_in, b_qkv, sinks, conv_w, a_log, dt_bias, dn_norm_w,
           w_att_branch, w_dn_branch, w_out, out_norm_w):
    b, s, d = x.shape
    m = b * s
    x2 = x.reshape(m, d)

    wk = w_in[:, OFF_AK:OFF_AV].reshape(d, N_KV_HEADS, 1, HEAD_DIM)
    wv = w_in[:, OFF_AV:OFF_AZ].reshape(d, N_KV_HEADS, 1, HEAD_DIM)
    w_kv = jnp.concatenate([wk, wv], axis=2).reshape(d, 2 * ATT_KV_W)
    w_main = jnp.concatenate(
        [w_in[:, :OFF_AK], w_kv, w_in[:, OFF_AZ:OFF_DB], w_in[:, OFF_G:]], axis=1).astype(BF16)
    w_small = jnp.pad(w_in[:, OFF_DB:OFF_G], ((0, 0), (0, SMALL_W - 2 * DN_V_HEADS))).astype(BF16)
    bk = b_qkv[ATT_Q_W:ATT_Q_W + ATT_KV_W].reshape(N_KV_HEADS, 1, HEAD_DIM)
    bv = b_qkv[ATT_Q_W + ATT_KV_W:].reshape(N_KV_HEADS, 1, HEAD_DIM)
    bias = jnp.concatenate([b_qkv[:ATT_Q_W], jnp.concatenate([bk, bv], axis=1).reshape(-1),
                            jnp.zeros((MAIN_W - P_AZ,), F32)]).reshape(1, MAIN_W)

    proj, small = _inproj(x2, norm_w.reshape(1, d), w_main, bias, w_small)
    proj3 = proj.reshape(b, s, MAIN_W)

    lane_pad = (DN_V_HEADS, SMALL_W - 2 * DN_V_HEADS)
    alog_row = jnp.pad(a_log, lane_pad).reshape(1, SMALL_W)
    dt_row = jnp.pad(dt_bias, lane_pad).reshape(1, SMALL_W)
    pack = _gates(small.reshape(b, s, SMALL_W), alog_row, dt_row)
    beta = pack[..., :DN_V_HEADS].reshape(b, s, DN_K_HEADS, V_PER_K)
    gcum = pack[..., DN_V_HEADS:2 * DN_V_HEADS].reshape(b, s, DN_K_HEADS, V_PER_K)
    colp = jnp.concatenate([beta, gcum], axis=-1).transpose(0, 2, 1, 3)
    rowp = gcum.reshape(b, s // CHUNK, CHUNK, DN_K_HEADS, V_PER_K).transpose(0, 3, 1, 4, 2)

    o_att = _attention(proj3, sinks)
    o_dn = _deltanet(proj3, colp, rowp, conv_w, dn_norm_w.reshape(1, DN_V_DIM))

    merged = _merge(o_att.reshape(m, ATT_W), o_dn.reshape(m, DN_V_W),
                    w_att_branch.astype(BF16), w_dn_branch.astype(BF16), proj)
    y = _outproj(merged, w_out.astype(BF16), x2, out_norm_w.reshape(1, d))
    return y.reshape(b, s, d)


def kernel(x, norm_w, w_in, b_qkv, sinks, conv_w, a_log, dt_bias, dn_norm_w,
           w_att_branch, w_dn_branch, w_out, final_norm_w):
    depth = norm_w.shape[0]
    assert depth == 1, "the final RMSNorm is fused into the single layer's output kernel"
    return _layer(x, norm_w[0], w_in[0], b_qkv[0], sinks[0], conv_w[0], a_log[0], dt_bias[0],
                  dn_norm_w[0], w_att_branch[0], w_dn_branch[0], w_out[0], final_norm_w)
```

```python
import functools

import jax
import jax.numpy as jnp
from jax import lax
from jax.experimental import pallas as pl
from jax.experimental.pallas import tpu as pltpu

F32 = jnp.float32
BF16 = jnp.bfloat16

D_MODEL = 2048
HEAD_DIM = 64
N_Q_HEADS = 32
N_KV_HEADS = 4
GROUP = N_Q_HEADS // N_KV_HEADS
WINDOW = 128
ATT_BLOCK = 128
ATT_Q_W = N_Q_HEADS * HEAD_DIM
ATT_KV_W = N_KV_HEADS * HEAD_DIM
ATT_W = ATT_Q_W

DN_K_HEADS = 16
DN_V_HEADS = 32
DN_K_DIM = 128
DN_V_DIM = 128
DN_QK_W = DN_K_HEADS * DN_K_DIM
DN_V_W = DN_V_HEADS * DN_V_DIM
DN_CONV_CH = 2 * DN_QK_W + DN_V_W
CONV_WIDTH = 4
CHUNK = 64
EPS = 1e-6

OFF_AQ = 0
OFF_AK = OFF_AQ + ATT_Q_W
OFF_AV = OFF_AK + ATT_KV_W
OFF_AZ = OFF_AV + ATT_KV_W
OFF_DQKV = OFF_AZ + ATT_W
OFF_DZ = OFF_DQKV + DN_CONV_CH
OFF_DB = OFF_DZ + DN_V_W
OFF_DA = OFF_DB + DN_V_HEADS
OFF_G = OFF_DA + DN_V_HEADS
IN_W = OFF_G + 2 * D_MODEL

MAIN_W = IN_W - 2 * DN_V_HEADS
P_AQ = 0
P_AZ = P_AQ + ATT_Q_W
P_DQ = P_AZ + ATT_W
P_DK = P_DQ + DN_QK_W
P_DV = P_DK + DN_QK_W
P_DZ = P_DV + DN_V_W
P_G = P_DZ + DN_V_W
P_KV = P_G + 2 * D_MODEL
SMALL_W = 128

V7X_VMEM_LIMIT = 56 * 1024 * 1024


def _silu(v):
    return v * jax.nn.sigmoid(v)


def _mm(a, b):
    return jnp.dot(a.astype(BF16), b.astype(BF16), preferred_element_type=F32)


INPROJ_TM = 1024
INPROJ_TN = 512
NORM_ROWS = 128


def _inproj_kernel(x_ref, nw_ref, w_ref, b_ref, ws_ref, o_ref, os_ref, h_ref):
    @pl.when(pl.program_id(1) == 0)
    def _():
        def body(r, carry):
            r0 = pl.multiple_of(r * NORM_ROWS, NORM_ROWS)
            xv = x_ref[pl.ds(r0, NORM_ROWS), :]
            ms = jnp.mean(xv * xv, axis=-1, keepdims=True)
            hv = (xv * lax.rsqrt(ms + EPS)) * nw_ref[...]
            h_ref[pl.ds(r0, NORM_ROWS), :] = hv.astype(BF16)
            return carry
        lax.fori_loop(0, INPROJ_TM // NORM_ROWS, body, 0)
        os_ref[...] = jnp.dot(h_ref[...], ws_ref[...], preferred_element_type=F32)

    acc = jnp.dot(h_ref[...], w_ref[...], preferred_element_type=F32)
    o_ref[...] = (acc + b_ref[...]).astype(o_ref.dtype)


def _inproj(x2, norm_w, w_main, bias, w_small):
    m = x2.shape[0]
    grid = (m // INPROJ_TM, MAIN_W // INPROJ_TN)
    return pl.pallas_call(
        _inproj_kernel,
        grid=grid,
        in_specs=[
            pl.BlockSpec((INPROJ_TM, D_MODEL), lambda i, j: (i, 0)),
            pl.BlockSpec((1, D_MODEL), lambda i, j: (0, 0)),
            pl.BlockSpec((D_MODEL, INPROJ_TN), lambda i, j: (0, j)),
            pl.BlockSpec((1, INPROJ_TN), lambda i, j: (0, j)),
            pl.BlockSpec((D_MODEL, SMALL_W), lambda i, j: (0, 0)),
        ],
        out_specs=[
            pl.BlockSpec((INPROJ_TM, INPROJ_TN), lambda i, j: (i, j)),
            pl.BlockSpec((INPROJ_TM, SMALL_W), lambda i, j: (i, 0)),
        ],
        out_shape=[
            jax.ShapeDtypeStruct((m, MAIN_W), BF16),
            jax.ShapeDtypeStruct((m, SMALL_W), F32),
        ],
        scratch_shapes=[pltpu.VMEM((INPROJ_TM, D_MODEL), BF16)],
        compiler_params=pltpu.CompilerParams(
            dimension_semantics=("parallel", "arbitrary"),
            vmem_limit_bytes=V7X_VMEM_LIMIT),
        name="inproj",
    )(x2, norm_w, w_main, bias, w_small)


def _gates_kernel(s_ref, alog_ref, dt_ref, o_ref):
    seq = s_ref.shape[1]
    row = lax.broadcasted_iota(jnp.int32, (CHUNK, SMALL_W), 0)
    lane = lax.broadcasted_iota(jnp.int32, (CHUNK, SMALL_W), 1)

    def body(c, carry):
        r0 = pl.multiple_of(c * CHUNK, CHUNK)
        xs = s_ref[0, pl.ds(r0, CHUNK), :]
        beta = jax.nn.sigmoid(xs)
        t = xs + dt_ref[...]
        softplus = jnp.maximum(t, 0.0) + jnp.log(1.0 + jnp.exp(-jnp.abs(t)))
        g = -jnp.exp(alog_ref[...]) * softplus
        shift = 1
        while shift < CHUNK:
            g = g + jnp.where(row >= shift, pltpu.roll(g, shift, axis=0), 0.0)
            shift *= 2
        o_ref[0, pl.ds(r0, CHUNK), :] = jnp.where(lane < DN_V_HEADS, beta, g)
        return carry

    lax.fori_loop(0, seq // CHUNK, body, 0)


def _gates(small3, alog_row, dt_row):
    b, s, _ = small3.shape
    return pl.pallas_call(
        _gates_kernel,
        grid=(b,),
        in_specs=[
            pl.BlockSpec((1, s, SMALL_W), lambda i: (i, 0, 0)),
            pl.BlockSpec((1, SMALL_W), lambda i: (0, 0)),
            pl.BlockSpec((1, SMALL_W), lambda i: (0, 0)),
        ],
        out_specs=pl.BlockSpec((1, s, SMALL_W), lambda i: (i, 0, 0)),
        out_shape=jax.ShapeDtypeStruct((b, s, SMALL_W), F32),
        compiler_params=pltpu.CompilerParams(dimension_semantics=("parallel",)),
        name="gates",
    )(small3, alog_row, dt_row)


PAIR_W = 2 * HEAD_DIM
KVH_W = GROUP * HEAD_DIM


def _attn_kernel(sinks_ref, q_ref, kvp_ref, kvc_ref, z_ref, o_ref):
    n = pl.program_id(1)
    h = pl.program_id(2)
    L = ATT_BLOCK
    lane = lax.broadcasted_iota(jnp.int32, (L, PAIR_W), 1)
    lo = lane < HEAD_DIM

    def prep(kv_ref):
        kv = kv_ref[0].astype(F32)
        sw = pltpu.roll(kv, HEAD_DIM, axis=1)
        kk = jnp.where(lo, kv, sw).astype(BF16)
        va = jnp.where(lo, sw, 0.0).astype(BF16)
        vb = jnp.where(lo, 0.0, kv).astype(BF16)
        return kk, va, vb

    kk_p, va_p, vb_p = prep(kvp_ref)
    kk_c, va_c, vb_c = prep(kvc_ref)
    kk = jnp.concatenate([kk_p, kk_c], axis=0)
    va = jnp.concatenate([va_p, va_c], axis=0)
    vb = jnp.concatenate([vb_p, vb_c], axis=0)

    qi = lax.broadcasted_iota(jnp.int32, (L, 2 * L), 0)
    ki = lax.broadcasted_iota(jnp.int32, (L, 2 * L), 1)
    rel = qi + L - ki
    valid = (rel >= 0) & (rel < WINDOW) & ((ki >= L) | (n > 0))

    def probs(qm, sink):
        s = lax.dot_general(qm, kk, (((1,), (1,)), ((), ())), preferred_element_type=F32)
        s = jnp.where(valid, s, -jnp.inf)
        m = jnp.maximum(jnp.max(s, axis=-1, keepdims=True), sink)
        p = jnp.exp(s - m)
        denom = jnp.sum(p, axis=-1, keepdims=True) + jnp.exp(sink - m)
        return p.astype(BF16), 1.0 / denom

    for pr in range(GROUP // 2):
        cols = slice(pr * PAIR_W, (pr + 1) * PAIR_W)
        q2 = q_ref[0, :, cols].astype(F32) * (HEAD_DIM ** -0.5)
        qa = jnp.where(lo, q2, 0.0).astype(BF16)
        qb = jnp.where(lo, 0.0, q2).astype(BF16)
        head = h * GROUP + 2 * pr
        pa, inv_a = probs(qa, sinks_ref[head])
        pb, inv_b = probs(qb, sinks_ref[head + 1])
        o2 = (jnp.dot(pa, va, preferred_element_type=F32)
              + jnp.dot(pb, vb, preferred_element_type=F32))
        o2 = o2 * jnp.where(lo, inv_a, inv_b)
        zz = z_ref[0, :, cols].astype(F32)
        o_ref[0, :, cols] = (o2 * _silu(zz)).astype(o_ref.dtype)


def _attention(proj3, sinks):
    b, s, _ = proj3.shape
    nb = s // ATT_BLOCK
    kv0 = P_KV // PAIR_W
    z0 = P_AZ // KVH_W
    return pl.pallas_call(
        _attn_kernel,
        grid=(b, nb, N_KV_HEADS),
        in_specs=[
            pl.BlockSpec(memory_space=pltpu.SMEM),
            pl.BlockSpec((1, ATT_BLOCK, KVH_W), lambda i, n, h: (i, n, h)),
            pl.BlockSpec((1, ATT_BLOCK, PAIR_W), lambda i, n, h: (i, jnp.maximum(n - 1, 0), kv0 + h)),
            pl.BlockSpec((1, ATT_BLOCK, PAIR_W), lambda i, n, h: (i, n, kv0 + h)),
            pl.BlockSpec((1, ATT_BLOCK, KVH_W), lambda i, n, h: (i, n, z0 + h)),
        ],
        out_specs=pl.BlockSpec((1, ATT_BLOCK, KVH_W), lambda i, n, h: (i, n, h)),
        out_shape=jax.ShapeDtypeStruct((b, s, ATT_W), BF16),
        compiler_params=pltpu.CompilerParams(
            dimension_semantics=("parallel", "parallel", "parallel")),
        name="attn",
    )(sinks, proj3, proj3, proj3, proj3)


V_PER_K = DN_V_HEADS // DN_K_HEADS
assert V_PER_K == 2 and V_PER_K * CHUNK == 128, "two value heads are packed side by side in 128 lanes"
VW = V_PER_K * DN_V_DIM
DN_HG = 4
DN_TS = 1024
DN_A_CHUNKS = 2
CONV_HALO = 8
HALO_BLK = 16


def _dnet_kernel(q_ref, k_ref, v_ref, z_ref, qh_ref, kh_ref, vh_ref, col_ref, row_ref,
                 cwq_ref, cwk_ref, cwv_ref, nw_ref, o_ref,
                 st_ref, u_ref, w_ref, qe_ref, kd_ref, qkd_ref):
    t = pl.program_id(2)
    C = CHUNK
    ri = lax.broadcasted_iota(jnp.int32, (C, 2 * C), 0)
    li = lax.broadcasted_iota(jnp.int32, (C, 2 * C), 1)
    lo = li < C
    ci = jnp.where(lo, li, li - C)
    tril = ri >= ci
    strict = ri > ci
    eye = (ri == ci).astype(F32)
    top = lax.broadcasted_iota(jnp.int32, (2 * C, 2 * C), 0) < C
    left = lax.broadcasted_iota(jnp.int32, (2 * C, 2 * C), 1) < C
    diag_blk = top == left

    @pl.when(t == 0)
    def _():
        st_ref[...] = jnp.zeros_like(st_ref)

    def conv_silu(ref, halo_ref, w_ref, cols, c, r0):
        width = cols.stop - cols.start
        cur = ref[0, pl.ds(r0, C), cols].astype(F32)
        p0 = pl.multiple_of(jnp.maximum(r0 - CONV_HALO, 0), CONV_HALO)
        inner = ref[0, pl.ds(p0, CONV_HALO), cols].astype(F32)
        outer = halo_ref[0, HALO_BLK - CONV_HALO:, cols].astype(F32)
        prev = jnp.where(c > 0, inner, jnp.where(t > 0, outer, 0.0))
        row = lax.broadcasted_iota(jnp.int32, (C, width), 0)
        pad = jnp.zeros((C - CONV_HALO, width), F32)
        w = w_ref[:, cols]
        y = None
        for i in range(CONV_WIDTH):
            sh = CONV_WIDTH - 1 - i
            if sh == 0:
                term = cur
            else:
                head = jnp.concatenate([pltpu.roll(prev, sh, axis=0), pad], axis=0)
                term = jnp.where(row < sh, head, pltpu.roll(cur, sh, axis=0))
            term = term * w[i:i + 1, :]
            y = term if y is None else y + term
        return _silu(y)

    def l2n(v):
        return v * lax.rsqrt(jnp.sum(v * v, axis=-1, keepdims=True) + EPS)

    def bcast(col):
        return jnp.broadcast_to(col, (C, 2 * C))

    def block_diag(m):
        return jnp.where(diag_blk, jnp.concatenate([m, m], axis=0), 0.0).astype(BF16)

    vslices = [slice(j * DN_V_DIM, (j + 1) * DN_V_DIM) for j in range(V_PER_K)]
    lane_sel = [lo, jnp.logical_not(lo)]


    def stage_a(it, carry):
        units = [(hh, it * DN_A_CHUNKS + cc) for cc in range(DN_A_CHUNKS) for hh in range(DN_HG)]
        r0s = [pl.multiple_of(c * C, C) for _, c in units]
        n = len(units)
        q, k, v = [], [], []
        for (hh, c), r0 in zip(units, r0s):
            kcols = slice(hh * DN_K_DIM, (hh + 1) * DN_K_DIM)
            vcols = slice(hh * VW, (hh + 1) * VW)
            q.append(l2n(conv_silu(q_ref, qh_ref, cwq_ref, kcols, c, r0)) * (DN_K_DIM ** -0.5))
            k.append(l2n(conv_silu(k_ref, kh_ref, cwk_ref, kcols, c, r0)))
            v.append(conv_silu(v_ref, vh_ref, cwv_ref, vcols, c, r0))
        qk_kk = []
        for i in range(n):
            kb = k[i].astype(BF16)
            qk_kk.append(lax.dot_general(jnp.concatenate([q[i].astype(BF16), kb], axis=0),
                                         jnp.concatenate([kb, kb], axis=0),
                                         (((1,), (1,)), ((), ())), preferred_element_type=F32))
        beta, gc, g_last, decay, a, x, p = [], [], [], [], [], [], []
        for i, ((hh, c), r0) in enumerate(zip(units, r0s)):
            colp = col_ref[0, hh, pl.ds(r0, C), :]
            beta.append([bcast(colp[:, j:j + 1]) for j in range(V_PER_K)])
            gc.append([bcast(colp[:, V_PER_K + j:V_PER_K + j + 1]) for j in range(V_PER_K)])
            gr = row_ref[0, hh, pl.ds(c, 1), :]
            g_last.append([gr[:, (j + 1) * C - 1:(j + 1) * C] for j in range(V_PER_K)])
            beta_p = jnp.where(lo, beta[i][0], beta[i][1])
            gc_p = jnp.where(lo, gc[i][0], gc[i][1])
            decay.append(jnp.exp(jnp.where(tril, gc_p - gr, -jnp.inf)))
            a.append(jnp.where(strict, beta_p * qk_kk[i][C:] * decay[i], 0.0))
        for i in range(n):
            x.append(eye - a[i])
            p.append(jnp.dot(a[i].astype(BF16), block_diag(a[i]), preferred_element_type=F32))
        n_sq = 2
        while n_sq < C:
            for i in range(n):
                rhs = jnp.concatenate([block_diag(p[i]), block_diag(x[i])], axis=1)
                pp_px = jnp.dot(p[i].astype(BF16), rhs, preferred_element_type=F32)
                x[i] = x[i] + pp_px[:, 2 * C:]
                p[i] = pp_px[:, :2 * C]
            n_sq *= 2
        sols = []
        for i in range(n):
            eg = [jnp.exp(gc[i][j]) for j in range(V_PER_K)]
            rhs = jnp.concatenate(
                [jnp.concatenate([v[i][:, vslices[j]] * beta[i][j],
                                  k[i] * (beta[i][j] * eg[j])], axis=1) for j in range(V_PER_K)],
                axis=0).astype(BF16)
            sols.append([jnp.dot(jnp.where(lane_sel[j], x[i], 0.0).astype(BF16), rhs,
                                 preferred_element_type=F32) for j in range(V_PER_K)])
            hh, r0 = units[i][0], r0s[i]
            qe_ref[hh, pl.ds(r0, C), :] = jnp.concatenate(
                [q[i] * eg[j] for j in range(V_PER_K)], axis=1).astype(BF16)
            kd_ref[hh, pl.ds(r0, C), :] = jnp.concatenate(
                [k[i] * jnp.exp(g_last[i][j] - gc[i][j]) for j in range(V_PER_K)],
                axis=1).astype(BF16)
            qkd_ref[hh, pl.ds(r0, C), :] = (qk_kk[i][:C] * decay[i]).astype(BF16)
        for i in range(n):
            hh, r0 = units[i][0], r0s[i]
            u_ref[hh, pl.ds(r0, C), :] = jnp.concatenate(
                [s[:, :DN_V_DIM] for s in sols[i]], axis=1)
            w_ref[hh, pl.ds(r0, C), :] = jnp.concatenate(
                [s[:, DN_V_DIM:] for s in sols[i]], axis=1).astype(BF16)
        return carry

    def stage_b(c, carry):
        r0 = pl.multiple_of(c * C, C)
        heads = [(hh, j) for hh in range(DN_HG) for j in range(V_PER_K)]
        ws_qs = []
        for hh, j in heads:
            stb = st_ref[hh * V_PER_K + j].astype(BF16)
            ws_qs.append(jnp.dot(jnp.concatenate([w_ref[hh, pl.ds(r0, C), vslices[j]],
                                                  qe_ref[hh, pl.ds(r0, C), vslices[j]]], axis=0),
                                 stb, preferred_element_type=F32))
        v_new = [(u_ref[hh, pl.ds(r0, C), vslices[j]] - ws_qs[i][:C]).astype(BF16)
                 for i, (hh, j) in enumerate(heads)]
        for i, (hh, j) in enumerate(heads):
            g_last = row_ref[0, hh, pl.ds(c, 1), (j + 1) * C - 1:(j + 1) * C]
            st_ref[hh * V_PER_K + j] = (
                st_ref[hh * V_PER_K + j] * jnp.exp(g_last)
                + lax.dot_general(kd_ref[hh, pl.ds(r0, C), vslices[j]], v_new[i],
                                  (((0,), (0,)), ((), ())), preferred_element_type=F32))
        for i, (hh, j) in enumerate(heads):
            qkd = qkd_ref[hh, pl.ds(r0, C), :]
            qkd_j = jnp.where(lane_sel[j], qkd, jnp.zeros_like(qkd))
            v_both = jnp.concatenate([v_new[i - j], v_new[i - j + 1]], axis=0)
            o = ws_qs[i][C:] + jnp.dot(qkd_j, v_both, preferred_element_type=F32)
            on = o * lax.rsqrt(jnp.mean(o * o, axis=-1, keepdims=True) + EPS) * nw_ref[...]
            ocols = slice(hh * VW + j * DN_V_DIM, hh * VW + (j + 1) * DN_V_DIM)
            zz = z_ref[0, pl.ds(r0, C), ocols].astype(F32)
            o_ref[0, pl.ds(r0, C), ocols] = (on * _silu(zz)).astype(o_ref.dtype)
        return carry

    lax.fori_loop(0, DN_TS // (C * DN_A_CHUNKS), stage_a, 0)
    lax.fori_loop(0, DN_TS // C, stage_b, 0)


def _deltanet(proj3, colp, rowp, conv_w, dn_norm_w):
    b, s, _ = proj3.shape
    kw = DN_HG * DN_K_DIM
    vw = DN_HG * VW
    q0, k0, v0, z0 = P_DQ // kw, P_DK // kw, P_DV // vw, P_DZ // vw
    ck0, cv0 = DN_QK_W // kw, 2 * DN_QK_W // vw
    hb = DN_TS // HALO_BLK

    def halo(t):
        return jnp.maximum(t * hb - 1, 0)

    return pl.pallas_call(
        _dnet_kernel,
        grid=(b, DN_K_HEADS // DN_HG, s // DN_TS),
        in_specs=[
            pl.BlockSpec((1, DN_TS, kw), lambda i, g, t: (i, t, q0 + g)),
            pl.BlockSpec((1, DN_TS, kw), lambda i, g, t: (i, t, k0 + g)),
            pl.BlockSpec((1, DN_TS, vw), lambda i, g, t: (i, t, v0 + g)),
            pl.BlockSpec((1, DN_TS, vw), lambda i, g, t: (i, t, z0 + g)),
            pl.BlockSpec((1, HALO_BLK, kw), lambda i, g, t: (i, halo(t), q0 + g)),
            pl.BlockSpec((1, HALO_BLK, kw), lambda i, g, t: (i, halo(t), k0 + g)),
            pl.BlockSpec((1, HALO_BLK, vw), lambda i, g, t: (i, halo(t), v0 + g)),
            pl.BlockSpec((1, DN_HG, DN_TS, 2 * V_PER_K), lambda i, g, t: (i, g, t, 0)),
            pl.BlockSpec((1, DN_HG, DN_TS // CHUNK, V_PER_K * CHUNK), lambda i, g, t: (i, g, t, 0)),
            pl.BlockSpec((CONV_WIDTH, kw), lambda i, g, t: (0, g)),
            pl.BlockSpec((CONV_WIDTH, kw), lambda i, g, t: (0, ck0 + g)),
            pl.BlockSpec((CONV_WIDTH, vw), lambda i, g, t: (0, cv0 + g)),
            pl.BlockSpec((1, DN_V_DIM), lambda i, g, t: (0, 0)),
        ],
        out_specs=pl.BlockSpec((1, DN_TS, vw), lambda i, g, t: (i, t, g)),
        out_shape=jax.ShapeDtypeStruct((b, s, DN_V_W), BF16),
        scratch_shapes=[
            pltpu.VMEM((DN_HG * V_PER_K, DN_K_DIM, DN_V_DIM), F32),
            pltpu.VMEM((DN_HG, DN_TS, VW), F32),
            pltpu.VMEM((DN_HG, DN_TS, VW), BF16),
            pltpu.VMEM((DN_HG, DN_TS, VW), BF16),
            pltpu.VMEM((DN_HG, DN_TS, VW), BF16),
            pltpu.VMEM((DN_HG, DN_TS, V_PER_K * CHUNK), BF16),
        ],
        compiler_params=pltpu.CompilerParams(
            dimension_semantics=("parallel", "parallel", "arbitrary"),
            vmem_limit_bytes=V7X_VMEM_LIMIT),
        name="dnet",
    )(proj3, proj3, proj3, proj3, proj3, proj3, proj3, colp, rowp, conv_w, conv_w, conv_w,
      dn_norm_w)


MERGE_TM = 512
MERGE_TN = 512


def _merge_kernel(oa_ref, od_ref, wa_ref, wd_ref, ga_ref, gd_ref, o_ref):
    ya = jnp.dot(oa_ref[...], wa_ref[...], preferred_element_type=F32)
    yd = jnp.dot(od_ref[...], wd_ref[...], preferred_element_type=F32)
    ga = jax.nn.sigmoid(ga_ref[...].astype(F32))
    gd = jax.nn.sigmoid(gd_ref[...].astype(F32))
    o_ref[...] = (ga * ya + gd * yd).astype(o_ref.dtype)


def _merge(oa, od, wa, wd, proj2):
    m = oa.shape[0]
    g0 = P_G // MERGE_TN
    g1 = (P_G + D_MODEL) // MERGE_TN
    return pl.pallas_call(
        _merge_kernel,
        grid=(m // MERGE_TM, D_MODEL // MERGE_TN),
        in_specs=[
            pl.BlockSpec((MERGE_TM, ATT_W), lambda i, j: (i, 0)),
            pl.BlockSpec((MERGE_TM, DN_V_W), lambda i, j: (i, 0)),
            pl.BlockSpec((ATT_W, MERGE_TN), lambda i, j: (0, j)),
            pl.BlockSpec((DN_V_W, MERGE_TN), lambda i, j: (0, j)),
            pl.BlockSpec((MERGE_TM, MERGE_TN), lambda i, j: (i, g0 + j)),
            pl.BlockSpec((MERGE_TM, MERGE_TN), lambda i, j: (i, g1 + j)),
        ],
        out_specs=pl.BlockSpec((MERGE_TM, MERGE_TN), lambda i, j: (i, j)),
        out_shape=jax.ShapeDtypeStruct((m, D_MODEL), BF16),
        compiler_params=pltpu.CompilerParams(
            dimension_semantics=("parallel", "parallel"),
            vmem_limit_bytes=V7X_VMEM_LIMIT),
        name="merge",
    )(oa, od, wa, wd, proj2, proj2)


OUT_TM = 512


def _out_kernel(m_ref, w_ref, x_ref, nw_ref, o_ref):
    y = x_ref[...] + jnp.dot(m_ref[...], w_ref[...], preferred_element_type=F32)
    ms = jnp.mean(y * y, axis=-1, keepdims=True)
    o_ref[...] = (y * lax.rsqrt(ms + EPS)) * nw_ref[...]


def _outproj(merged, w_out, x2, final_norm_w):
    m = x2.shape[0]
    return pl.pallas_call(
        _out_kernel,
        grid=(m // OUT_TM,),
        in_specs=[
            pl.BlockSpec((OUT_TM, D_MODEL), lambda i: (i, 0)),
            pl.BlockSpec((D_MODEL, D_MODEL), lambda i: (0, 0)),
            pl.BlockSpec((OUT_TM, D_MODEL), lambda i: (i, 0)),
            pl.BlockSpec((1, D_MODEL), lambda i: (0, 0)),
        ],
        out_specs=pl.BlockSpec((OUT_TM, D_MODEL), lambda i: (i, 0)),
        out_shape=jax.ShapeDtypeStruct((m, D_MODEL), F32),
        compiler_params=pltpu.CompilerParams(
            dimension_semantics=("parallel",),
            vmem_limit_bytes=V7X_VMEM_LIMIT),
        name="outproj",
    )(merged, w_out, x2, final_norm_w)


def _layer(x, norm_w, w_in, b_qkv, sinks, conv_w, a_log, dt_bias, dn_norm_w,
           w_att_branch, w_dn_branch, w_out, out_norm_w):
    b, s, d = x.shape
    m = b * s
    x2 = x.reshape(m, d)

    wk = w_in[:, OFF_AK:OFF_AV].reshape(d, N_KV_HEADS, 1, HEAD_DIM)
    wv = w_in[:, OFF_AV:OFF_AZ].reshape(d, N_KV_HEADS, 1, HEAD_DIM)
    w_kv = jnp.concatenate([wk, wv], axis=2).reshape(d, 2 * ATT_KV_W)
    w_main = jnp.concatenate(
        [w_in[:, :OFF_AK], w_in[:, OFF_AZ:OFF_DB], w_in[:, OFF_G:], w_kv], axis=1).astype(BF16)
    w_small = jnp.pad(w_in[:, OFF_DB:OFF_G], ((0, 0), (0, SMALL_W - 2 * DN_V_HEADS))).astype(BF16)
    bk = b_qkv[ATT_Q_W:ATT_Q_W + ATT_KV_W].reshape(N_KV_HEADS, 1, HEAD_DIM)
    bv = b_qkv[ATT_Q_W + ATT_KV_W:].reshape(N_KV_HEADS, 1, HEAD_DIM)
    bias = jnp.concatenate([b_qkv[:ATT_Q_W], jnp.zeros((P_KV - P_AZ,), F32),
                            jnp.concatenate([bk, bv], axis=1).reshape(-1)]).reshape(1, MAIN_W)

    proj, small = _inproj(x2, norm_w.reshape(1, d), w_main, bias, w_small)
    proj3 = proj.reshape(b, s, MAIN_W)

    lane_pad = (DN_V_HEADS, SMALL_W - 2 * DN_V_HEADS)
    alog_row = jnp.pad(a_log, lane_pad).reshape(1, SMALL_W)
    dt_row = jnp.pad(dt_bias, lane_pad).reshape(1, SMALL_W)
    pack = _gates(small.reshape(b, s, SMALL_W), alog_row, dt_row)
    beta = pack[..., :DN_V_HEADS].reshape(b, s, DN_K_HEADS, V_PER_K)
    gcum = pack[..., DN_V_HEADS:2 * DN_V_HEADS].reshape(b, s, DN_K_HEADS, V_PER_K)
    colp = jnp.concatenate([beta, gcum], axis=-1).transpose(0, 2, 1, 3)
    rowp = gcum.reshape(b, s // CHUNK, CHUNK, DN_K_HEADS, V_PER_K).transpose(0, 3, 1, 4, 2)
    rowp = rowp.reshape(b, DN_K_HEADS, s // CHUNK, V_PER_K * CHUNK)

    o_att = _attention(proj3, sinks)
    o_dn = _deltanet(proj3, colp, rowp, conv_w, dn_norm_w.reshape(1, DN_V_DIM))

    merged = _merge(o_att.reshape(m, ATT_W), o_dn.reshape(m, DN_V_W),
                    w_att_branch.astype(BF16), w_dn_branch.astype(BF16), proj)
    y = _outproj(merged, w_out.astype(BF16), x2, out_norm_w.reshape(1, d))
    return y.reshape(b, s, d)


def kernel(x, norm_w, w_in, b_qkv, sinks, conv_w, a_log, dt_bias, dn_norm_w,
           w_att_branch, w_dn_branch, w_out, final_norm_w):
    depth = norm_w.shape[0]
    assert depth == 1, "the final RMSNorm is fused into the single layer's output kernel"
    return _layer(x, norm_w[0], w_in[0], b_qkv[0], sinks[0], conv_w[0], a_log[0], dt_bias[0],
                  dn_norm_w[0], w_att_branch[0], w_dn_branch[0], w_out[0], final_norm_w)
```

```python
import functools

import jax
import jax.numpy as jnp
from jax import lax
from jax.experimental import pallas as pl
from jax.experimental.pallas import tpu as pltpu

F32 = jnp.float32
BF16 = jnp.bfloat16

D_MODEL = 2048
HEAD_DIM = 64
N_Q_HEADS = 32
N_KV_HEADS = 4
GROUP = N_Q_HEADS // N_KV_HEADS
WINDOW = 128
ATT_BLOCK = 128
ATT_Q_W = N_Q_HEADS * HEAD_DIM
ATT_KV_W = N_KV_HEADS * HEAD_DIM
ATT_W = ATT_Q_W

DN_K_HEADS = 16
DN_V_HEADS = 32
DN_K_DIM = 128
DN_V_DIM = 128
DN_QK_W = DN_K_HEADS * DN_K_DIM
DN_V_W = DN_V_HEADS * DN_V_DIM
DN_CONV_CH = 2 * DN_QK_W + DN_V_W
CONV_WIDTH = 4
CHUNK = 64
EPS = 1e-6

OFF_AQ = 0
OFF_AK = OFF_AQ + ATT_Q_W
OFF_AV = OFF_AK + ATT_KV_W
OFF_AZ = OFF_AV + ATT_KV_W
OFF_DQKV = OFF_AZ + ATT_W
OFF_DZ = OFF_DQKV + DN_CONV_CH
OFF_DB = OFF_DZ + DN_V_W
OFF_DA = OFF_DB + DN_V_HEADS
OFF_G = OFF_DA + DN_V_HEADS
IN_W = OFF_G + 2 * D_MODEL

MAIN_W = IN_W - 2 * DN_V_HEADS
P_AQ = 0
P_AZ = P_AQ + ATT_Q_W
P_DQ = P_AZ + ATT_W
P_DK = P_DQ + DN_QK_W
P_DV = P_DK + DN_QK_W
P_DZ = P_DV + DN_V_W
P_G = P_DZ + DN_V_W
P_KV = P_G + 2 * D_MODEL
SMALL_W = 128

V7X_VMEM_LIMIT = 56 * 1024 * 1024


def _silu(v):
    return v * jax.nn.sigmoid(v)


def _mm(a, b):
    return jnp.dot(a.astype(BF16), b.astype(BF16), preferred_element_type=F32)


INPROJ_TM = 1024
INPROJ_TN = 512
NORM_ROWS = 128


def _inproj_kernel(x_ref, nw_ref, wa_ref, wt_ref, b_ref, ws_ref, o_ref, os_ref, h_ref):
    j = pl.program_id(1)

    @pl.when(j == 0)
    def _():
        def body(r, carry):
            r0 = pl.multiple_of(r * NORM_ROWS, NORM_ROWS)
            xv = x_ref[pl.ds(r0, NORM_ROWS), :]
            ms = jnp.mean(xv * xv, axis=-1, keepdims=True)
            hv = (xv * lax.rsqrt(ms + EPS)) * nw_ref[...]
            h_ref[pl.ds(r0, NORM_ROWS), :] = hv.astype(BF16)
            return carry
        lax.fori_loop(0, INPROJ_TM // NORM_ROWS, body, 0)
        os_ref[...] = jnp.dot(h_ref[...], ws_ref[...], preferred_element_type=F32)

    def project(w_ref):
        acc = jnp.dot(h_ref[...], w_ref[...], preferred_element_type=F32)
        o_ref[...] = (acc + b_ref[...]).astype(o_ref.dtype)

    @pl.when(j < HEAD_BLOCKS)
    def _():
        project(wa_ref)

    @pl.when(j >= HEAD_BLOCKS)
    def _():
        project(wt_ref)


HEAD_BLOCKS = P_G // INPROJ_TN
SKIP_FROM = OFF_AK // INPROJ_TN


def _inproj(x2, norm_w, w_all, w_tail, bias, w_small):
    m = x2.shape[0]
    grid = (m // INPROJ_TM, MAIN_W // INPROJ_TN)

    def head_block(i, j):
        return 0, jnp.where(j < SKIP_FROM, j, jnp.minimum(j, HEAD_BLOCKS - 1) + 1)

    return pl.pallas_call(
        _inproj_kernel,
        grid=grid,
        in_specs=[
            pl.BlockSpec((INPROJ_TM, D_MODEL), lambda i, j: (i, 0)),
            pl.BlockSpec((1, D_MODEL), lambda i, j: (0, 0)),
            pl.BlockSpec((D_MODEL, INPROJ_TN), head_block),
            pl.BlockSpec((D_MODEL, INPROJ_TN), lambda i, j: (0, jnp.maximum(j - HEAD_BLOCKS, 0))),
            pl.BlockSpec((1, INPROJ_TN), lambda i, j: (0, j)),
            pl.BlockSpec((D_MODEL, SMALL_W), lambda i, j: (0, 0)),
        ],
        out_specs=[
            pl.BlockSpec((INPROJ_TM, INPROJ_TN), lambda i, j: (i, j)),
            pl.BlockSpec((INPROJ_TM, SMALL_W), lambda i, j: (i, 0)),
        ],
        out_shape=[
            jax.ShapeDtypeStruct((m, MAIN_W), BF16),
            jax.ShapeDtypeStruct((m, SMALL_W), F32),
        ],
        scratch_shapes=[pltpu.VMEM((INPROJ_TM, D_MODEL), BF16)],
        compiler_params=pltpu.CompilerParams(
            dimension_semantics=("parallel", "arbitrary"),
            vmem_limit_bytes=V7X_VMEM_LIMIT),
        name="inproj",
    )(x2, norm_w, w_all, w_tail, bias, w_small)


GATE_ROWS = 2 * CHUNK
DN_HG = 4
GATE_GROUPS = DN_K_HEADS // DN_HG
GROUP_VH = DN_V_HEADS // GATE_GROUPS


def _gates_kernel(s_ref, alog_ref, dt_ref, col_ref, row_ref):
    seq = s_ref.shape[1]
    row = lax.broadcasted_iota(jnp.int32, (CHUNK, SMALL_W), 0)
    lane = lax.broadcasted_iota(jnp.int32, (GATE_ROWS, SMALL_W), 1)

    def body(blk, carry):
        r0 = pl.multiple_of(blk * GATE_ROWS, GATE_ROWS)
        xs = s_ref[0, pl.ds(r0, GATE_ROWS), :]
        beta = jax.nn.sigmoid(xs)
        t = xs + dt_ref[...]
        softplus = jnp.maximum(t, 0.0) + jnp.log(1.0 + jnp.exp(-jnp.abs(t)))
        g = -jnp.exp(alog_ref[...]) * softplus
        halves = []
        for part in range(GATE_ROWS // CHUNK):
            gp = g[part * CHUNK:(part + 1) * CHUNK]
            shift = 1
            while shift < CHUNK:
                gp = gp + jnp.where(row >= shift, pltpu.roll(gp, shift, axis=0), 0.0)
                shift *= 2
            halves.append(gp)
        gcum = jnp.concatenate(halves, axis=0)
        for grp in range(GATE_GROUPS):
            b_g = pltpu.roll(beta, (SMALL_W - grp * GROUP_VH) % SMALL_W, axis=1)
            g_g = pltpu.roll(gcum, SMALL_W - DN_V_HEADS - grp * GROUP_VH + GROUP_VH, axis=1)
            col_ref[0, grp, pl.ds(r0, GATE_ROWS), :] = jnp.where(lane < GROUP_VH, b_g, g_g)
        row_ref[0, blk] = gcum.T[DN_V_HEADS:2 * DN_V_HEADS]
        return carry

    lax.fori_loop(0, seq // GATE_ROWS, body, 0)


def _gates(small3, alog_row, dt_row):
    b, s, _ = small3.shape
    return pl.pallas_call(
        _gates_kernel,
        grid=(b,),
        in_specs=[
            pl.BlockSpec((1, s, SMALL_W), lambda i: (i, 0, 0)),
            pl.BlockSpec((1, SMALL_W), lambda i: (0, 0)),
            pl.BlockSpec((1, SMALL_W), lambda i: (0, 0)),
        ],
        out_specs=[
            pl.BlockSpec((1, GATE_GROUPS, s, SMALL_W), lambda i: (i, 0, 0, 0)),
            pl.BlockSpec((1, s // GATE_ROWS, DN_V_HEADS, GATE_ROWS), lambda i: (i, 0, 0, 0)),
        ],
        out_shape=[
            jax.ShapeDtypeStruct((b, GATE_GROUPS, s, SMALL_W), F32),
            jax.ShapeDtypeStruct((b, s // GATE_ROWS, DN_V_HEADS, GATE_ROWS), F32),
        ],
        compiler_params=pltpu.CompilerParams(dimension_semantics=("parallel",)),
        name="gates",
    )(small3, alog_row, dt_row)


PAIR_W = 2 * HEAD_DIM
KVH_W = GROUP * HEAD_DIM


def _attn_kernel(sinks_ref, q_ref, kvp_ref, kvc_ref, z_ref, o_ref):
    n = pl.program_id(1)
    L = ATT_BLOCK
    lane = lax.broadcasted_iota(jnp.int32, (L, PAIR_W), 1)
    lo = lane < HEAD_DIM

    def prep(kv):
        kv = kv.astype(F32)
        sw = pltpu.roll(kv, HEAD_DIM, axis=1)
        kk = jnp.where(lo, kv, sw).astype(BF16)
        va = jnp.where(lo, sw, 0.0).astype(BF16)
        vb = jnp.where(lo, 0.0, kv).astype(BF16)
        return kk, va, vb

    qi = lax.broadcasted_iota(jnp.int32, (L, 2 * L), 0)
    ki = lax.broadcasted_iota(jnp.int32, (L, 2 * L), 1)
    rel = qi + L - ki
    valid = (rel >= 0) & (rel < WINDOW) & ((ki >= L) | (n > 0))

    for h in range(N_KV_HEADS):
        kcols = slice(h * PAIR_W, (h + 1) * PAIR_W)
        kk_p, va_p, vb_p = prep(kvp_ref[0, :, kcols])
        kk_c, va_c, vb_c = prep(kvc_ref[0, :, kcols])
        kk = jnp.concatenate([kk_p, kk_c], axis=0)
        va = jnp.concatenate([va_p, va_c], axis=0)
        vb = jnp.concatenate([vb_p, vb_c], axis=0)
        pair_cols = [slice(h * KVH_W + pr * PAIR_W, h * KVH_W + (pr + 1) * PAIR_W)
                     for pr in range(GROUP // 2)]
        scores = []
        for cols in pair_cols:
            q2 = q_ref[0, :, cols].astype(F32) * (HEAD_DIM ** -0.5)
            for qm in (jnp.where(lo, q2, 0.0), jnp.where(lo, 0.0, q2)):
                scores.append(lax.dot_general(qm.astype(BF16), kk, (((1,), (1,)), ((), ())),
                                              preferred_element_type=F32))
        probs, inv = [], []
        for i, s in enumerate(scores):
            sink = sinks_ref[h * GROUP + i]
            s = jnp.where(valid, s, -jnp.inf)
            m = jnp.maximum(jnp.max(s, axis=-1, keepdims=True), sink)
            p = jnp.exp(s - m)
            inv.append(1.0 / (jnp.sum(p, axis=-1, keepdims=True) + jnp.exp(sink - m)))
            probs.append(p.astype(BF16))
        for pr, cols in enumerate(pair_cols):
            o2 = (jnp.dot(probs[2 * pr], va, preferred_element_type=F32)
                  + jnp.dot(probs[2 * pr + 1], vb, preferred_element_type=F32))
            o2 = o2 * jnp.where(lo, inv[2 * pr], inv[2 * pr + 1])
            zz = z_ref[0, :, cols].astype(F32)
            o_ref[0, :, cols] = (o2 * _silu(zz)).astype(o_ref.dtype)


def _attention(proj3, sinks):
    b, s, _ = proj3.shape
    nb = s // ATT_BLOCK
    kvw = 2 * ATT_KV_W
    kv0 = P_KV // kvw
    z0 = P_AZ // ATT_W
    return pl.pallas_call(
        _attn_kernel,
        grid=(b, nb),
        in_specs=[
            pl.BlockSpec(memory_space=pltpu.SMEM),
            pl.BlockSpec((1, ATT_BLOCK, ATT_Q_W), lambda i, n: (i, n, 0)),
            pl.BlockSpec((1, ATT_BLOCK, kvw), lambda i, n: (i, jnp.maximum(n - 1, 0), kv0)),
            pl.BlockSpec((1, ATT_BLOCK, kvw), lambda i, n: (i, n, kv0)),
            pl.BlockSpec((1, ATT_BLOCK, ATT_W), lambda i, n: (i, n, z0)),
        ],
        out_specs=pl.BlockSpec((1, ATT_BLOCK, ATT_W), lambda i, n: (i, n, 0)),
        out_shape=jax.ShapeDtypeStruct((b, s, ATT_W), BF16),
        compiler_params=pltpu.CompilerParams(
            dimension_semantics=("parallel", "parallel")),
        name="attn",
    )(sinks, proj3, proj3, proj3, proj3)


V_PER_K = DN_V_HEADS // DN_K_HEADS
assert V_PER_K == 2 and V_PER_K * CHUNK == 128, "two value heads are packed side by side in 128 lanes"
VW = V_PER_K * DN_V_DIM
DN_TS = 1024
DN_A_CHUNKS = GATE_ROWS // CHUNK
CONV_HALO = 8
HALO_BLK = 16


def _dnet_kernel(q_ref, k_ref, v_ref, z_ref, qh_ref, kh_ref, vh_ref, col_ref, row_ref,
                 cwq_ref, cwk_ref, cwv_ref, nw_ref, o_ref,
                 st_ref, u_ref, w_ref, qe_ref, kd_ref, qkd_ref):
    t = pl.program_id(2)
    C = CHUNK
    ri = lax.broadcasted_iota(jnp.int32, (C, 2 * C), 0)
    li = lax.broadcasted_iota(jnp.int32, (C, 2 * C), 1)
    lo = li < C
    ci = jnp.where(lo, li, li - C)
    tril = ri >= ci
    strict = ri > ci
    eye = (ri == ci).astype(F32)
    top = lax.broadcasted_iota(jnp.int32, (2 * C, 2 * C), 0) < C
    left = lax.broadcasted_iota(jnp.int32, (2 * C, 2 * C), 1) < C
    diag_blk = top == left

    @pl.when(t == 0)
    def _():
        st_ref[...] = jnp.zeros_like(st_ref)

    def conv_silu(ref, halo_ref, w_ref, cols, c, r0):
        width = cols.stop - cols.start
        cur = ref[0, pl.ds(r0, C), cols].astype(F32)
        p0 = pl.multiple_of(jnp.maximum(r0 - CONV_HALO, 0), CONV_HALO)
        inner = ref[0, pl.ds(p0, CONV_HALO), cols].astype(F32)
        outer = halo_ref[0, HALO_BLK - CONV_HALO:, cols].astype(F32)
        prev = jnp.where(c > 0, inner, jnp.where(t > 0, outer, 0.0))
        row = lax.broadcasted_iota(jnp.int32, (C, width), 0)
        pad = jnp.zeros((C - CONV_HALO, width), F32)
        w = w_ref[:, cols]
        y = None
        for i in range(CONV_WIDTH):
            sh = CONV_WIDTH - 1 - i
            if sh == 0:
                term = cur
            else:
                head = jnp.concatenate([pltpu.roll(prev, sh, axis=0), pad], axis=0)
                term = jnp.where(row < sh, head, pltpu.roll(cur, sh, axis=0))
            term = term * w[i:i + 1, :]
            y = term if y is None else y + term
        return _silu(y)

    def l2n(v):
        return v * lax.rsqrt(jnp.sum(v * v, axis=-1, keepdims=True) + EPS)

    def bcast(col):
        return jnp.broadcast_to(col, (C, 2 * C))

    def block_diag(m):
        return jnp.where(diag_blk, jnp.concatenate([m, m], axis=0), 0.0).astype(BF16)

    vslices = [slice(j * DN_V_DIM, (j + 1) * DN_V_DIM) for j in range(V_PER_K)]
    lane_sel = [lo, jnp.logical_not(lo)]


    def stage_a(it, carry):
        halves = [cc for cc in range(DN_A_CHUNKS) for _ in range(DN_HG)]
        units = [(hh, it * DN_A_CHUNKS + cc) for cc in range(DN_A_CHUNKS) for hh in range(DN_HG)]
        r0s = [pl.multiple_of(c * C, C) for _, c in units]
        n = len(units)
        g_rows = row_ref[0, it]
        q, k, v = [], [], []
        for (hh, c), r0 in zip(units, r0s):
            kcols = slice(hh * DN_K_DIM, (hh + 1) * DN_K_DIM)
            vcols = slice(hh * VW, (hh + 1) * VW)
            q.append(l2n(conv_silu(q_ref, qh_ref, cwq_ref, kcols, c, r0)) * (DN_K_DIM ** -0.5))
            k.append(l2n(conv_silu(k_ref, kh_ref, cwk_ref, kcols, c, r0)))
            v.append(conv_silu(v_ref, vh_ref, cwv_ref, vcols, c, r0))
        qk_kk = []
        for i in range(n):
            kb = k[i].astype(BF16)
            qk_kk.append(lax.dot_general(jnp.concatenate([q[i].astype(BF16), kb], axis=0),
                                         jnp.concatenate([kb, kb], axis=0),
                                         (((1,), (1,)), ((), ())), preferred_element_type=F32))
        beta, gc, g_last, decay, a, x, p = [], [], [], [], [], [], []
        for i, ((hh, c), r0) in enumerate(zip(units, r0s)):
            colp = col_ref[0, 0, pl.ds(r0, C), :]
            vh = [hh * V_PER_K + j for j in range(V_PER_K)]
            beta.append([bcast(colp[:, h:h + 1]) for h in vh])
            gc.append([bcast(colp[:, GROUP_VH + h:GROUP_VH + h + 1]) for h in vh])
            tok = slice(halves[i] * C, (halves[i] + 1) * C)
            gr = jnp.concatenate([g_rows[h:h + 1, tok] for h in vh], axis=1)
            g_last.append([gr[:, (j + 1) * C - 1:(j + 1) * C] for j in range(V_PER_K)])
            beta_p = jnp.where(lo, beta[i][0], beta[i][1])
            gc_p = jnp.where(lo, gc[i][0], gc[i][1])
            decay.append(jnp.exp(jnp.where(tril, gc_p - gr, -jnp.inf)))
            a.append(jnp.where(strict, beta_p * qk_kk[i][C:] * decay[i], 0.0))
        for i in range(n):
            x.append(eye - a[i])
            p.append(jnp.dot(a[i].astype(BF16), block_diag(a[i]), preferred_element_type=F32))
        n_sq = 2
        while n_sq < C:
            for i in range(n):
                rhs = jnp.concatenate([block_diag(p[i]), block_diag(x[i])], axis=1)
                pp_px = jnp.dot(p[i].astype(BF16), rhs, preferred_element_type=F32)
                x[i] = x[i] + pp_px[:, 2 * C:]
                p[i] = pp_px[:, :2 * C]
            n_sq *= 2
        sols = []
        for i in range(n):
            eg = [jnp.exp(gc[i][j]) for j in range(V_PER_K)]
            rhs = jnp.concatenate(
                [jnp.concatenate([v[i][:, vslices[j]] * beta[i][j],
                                  k[i] * (beta[i][j] * eg[j])], axis=1) for j in range(V_PER_K)],
                axis=0).astype(BF16)
            sols.append([jnp.dot(jnp.where(lane_sel[j], x[i], 0.0).astype(BF16), rhs,
                                 preferred_element_type=F32) for j in range(V_PER_K)])
            hh, r0 = units[i][0], r0s[i]
            qe_ref[hh, pl.ds(r0, C), :] = jnp.concatenate(
                [q[i] * eg[j] for j in range(V_PER_K)], axis=1).astype(BF16)
            kd_ref[hh, pl.ds(r0, C), :] = jnp.concatenate(
                [k[i] * jnp.exp(g_last[i][j] - gc[i][j]) for j in range(V_PER_K)],
                axis=1).astype(BF16)
            qkd_ref[hh, pl.ds(r0, C), :] = (qk_kk[i][:C] * decay[i]).astype(BF16)
        for i in range(n):
            hh, r0 = units[i][0], r0s[i]
            u_ref[hh, pl.ds(r0, C), :] = jnp.concatenate(
                [s[:, :DN_V_DIM] for s in sols[i]], axis=1)
            w_ref[hh, pl.ds(r0, C), :] = jnp.concatenate(
                [s[:, DN_V_DIM:] for s in sols[i]], axis=1).astype(BF16)
        return carry

    def stage_b(c, carry):
        r0 = pl.multiple_of(c * C, C)
        heads = [(hh, j) for hh in range(DN_HG) for j in range(V_PER_K)]
        ws_qs = []
        for hh, j in heads:
            stb = st_ref[hh * V_PER_K + j].astype(BF16)
            ws_qs.append(jnp.dot(jnp.concatenate([w_ref[hh, pl.ds(r0, C), vslices[j]],
                                                  qe_ref[hh, pl.ds(r0, C), vslices[j]]], axis=0),
                                 stb, preferred_element_type=F32))
        v_new = [(u_ref[hh, pl.ds(r0, C), vslices[j]] - ws_qs[i][:C]).astype(BF16)
                 for i, (hh, j) in enumerate(heads)]
        for i, (hh, j) in enumerate(heads):
            gl = GROUP_VH + hh * V_PER_K + j
            g_last = col_ref[0, 0, pl.ds(r0 + C - 1, 1), gl:gl + 1]
            st_ref[hh * V_PER_K + j] = (
                st_ref[hh * V_PER_K + j] * jnp.exp(g_last)
                + lax.dot_general(kd_ref[hh, pl.ds(r0, C), vslices[j]], v_new[i],
                                  (((0,), (0,)), ((), ())), preferred_element_type=F32))
        for i, (hh, j) in enumerate(heads):
            qkd = qkd_ref[hh, pl.ds(r0, C), :]
            qkd_j = jnp.where(lane_sel[j], qkd, jnp.zeros_like(qkd))
            v_both = jnp.concatenate([v_new[i - j], v_new[i - j + 1]], axis=0)
            o = ws_qs[i][C:] + jnp.dot(qkd_j, v_both, preferred_element_type=F32)
            on = o * lax.rsqrt(jnp.mean(o * o, axis=-1, keepdims=True) + EPS) * nw_ref[...]
            ocols = slice(hh * VW + j * DN_V_DIM, hh * VW + (j + 1) * DN_V_DIM)
            zz = z_ref[0, pl.ds(r0, C), ocols].astype(F32)
            o_ref[0, pl.ds(r0, C), ocols] = (on * _silu(zz)).astype(o_ref.dtype)
        return carry

    lax.fori_loop(0, DN_TS // (C * DN_A_CHUNKS), stage_a, 0)
    lax.fori_loop(0, DN_TS // C, stage_b, 0)


def _deltanet(proj3, colp, rowp, conv_w, dn_norm_w):
    b, s, _ = proj3.shape
    kw = DN_HG * DN_K_DIM
    vw = DN_HG * VW
    q0, k0, v0, z0 = P_DQ // kw, P_DK // kw, P_DV // vw, P_DZ // vw
    ck0, cv0 = DN_QK_W // kw, 2 * DN_QK_W // vw
    hb = DN_TS // HALO_BLK

    def halo(t):
        return jnp.maximum(t * hb - 1, 0)

    return pl.pallas_call(
        _dnet_kernel,
        grid=(b, DN_K_HEADS // DN_HG, s // DN_TS),
        in_specs=[
            pl.BlockSpec((1, DN_TS, kw), lambda i, g, t: (i, t, q0 + g)),
            pl.BlockSpec((1, DN_TS, kw), lambda i, g, t: (i, t, k0 + g)),
            pl.BlockSpec((1, DN_TS, vw), lambda i, g, t: (i, t, v0 + g)),
            pl.BlockSpec((1, DN_TS, vw), lambda i, g, t: (i, t, z0 + g)),
            pl.BlockSpec((1, HALO_BLK, kw), lambda i, g, t: (i, halo(t), q0 + g)),
            pl.BlockSpec((1, HALO_BLK, kw), lambda i, g, t: (i, halo(t), k0 + g)),
            pl.BlockSpec((1, HALO_BLK, vw), lambda i, g, t: (i, halo(t), v0 + g)),
            pl.BlockSpec((1, 1, DN_TS, SMALL_W), lambda i, g, t: (i, g, t, 0)),
            pl.BlockSpec((1, DN_TS // GATE_ROWS, GROUP_VH, GATE_ROWS), lambda i, g, t: (i, t, g, 0)),
            pl.BlockSpec((CONV_WIDTH, kw), lambda i, g, t: (0, g)),
            pl.BlockSpec((CONV_WIDTH, kw), lambda i, g, t: (0, ck0 + g)),
            pl.BlockSpec((CONV_WIDTH, vw), lambda i, g, t: (0, cv0 + g)),
            pl.BlockSpec((1, DN_V_DIM), lambda i, g, t: (0, 0)),
        ],
        out_specs=pl.BlockSpec((1, DN_TS, vw), lambda i, g, t: (i, t, g)),
        out_shape=jax.ShapeDtypeStruct((b, s, DN_V_W), BF16),
        scratch_shapes=[
            pltpu.VMEM((DN_HG * V_PER_K, DN_K_DIM, DN_V_DIM), F32),
            pltpu.VMEM((DN_HG, DN_TS, VW), F32),
            pltpu.VMEM((DN_HG, DN_TS, VW), BF16),
            pltpu.VMEM((DN_HG, DN_TS, VW), BF16),
            pltpu.VMEM((DN_HG, DN_TS, VW), BF16),
            pltpu.VMEM((DN_HG, DN_TS, V_PER_K * CHUNK), BF16),
        ],
        compiler_params=pltpu.CompilerParams(
            dimension_semantics=("parallel", "parallel", "arbitrary"),
            vmem_limit_bytes=V7X_VMEM_LIMIT),
        name="dnet",
    )(proj3, proj3, proj3, proj3, proj3, proj3, proj3, colp, rowp, conv_w, conv_w, conv_w,
      dn_norm_w)


MERGE_TM = 512
MERGE_TN = 512


def _merge_kernel(oa_ref, od_ref, wa_ref, wd_ref, ga_ref, gd_ref, o_ref):
    ya = jnp.dot(oa_ref[...], wa_ref[...], preferred_element_type=F32)
    yd = jnp.dot(od_ref[...], wd_ref[...], preferred_element_type=F32)
    ga = jax.nn.sigmoid(ga_ref[...].astype(F32))
    gd = jax.nn.sigmoid(gd_ref[...].astype(F32))
    o_ref[...] = (ga * ya + gd * yd).astype(o_ref.dtype)


def _merge(oa, od, wa, wd, proj2):
    m = oa.shape[0]
    g0 = P_G // MERGE_TN
    g1 = (P_G + D_MODEL) // MERGE_TN
    return pl.pallas_call(
        _merge_kernel,
        grid=(m // MERGE_TM, D_MODEL // MERGE_TN),
        in_specs=[
            pl.BlockSpec((MERGE_TM, ATT_W), lambda i, j: (i, 0)),
            pl.BlockSpec((MERGE_TM, DN_V_W), lambda i, j: (i, 0)),
            pl.BlockSpec((ATT_W, MERGE_TN), lambda i, j: (0, j)),
            pl.BlockSpec((DN_V_W, MERGE_TN), lambda i, j: (0, j)),
            pl.BlockSpec((MERGE_TM, MERGE_TN), lambda i, j: (i, g0 + j)),
            pl.BlockSpec((MERGE_TM, MERGE_TN), lambda i, j: (i, g1 + j)),
        ],
        out_specs=pl.BlockSpec((MERGE_TM, MERGE_TN), lambda i, j: (i, j)),
        out_shape=jax.ShapeDtypeStruct((m, D_MODEL), BF16),
        compiler_params=pltpu.CompilerParams(
            dimension_semantics=("parallel", "parallel"),
            vmem_limit_bytes=V7X_VMEM_LIMIT),
        name="merge",
    )(oa, od, wa, wd, proj2, proj2)


OUT_TM = 512


def _out_kernel(m_ref, w_ref, x_ref, nw_ref, o_ref):
    y = x_ref[...] + jnp.dot(m_ref[...], w_ref[...], preferred_element_type=F32)
    ms = jnp.mean(y * y, axis=-1, keepdims=True)
    o_ref[...] = (y * lax.rsqrt(ms + EPS)) * nw_ref[...]


def _outproj(merged, w_out, x2, final_norm_w):
    m = x2.shape[0]
    return pl.pallas_call(
        _out_kernel,
        grid=(m // OUT_TM,),
        in_specs=[
            pl.BlockSpec((OUT_TM, D_MODEL), lambda i: (i, 0)),
            pl.BlockSpec((D_MODEL, D_MODEL), lambda i: (0, 0)),
            pl.BlockSpec((OUT_TM, D_MODEL), lambda i: (i, 0)),
            pl.BlockSpec((1, D_MODEL), lambda i: (0, 0)),
        ],
        out_specs=pl.BlockSpec((OUT_TM, D_MODEL), lambda i: (i, 0)),
        out_shape=jax.ShapeDtypeStruct((m, D_MODEL), F32),
        compiler_params=pltpu.CompilerParams(
            dimension_semantics=("parallel",),
            vmem_limit_bytes=V7X_VMEM_LIMIT),
        name="outproj",
    )(merged, w_out, x2, final_norm_w)


def _layer(x, norm_w, w_in, b_qkv, sinks, conv_w, a_log, dt_bias, dn_norm_w,
           w_att_branch, w_dn_branch, w_out, out_norm_w):
    b, s, d = x.shape
    m = b * s
    x2 = x.reshape(m, d)

    wk = w_in[:, OFF_AK:OFF_AV].reshape(d, N_KV_HEADS, 1, HEAD_DIM)
    wv = w_in[:, OFF_AV:OFF_AZ].reshape(d, N_KV_HEADS, 1, HEAD_DIM)
    w_kv = jnp.concatenate([wk, wv], axis=2).reshape(d, 2 * ATT_KV_W)
    w_all = w_in.astype(BF16)
    w_tail = jnp.concatenate([w_in[:, OFF_G:], w_kv], axis=1).astype(BF16)
    w_small = jnp.pad(w_in[:, OFF_DB:OFF_G], ((0, 0), (0, SMALL_W - 2 * DN_V_HEADS))).astype(BF16)
    bk = b_qkv[ATT_Q_W:ATT_Q_W + ATT_KV_W].reshape(N_KV_HEADS, 1, HEAD_DIM)
    bv = b_qkv[ATT_Q_W + ATT_KV_W:].reshape(N_KV_HEADS, 1, HEAD_DIM)
    bias = jnp.concatenate([b_qkv[:ATT_Q_W], jnp.zeros((P_KV - P_AZ,), F32),
                            jnp.concatenate([bk, bv], axis=1).reshape(-1)]).reshape(1, MAIN_W)

    proj, small = _inproj(x2, norm_w.reshape(1, d), w_all, w_tail, bias, w_small)
    proj3 = proj.reshape(b, s, MAIN_W)

    lane_pad = (DN_V_HEADS, SMALL_W - 2 * DN_V_HEADS)
    alog_row = jnp.pad(a_log, lane_pad).reshape(1, SMALL_W)
    dt_row = jnp.pad(dt_bias, lane_pad).reshape(1, SMALL_W)
    colp, rowp = _gates(small.reshape(b, s, SMALL_W), alog_row, dt_row)

    o_att = _attention(proj3, sinks)
    o_dn = _deltanet(proj3, colp, rowp, conv_w, dn_norm_w.reshape(1, DN_V_DIM))

    merged = _merge(o_att.reshape(m, ATT_W), o_dn.reshape(m, DN_V_W),
                    w_att_branch.astype(BF16), w_dn_branch.astype(BF16), proj)
    y = _outproj(merged, w_out.astype(BF16), x2, out_norm_w.reshape(1, d))
    return y.reshape(b, s, d)


def kernel(x, norm_w, w_in, b_qkv, sinks, conv_w, a_log, dt_bias, dn_norm_w,
           w_att_branch, w_dn_branch, w_out, final_norm_w):
    depth = norm_w.shape[0]
    assert depth == 1, "the final RMSNorm is fused into the single layer's output kernel"
    return _layer(x, norm_w[0], w_in[0], b_qkv[0], sinks[0], conv_w[0], a_log[0], dt_bias[0],
                  dn_norm_w[0], w_att_branch[0], w_dn_branch[0], w_out[0], final_norm_w)
```

```python
import functools

import jax
import jax.numpy as jnp
from jax import lax
from jax.experimental import pallas as pl
from jax.experimental.pallas import tpu as pltpu

F32 = jnp.float32
BF16 = jnp.bfloat16

D_MODEL = 2048
HEAD_DIM = 64
N_Q_HEADS = 32
N_KV_HEADS = 4
GROUP = N_Q_HEADS // N_KV_HEADS
WINDOW = 128
ATT_BLOCK = 128
ATT_Q_W = N_Q_HEADS * HEAD_DIM
ATT_KV_W = N_KV_HEADS * HEAD_DIM
ATT_W = ATT_Q_W

DN_K_HEADS = 16
DN_V_HEADS = 32
DN_K_DIM = 128
DN_V_DIM = 128
DN_QK_W = DN_K_HEADS * DN_K_DIM
DN_V_W = DN_V_HEADS * DN_V_DIM
DN_CONV_CH = 2 * DN_QK_W + DN_V_W
CONV_WIDTH = 4
CHUNK = 64
EPS = 1e-6

OFF_AQ = 0
OFF_AK = OFF_AQ + ATT_Q_W
OFF_AV = OFF_AK + ATT_KV_W
OFF_AZ = OFF_AV + ATT_KV_W
OFF_DQKV = OFF_AZ + ATT_W
OFF_DZ = OFF_DQKV + DN_CONV_CH
OFF_DB = OFF_DZ + DN_V_W
OFF_DA = OFF_DB + DN_V_HEADS
OFF_G = OFF_DA + DN_V_HEADS
IN_W = OFF_G + 2 * D_MODEL

MAIN_W = IN_W - 2 * DN_V_HEADS
P_AQ = 0
P_AZ = P_AQ + ATT_Q_W
P_DQ = P_AZ + ATT_W
P_DK = P_DQ + DN_QK_W
P_DV = P_DK + DN_QK_W
P_DZ = P_DV + DN_V_W
P_G = P_DZ + DN_V_W
P_KV = P_G + 2 * D_MODEL
SMALL_W = 128

V7X_VMEM_LIMIT = 56 * 1024 * 1024


def _sigmoid(v):
    return 0.5 * jnp.tanh(0.5 * v) + 0.5


def _silu(v):
    h = 0.5 * v
    return h * jnp.tanh(h) + h


def _mm(a, b):
    return jnp.dot(a.astype(BF16), b.astype(BF16), preferred_element_type=F32)


INPROJ_TM = 2048
INPROJ_TN = 512
NORM_ROWS = 128


def _inproj_kernel(x_ref, nw_ref, wa_ref, wt_ref, b_ref, ws_ref, o_ref, os_ref, h_ref):
    j = pl.program_id(1)

    @pl.when(j == 0)
    def _():
        def body(r, carry):
            r0 = pl.multiple_of(r * NORM_ROWS, NORM_ROWS)
            xv = x_ref[pl.ds(r0, NORM_ROWS), :]
            ms = jnp.mean(xv * xv, axis=-1, keepdims=True)
            hv = (xv * lax.rsqrt(ms + EPS)) * nw_ref[...]
            h_ref[pl.ds(r0, NORM_ROWS), :] = hv.astype(BF16)
            return carry
        lax.fori_loop(0, INPROJ_TM // NORM_ROWS, body, 0)
        os_ref[...] = jnp.dot(h_ref[...], ws_ref[...], preferred_element_type=F32)

    def project(w_ref):
        acc = jnp.dot(h_ref[...], w_ref[...], preferred_element_type=F32)
        o_ref[...] = (acc + b_ref[...]).astype(o_ref.dtype)

    @pl.when(j < HEAD_BLOCKS)
    def _():
        project(wa_ref)

    @pl.when(j >= HEAD_BLOCKS)
    def _():
        project(wt_ref)


HEAD_BLOCKS = P_G // INPROJ_TN
SKIP_FROM = OFF_AK // INPROJ_TN


def _inproj(x2, norm_w, w_all, w_tail, bias, w_small):
    m = x2.shape[0]
    grid = (m // INPROJ_TM, MAIN_W // INPROJ_TN)

    def head_block(i, j):
        return 0, jnp.where(j < SKIP_FROM, j, jnp.minimum(j, HEAD_BLOCKS - 1) + 1)

    return pl.pallas_call(
        _inproj_kernel,
        grid=grid,
        in_specs=[
            pl.BlockSpec((INPROJ_TM, D_MODEL), lambda i, j: (i, 0), pipeline_mode=pl.Buffered(1)),
            pl.BlockSpec((1, D_MODEL), lambda i, j: (0, 0)),
            pl.BlockSpec((D_MODEL, INPROJ_TN), head_block),
            pl.BlockSpec((D_MODEL, INPROJ_TN), lambda i, j: (0, jnp.maximum(j - HEAD_BLOCKS, 0))),
            pl.BlockSpec((1, INPROJ_TN), lambda i, j: (0, j)),
            pl.BlockSpec((D_MODEL, SMALL_W), lambda i, j: (0, 0)),
        ],
        out_specs=[
            pl.BlockSpec((INPROJ_TM, INPROJ_TN), lambda i, j: (i, j)),
            pl.BlockSpec((INPROJ_TM, SMALL_W), lambda i, j: (i, 0)),
        ],
        out_shape=[
            jax.ShapeDtypeStruct((m, MAIN_W), BF16),
            jax.ShapeDtypeStruct((m, SMALL_W), F32),
        ],
        scratch_shapes=[pltpu.VMEM((INPROJ_TM, D_MODEL), BF16)],
        compiler_params=pltpu.CompilerParams(
            dimension_semantics=("parallel", "arbitrary"),
            vmem_limit_bytes=V7X_VMEM_LIMIT),
        name="inproj",
    )(x2, norm_w, w_all, w_tail, bias, w_small)


GATE_ROWS = 2 * CHUNK
DN_HG = 4
GATE_GROUPS = DN_K_HEADS // DN_HG
GROUP_VH = DN_V_HEADS // GATE_GROUPS


def _gates_kernel(s_ref, alog_ref, dt_ref, col_ref, row_ref):
    seq = s_ref.shape[1]
    row = lax.broadcasted_iota(jnp.int32, (CHUNK, SMALL_W), 0)
    lane = lax.broadcasted_iota(jnp.int32, (GATE_ROWS, SMALL_W), 1)

    def body(blk, carry):
        r0 = pl.multiple_of(blk * GATE_ROWS, GATE_ROWS)
        xs = s_ref[0, pl.ds(r0, GATE_ROWS), :]
        beta = _sigmoid(xs)
        t = xs + dt_ref[...]
        softplus = jnp.maximum(t, 0.0) + jnp.log(1.0 + jnp.exp(-jnp.abs(t)))
        g = -jnp.exp(alog_ref[...]) * softplus
        halves = []
        for part in range(GATE_ROWS // CHUNK):
            gp = g[part * CHUNK:(part + 1) * CHUNK]
            shift = 1
            while shift < CHUNK:
                gp = gp + jnp.where(row >= shift, pltpu.roll(gp, shift, axis=0), 0.0)
                shift *= 2
            halves.append(gp)
        gcum = jnp.concatenate(halves, axis=0)
        for grp in range(GATE_GROUPS):
            b_g = pltpu.roll(beta, (SMALL_W - grp * GROUP_VH) % SMALL_W, axis=1)
            g_g = pltpu.roll(gcum, SMALL_W - DN_V_HEADS - grp * GROUP_VH + GROUP_VH, axis=1)
            col_ref[0, grp, pl.ds(r0, GATE_ROWS), :] = jnp.where(lane < GROUP_VH, b_g, g_g)
        row_ref[0, blk] = gcum.T[DN_V_HEADS:2 * DN_V_HEADS]
        return carry

    lax.fori_loop(0, seq // GATE_ROWS, body, 0)


def _gates(small3, alog_row, dt_row):
    b, s, _ = small3.shape
    return pl.pallas_call(
        _gates_kernel,
        grid=(b,),
        in_specs=[
            pl.BlockSpec((1, s, SMALL_W), lambda i: (i, 0, 0)),
            pl.BlockSpec((1, SMALL_W), lambda i: (0, 0)),
            pl.BlockSpec((1, SMALL_W), lambda i: (0, 0)),
        ],
        out_specs=[
            pl.BlockSpec((1, GATE_GROUPS, s, SMALL_W), lambda i: (i, 0, 0, 0)),
            pl.BlockSpec((1, s // GATE_ROWS, DN_V_HEADS, GATE_ROWS), lambda i: (i, 0, 0, 0)),
        ],
        out_shape=[
            jax.ShapeDtypeStruct((b, GATE_GROUPS, s, SMALL_W), F32),
            jax.ShapeDtypeStruct((b, s // GATE_ROWS, DN_V_HEADS, GATE_ROWS), F32),
        ],
        compiler_params=pltpu.CompilerParams(dimension_semantics=("parallel",)),
        name="gates",
    )(small3, alog_row, dt_row)


PAIR_W = 2 * HEAD_DIM
KVH_W = GROUP * HEAD_DIM


def _attn_kernel(sinks_ref, q_ref, kvp_ref, kvc_ref, z_ref, o_ref):
    n = pl.program_id(1)
    L = ATT_BLOCK
    lane = lax.broadcasted_iota(jnp.int32, (L, PAIR_W), 1)
    lo = lane < HEAD_DIM

    def prep(kv):
        kv = kv.astype(F32)
        sw = pltpu.roll(kv, HEAD_DIM, axis=1)
        kk = jnp.where(lo, kv, sw).astype(BF16)
        va = jnp.where(lo, sw, 0.0).astype(BF16)
        vb = jnp.where(lo, 0.0, kv).astype(BF16)
        return kk, va, vb

    qi = lax.broadcasted_iota(jnp.int32, (L, 2 * L), 0)
    ki = lax.broadcasted_iota(jnp.int32, (L, 2 * L), 1)
    rel = qi + L - ki
    valid = (rel >= 0) & (rel < WINDOW) & ((ki >= L) | (n > 0))

    for h in range(N_KV_HEADS):
        kcols = slice(h * PAIR_W, (h + 1) * PAIR_W)
        kk_p, va_p, vb_p = prep(kvp_ref[0, :, kcols])
        kk_c, va_c, vb_c = prep(kvc_ref[0, :, kcols])
        kk = jnp.concatenate([kk_p, kk_c], axis=0)
        va = jnp.concatenate([va_p, va_c], axis=0)
        vb = jnp.concatenate([vb_p, vb_c], axis=0)
        pair_cols = [slice(h * KVH_W + pr * PAIR_W, h * KVH_W + (pr + 1) * PAIR_W)
                     for pr in range(GROUP // 2)]
        scores = []
        for cols in pair_cols:
            q2 = q_ref[0, :, cols].astype(F32) * (HEAD_DIM ** -0.5)
            for qm in (jnp.where(lo, q2, 0.0), jnp.where(lo, 0.0, q2)):
                scores.append(lax.dot_general(qm.astype(BF16), kk, (((1,), (1,)), ((), ())),
                                              preferred_element_type=F32))
        probs, inv = [], []
        for i, s in enumerate(scores):
            sink = sinks_ref[h * GROUP + i]
            s = jnp.where(valid, s, -jnp.inf)
            m = jnp.maximum(jnp.max(s, axis=-1, keepdims=True), sink)
            p = jnp.exp(s - m)
            inv.append(1.0 / (jnp.sum(p, axis=-1, keepdims=True) + jnp.exp(sink - m)))
            probs.append(p.astype(BF16))
        for pr, cols in enumerate(pair_cols):
            o2 = (jnp.dot(probs[2 * pr], va, preferred_element_type=F32)
                  + jnp.dot(probs[2 * pr + 1], vb, preferred_element_type=F32))
            o2 = o2 * jnp.where(lo, inv[2 * pr], inv[2 * pr + 1])
            zz = z_ref[0, :, cols].astype(F32)
            o_ref[0, :, cols] = (o2 * _silu(zz)).astype(o_ref.dtype)


def _attention(proj3, sinks):
    b, s, _ = proj3.shape
    nb = s // ATT_BLOCK
    kvw = 2 * ATT_KV_W
    kv0 = P_KV // kvw
    z0 = P_AZ // ATT_W
    return pl.pallas_call(
        _attn_kernel,
        grid=(b, nb),
        in_specs=[
            pl.BlockSpec(memory_space=pltpu.SMEM),
            pl.BlockSpec((1, ATT_BLOCK, ATT_Q_W), lambda i, n: (i, n, 0)),
            pl.BlockSpec((1, ATT_BLOCK, kvw), lambda i, n: (i, jnp.maximum(n - 1, 0), kv0)),
            pl.BlockSpec((1, ATT_BLOCK, kvw), lambda i, n: (i, n, kv0)),
            pl.BlockSpec((1, ATT_BLOCK, ATT_W), lambda i, n: (i, n, z0)),
        ],
        out_specs=pl.BlockSpec((1, ATT_BLOCK, ATT_W), lambda i, n: (i, n, 0)),
        out_shape=jax.ShapeDtypeStruct((b, s, ATT_W), BF16),
        compiler_params=pltpu.CompilerParams(
            dimension_semantics=("parallel", "parallel")),
        name="attn",
    )(sinks, proj3, proj3, proj3, proj3)


V_PER_K = DN_V_HEADS // DN_K_HEADS
assert V_PER_K == 2 and V_PER_K * CHUNK == 128, "two value heads are packed side by side in 128 lanes"
VW = V_PER_K * DN_V_DIM
DN_TS = 1024
DN_A_CHUNKS = 2
assert DN_A_CHUNKS % (GATE_ROWS // CHUNK) == 0
A_PER_B = 1
RAW_W = 2 * DN_K_DIM + VW
HALO_BLK = 16


def _dnet_kernel(q_ref, k_ref, v_ref, z_ref, qh_ref, kh_ref, vh_ref, col_ref, row_ref,
                 cwq_ref, cwk_ref, cwv_ref, nw_ref, o_ref,
                 st_ref, u_ref, w_ref, qe_ref, kd_ref, qkd_ref, raw_ref):
    t = pl.program_id(2)
    C = CHUNK
    ri = lax.broadcasted_iota(jnp.int32, (C, 2 * C), 0)
    li = lax.broadcasted_iota(jnp.int32, (C, 2 * C), 1)
    lo = li < C
    ci = jnp.where(lo, li, li - C)
    tril = ri >= ci
    strict = ri > ci
    eye = (ri == ci).astype(F32)
    top = lax.broadcasted_iota(jnp.int32, (2 * C, 2 * C), 0) < C
    left = lax.broadcasted_iota(jnp.int32, (2 * C, 2 * C), 1) < C
    diag_blk = top == left

    @pl.when(t == 0)
    def _():
        st_ref[...] = jnp.zeros_like(st_ref)

    def head_cols(ref, hh):
        per_head = ref.shape[2] // DN_HG
        return slice(hh * per_head, (hh + 1) * per_head)

    raw_off = (0, DN_K_DIM, 2 * DN_K_DIM)
    raw_ref[0:C - HALO_BLK, :] = jnp.zeros((C - HALO_BLK, raw_ref.shape[1]), BF16)
    for ref, halo_ref, off in zip((q_ref, k_ref, v_ref), (qh_ref, kh_ref, vh_ref), raw_off):
        tail = halo_ref[0]
        tail = jnp.where(t > 0, tail, jnp.zeros_like(tail))
        for hh in range(DN_HG):
            hc = head_cols(ref, hh)
            raw_ref[C - HALO_BLK:C, hh * RAW_W + off:hh * RAW_W + off + hc.stop - hc.start] = tail[:, hc]

    def stage_rows(r, carry):
        r0 = pl.multiple_of(r * C, C)
        for ref, off in zip((q_ref, k_ref, v_ref), raw_off):
            for hh in range(DN_HG):
                hc = head_cols(ref, hh)
                raw_ref[pl.ds(r0 + C, C), hh * RAW_W + off:hh * RAW_W + off + hc.stop - hc.start] = (
                    ref[0, pl.ds(r0, C), hc])
        return carry

    lax.fori_loop(0, DN_TS // C, stage_rows, 0)

    sr = lax.broadcasted_iota(jnp.int32, ((CONV_WIDTH - 1) * C, 2 * C), 0)
    sc = lax.broadcasted_iota(jnp.int32, ((CONV_WIDTH - 1) * C, 2 * C), 1)
    shift_mat = (sc == C + jnp.bitwise_and(sr, C - 1)
                 - (jnp.right_shift(sr, C.bit_length() - 1) + 1)).astype(BF16)

    def conv_silu(hh, r0):
        rcols = slice(hh * RAW_W, (hh + 1) * RAW_W)
        w = jnp.concatenate([cwq_ref[:, head_cols(q_ref, hh)], cwk_ref[:, head_cols(k_ref, hh)],
                             cwv_ref[:, head_cols(v_ref, hh)]], axis=1)
        window = raw_ref[pl.ds(r0, 2 * C), rcols]
        taps = jnp.dot(shift_mat, window, preferred_element_type=F32)
        y = window[C:].astype(F32) * w[CONV_WIDTH - 1:CONV_WIDTH, :]
        for s in range(1, CONV_WIDTH):
            y = y + taps[(s - 1) * C:s * C] * w[CONV_WIDTH - 1 - s:CONV_WIDTH - s, :]
        return _silu(y)

    def l2n(v):
        return v * lax.rsqrt(jnp.sum(v * v, axis=-1, keepdims=True) + EPS)

    def bcast(col):
        return jnp.broadcast_to(col, (C, 2 * C))

    def block_diag(m):
        return jnp.where(diag_blk, jnp.concatenate([m, m], axis=0), 0.0).astype(BF16)

    vslices = [slice(j * DN_V_DIM, (j + 1) * DN_V_DIM) for j in range(V_PER_K)]
    lane_sel = [lo, jnp.logical_not(lo)]


    def stage_a(it):
        per_tile = GATE_ROWS // C
        halves = [cc % per_tile for cc in range(DN_A_CHUNKS) for _ in range(DN_HG)]
        tiles = [cc // per_tile for cc in range(DN_A_CHUNKS) for _ in range(DN_HG)]
        units = [(hh, it * DN_A_CHUNKS + cc) for cc in range(DN_A_CHUNKS) for hh in range(DN_HG)]
        r0s = [pl.multiple_of(c * C, C) for _, c in units]
        n = len(units)
        g_rows = [row_ref[0, it * (DN_A_CHUNKS // per_tile) + r] for r in range(DN_A_CHUNKS // per_tile)]
        q, k, v = [], [], []
        for (hh, c), r0 in zip(units, r0s):
            y = conv_silu(hh, r0)
            q.append(l2n(y[:, raw_off[0]:raw_off[1]]) * (DN_K_DIM ** -0.5))
            k.append(l2n(y[:, raw_off[1]:raw_off[2]]))
            v.append(y[:, raw_off[2]:])
        yield
        qk_kk = []
        for i in range(n):
            kb = k[i].astype(BF16)
            qk_kk.append(lax.dot_general(jnp.concatenate([q[i].astype(BF16), kb], axis=0),
                                         jnp.concatenate([kb, kb], axis=0),
                                         (((1,), (1,)), ((), ())), preferred_element_type=F32))
        yield
        beta, gc, g_last, decay, a, x, p = [], [], [], [], [], [], []
        for i, ((hh, c), r0) in enumerate(zip(units, r0s)):
            colp = col_ref[0, 0, pl.ds(r0, C), :]
            vh = [hh * V_PER_K + j for j in range(V_PER_K)]
            beta.append([bcast(colp[:, h:h + 1]) for h in vh])
            gc.append([bcast(colp[:, GROUP_VH + h:GROUP_VH + h + 1]) for h in vh])
            tok = slice(halves[i] * C, (halves[i] + 1) * C)
            gr = jnp.concatenate([g_rows[tiles[i]][h:h + 1, tok] for h in vh], axis=1)
            g_last.append([gr[:, (j + 1) * C - 1:(j + 1) * C] for j in range(V_PER_K)])
            beta_p = jnp.where(lo, beta[i][0], beta[i][1])
            gc_p = jnp.where(lo, gc[i][0], gc[i][1])
            decay.append(jnp.exp(jnp.where(tril, gc_p - gr, -jnp.inf)))
            a.append(jnp.where(strict, beta_p * qk_kk[i][C:] * decay[i], 0.0))
        for i in range(n):
            x.append(eye - a[i])
            p.append(jnp.dot(a[i].astype(BF16), block_diag(a[i]), preferred_element_type=F32))
        yield
        n_sq = 2
        while n_sq < C:
            for i in range(n):
                rhs = jnp.concatenate([block_diag(p[i]), block_diag(x[i])], axis=1)
                pp_px = jnp.dot(p[i].astype(BF16), rhs, preferred_element_type=F32)
                x[i] = x[i] + pp_px[:, 2 * C:]
                p[i] = pp_px[:, :2 * C]
            n_sq *= 2
            yield
        sols = []
        for i in range(n):
            eg = [jnp.exp(gc[i][j]) for j in range(V_PER_K)]
            rhs = jnp.concatenate(
                [jnp.concatenate([v[i][:, vslices[j]] * beta[i][j],
                                  k[i] * (beta[i][j] * eg[j])], axis=1) for j in range(V_PER_K)],
                axis=0).astype(BF16)
            sols.append([jnp.dot(jnp.where(lane_sel[j], x[i], 0.0).astype(BF16), rhs,
                                 preferred_element_type=F32) for j in range(V_PER_K)])
            hh, r0 = units[i][0], r0s[i]
            qe_ref[hh, pl.ds(r0, C), :] = jnp.concatenate(
                [q[i] * eg[j] for j in range(V_PER_K)], axis=1).astype(BF16)
            kd_ref[hh, pl.ds(pl.multiple_of(V_PER_K * r0, V_PER_K * C), V_PER_K * C), :] = jnp.concatenate(
                [k[i] * jnp.exp(g_last[i][j] - gc[i][j]) for j in range(V_PER_K)],
                axis=0).astype(BF16)
            qkd_ref[hh, pl.ds(r0, C), :] = (qk_kk[i][:C] * decay[i]).astype(BF16)
        yield
        for i in range(n):
            hh, r0 = units[i][0], r0s[i]
            u_ref[hh, pl.ds(r0, C), :] = jnp.concatenate(
                [s[:, :DN_V_DIM] for s in sols[i]], axis=1)
            w_ref[hh, pl.ds(r0, C), :] = jnp.concatenate(
                [s[:, DN_V_DIM:] for s in sols[i]], axis=1).astype(BF16)

    def stage_b(c):
        r0 = pl.multiple_of(c * C, C)
        zero_blk = jnp.zeros((DN_K_DIM, DN_V_DIM), BF16)
        first = lax.broadcasted_iota(jnp.int32, (C, VW), 1) < DN_V_DIM
        ws_qs, v_new = [], []
        for hh in range(DN_HG):
            s0 = st_ref[hh * V_PER_K].astype(BF16)
            s1 = st_ref[hh * V_PER_K + 1].astype(BF16)
            s_bd = jnp.concatenate([jnp.concatenate([s0, zero_blk], axis=1),
                                    jnp.concatenate([zero_blk, s1], axis=1)], axis=0)
            ws_qs.append(jnp.dot(jnp.concatenate([w_ref[hh, pl.ds(r0, C), :],
                                                  qe_ref[hh, pl.ds(r0, C), :]], axis=0),
                                 s_bd, preferred_element_type=F32))
            v_new.append((u_ref[hh, pl.ds(r0, C), :] - ws_qs[hh][:C]).astype(BF16))
        yield
        v_bd = []
        for hh in range(DN_HG):
            zeros = jnp.zeros_like(v_new[hh])
            v_bd.append(jnp.concatenate([jnp.where(first, v_new[hh], zeros),
                                         jnp.where(first, zeros, v_new[hh])], axis=0))
            kd = kd_ref[hh, pl.ds(pl.multiple_of(V_PER_K * r0, V_PER_K * C), V_PER_K * C), :]
            kv = lax.dot_general(kd, v_bd[hh], (((0,), (0,)), ((), ())),
                                 preferred_element_type=F32)
            for j in range(V_PER_K):
                gl = GROUP_VH + hh * V_PER_K + j
                g_last = col_ref[0, 0, pl.ds(r0 + C - 1, 1), gl:gl + 1]
                st_ref[hh * V_PER_K + j] = (st_ref[hh * V_PER_K + j] * jnp.exp(g_last)
                                            + kv[:, vslices[j]])
        yield
        for hh in range(DN_HG):
            o_both = ws_qs[hh][C:] + jnp.dot(qkd_ref[hh, pl.ds(r0, C), :], v_bd[hh],
                                             preferred_element_type=F32)
            for j in range(V_PER_K):
                o = o_both[:, vslices[j]]
                on = o * lax.rsqrt(jnp.mean(o * o, axis=-1, keepdims=True) + EPS) * nw_ref[...]
                ocols = slice(hh * VW + j * DN_V_DIM, hh * VW + (j + 1) * DN_V_DIM)
                zz = z_ref[0, pl.ds(r0, C), ocols].astype(F32)
                o_ref[0, pl.ds(r0, C), ocols] = (on * _silu(zz)).astype(o_ref.dtype)

    def recurrence(it):
        for cc in range(DN_A_CHUNKS):
            yield from stage_b(it * DN_A_CHUNKS + cc)
            yield

    def emit(main, side=()):
        side = iter(side)
        for step, _ in enumerate(main):
            if step % A_PER_B == A_PER_B - 1:
                next(side, None)
        for _ in side:
            pass

    n_iter = DN_TS // (C * DN_A_CHUNKS)
    emit(stage_a(0))

    def body(it, carry):
        emit(stage_a(it + 1), recurrence(it))
        return carry

    lax.fori_loop(0, n_iter - 1, body, 0)
    emit(recurrence(n_iter - 1))


def _deltanet(proj3, colp, rowp, conv_w, dn_norm_w):
    b, s, _ = proj3.shape
    kw = DN_HG * DN_K_DIM
    vw = DN_HG * VW
    q0, k0, v0, z0 = P_DQ // kw, P_DK // kw, P_DV // vw, P_DZ // vw
    ck0, cv0 = DN_QK_W // kw, 2 * DN_QK_W // vw
    hb = DN_TS // HALO_BLK

    def halo(t):
        return jnp.maximum(t * hb - 1, 0)

    return pl.pallas_call(
        _dnet_kernel,
        grid=(b, DN_K_HEADS // DN_HG, s // DN_TS),
        in_specs=[
            pl.BlockSpec((1, DN_TS, kw), lambda i, g, t: (i, t, q0 + g)),
            pl.BlockSpec((1, DN_TS, kw), lambda i, g, t: (i, t, k0 + g)),
            pl.BlockSpec((1, DN_TS, vw), lambda i, g, t: (i, t, v0 + g)),
            pl.BlockSpec((1, DN_TS, vw), lambda i, g, t: (i, t, z0 + g)),
            pl.BlockSpec((1, HALO_BLK, kw), lambda i, g, t: (i, halo(t), q0 + g)),
            pl.BlockSpec((1, HALO_BLK, kw), lambda i, g, t: (i, halo(t), k0 + g)),
            pl.BlockSpec((1, HALO_BLK, vw), lambda i, g, t: (i, halo(t), v0 + g)),
            pl.BlockSpec((1, 1, DN_TS, SMALL_W), lambda i, g, t: (i, g, t, 0)),
            pl.BlockSpec((1, DN_TS // GATE_ROWS, GROUP_VH, GATE_ROWS), lambda i, g, t: (i, t, g, 0)),
            pl.BlockSpec((CONV_WIDTH, kw), lambda i, g, t: (0, g)),
            pl.BlockSpec((CONV_WIDTH, kw), lambda i, g, t: (0, ck0 + g)),
            pl.BlockSpec((CONV_WIDTH, vw), lambda i, g, t: (0, cv0 + g)),
            pl.BlockSpec((1, DN_V_DIM), lambda i, g, t: (0, 0)),
        ],
        out_specs=pl.BlockSpec((1, DN_TS, vw), lambda i, g, t: (i, t, g)),
        out_shape=jax.ShapeDtypeStruct((b, s, DN_V_W), BF16),
        scratch_shapes=[
            pltpu.VMEM((DN_HG * V_PER_K, DN_K_DIM, DN_V_DIM), F32),
            pltpu.VMEM((DN_HG, DN_TS, VW), F32),
            pltpu.VMEM((DN_HG, DN_TS, VW), BF16),
            pltpu.VMEM((DN_HG, DN_TS, VW), BF16),
            pltpu.VMEM((DN_HG, V_PER_K * DN_TS, DN_K_DIM), BF16),
            pltpu.VMEM((DN_HG, DN_TS, V_PER_K * CHUNK), BF16),
            pltpu.VMEM((CHUNK + DN_TS, DN_HG * RAW_W), BF16),
        ],
        compiler_params=pltpu.CompilerParams(
            dimension_semantics=("parallel", "parallel", "arbitrary"),
            vmem_limit_bytes=V7X_VMEM_LIMIT),
        name="dnet",
    )(proj3, proj3, proj3, proj3, proj3, proj3, proj3, colp, rowp, conv_w, conv_w, conv_w,
      dn_norm_w)


MERGE_TM = 512
MERGE_TN = 512


def _merge_kernel(oa_ref, od_ref, wa_ref, wd_ref, ga_ref, gd_ref, o_ref):
    ya = jnp.dot(oa_ref[...], wa_ref[...], preferred_element_type=F32)
    yd = jnp.dot(od_ref[...], wd_ref[...], preferred_element_type=F32)
    ga = _sigmoid(ga_ref[...].astype(F32))
    gd = _sigmoid(gd_ref[...].astype(F32))
    o_ref[...] = (ga * ya + gd * yd).astype(o_ref.dtype)


def _merge(oa, od, wa, wd, proj2):
    m = oa.shape[0]
    g0 = P_G // MERGE_TN
    g1 = (P_G + D_MODEL) // MERGE_TN
    return pl.pallas_call(
        _merge_kernel,
        grid=(m // MERGE_TM, D_MODEL // MERGE_TN),
        in_specs=[
            pl.BlockSpec((MERGE_TM, ATT_W), lambda i, j: (i, 0)),
            pl.BlockSpec((MERGE_TM, DN_V_W), lambda i, j: (i, 0)),
            pl.BlockSpec((ATT_W, MERGE_TN), lambda i, j: (0, j)),
            pl.BlockSpec((DN_V_W, MERGE_TN), lambda i, j: (0, j)),
            pl.BlockSpec((MERGE_TM, MERGE_TN), lambda i, j: (i, g0 + j)),
            pl.BlockSpec((MERGE_TM, MERGE_TN), lambda i, j: (i, g1 + j)),
        ],
        out_specs=pl.BlockSpec((MERGE_TM, MERGE_TN), lambda i, j: (i, j)),
        out_shape=jax.ShapeDtypeStruct((m, D_MODEL), BF16),
        compiler_params=pltpu.CompilerParams(
            dimension_semantics=("parallel", "parallel"),
            vmem_limit_bytes=V7X_VMEM_LIMIT),
        name="merge",
    )(oa, od, wa, wd, proj2, proj2)


OUT_TM = 512


def _out_kernel(m_ref, w_ref, x_ref, nw_ref, o_ref):
    y = x_ref[...] + jnp.dot(m_ref[...], w_ref[...], preferred_element_type=F32)
    ms = jnp.mean(y * y, axis=-1, keepdims=True)
    o_ref[...] = (y * lax.rsqrt(ms + EPS)) * nw_ref[...]


def _outproj(merged, w_out, x2, final_norm_w):
    m = x2.shape[0]
    return pl.pallas_call(
        _out_kernel,
        grid=(m // OUT_TM,),
        in_specs=[
            pl.BlockSpec((OUT_TM, D_MODEL), lambda i: (i, 0)),
            pl.BlockSpec((D_MODEL, D_MODEL), lambda i: (0, 0)),
            pl.BlockSpec((OUT_TM, D_MODEL), lambda i: (i, 0)),
            pl.BlockSpec((1, D_MODEL), lambda i: (0, 0)),
        ],
        out_specs=pl.BlockSpec((OUT_TM, D_MODEL), lambda i: (i, 0)),
        out_shape=jax.ShapeDtypeStruct((m, D_MODEL), F32),
        compiler_params=pltpu.CompilerParams(
            dimension_semantics=("parallel",),
            vmem_limit_bytes=V7X_VMEM_LIMIT),
        name="outproj",
    )(merged, w_out, x2, final_norm_w)


def _layer(x, norm_w, w_in, b_qkv, sinks, conv_w, a_log, dt_bias, dn_norm_w,
           w_att_branch, w_dn_branch, w_out, out_norm_w):
    b, s, d = x.shape
    m = b * s
    x2 = x.reshape(m, d)

    wk = w_in[:, OFF_AK:OFF_AV].reshape(d, N_KV_HEADS, 1, HEAD_DIM)
    wv = w_in[:, OFF_AV:OFF_AZ].reshape(d, N_KV_HEADS, 1, HEAD_DIM)
    w_kv = jnp.concatenate([wk, wv], axis=2).reshape(d, 2 * ATT_KV_W)
    w_all = w_in.astype(BF16)
    w_tail = jnp.concatenate([w_in[:, OFF_G:], w_kv], axis=1).astype(BF16)
    w_small = jnp.pad(w_in[:, OFF_DB:OFF_G], ((0, 0), (0, SMALL_W - 2 * DN_V_HEADS))).astype(BF16)
    bk = b_qkv[ATT_Q_W:ATT_Q_W + ATT_KV_W].reshape(N_KV_HEADS, 1, HEAD_DIM)
    bv = b_qkv[ATT_Q_W + ATT_KV_W:].reshape(N_KV_HEADS, 1, HEAD_DIM)
    bias = jnp.concatenate([b_qkv[:ATT_Q_W], jnp.zeros((P_KV - P_AZ,), F32),
                            jnp.concatenate([bk, bv], axis=1).reshape(-1)]).reshape(1, MAIN_W)

    proj, small = _inproj(x2, norm_w.reshape(1, d), w_all, w_tail, bias, w_small)
    proj3 = proj.reshape(b, s, MAIN_W)

    lane_pad = (DN_V_HEADS, SMALL_W - 2 * DN_V_HEADS)
    alog_row = jnp.pad(a_log, lane_pad).reshape(1, SMALL_W)
    dt_row = jnp.pad(dt_bias, lane_pad).reshape(1, SMALL_W)
    colp, rowp = _gates(small.reshape(b, s, SMALL_W), alog_row, dt_row)

    o_att = _attention(proj3, sinks)
    o_dn = _deltanet(proj3, colp, rowp, conv_w, dn_norm_w.reshape(1, DN_V_DIM))

    merged = _merge(o_att.reshape(m, ATT_W), o_dn.reshape(m, DN_V_W),
                    w_att_branch.astype(BF16), w_dn_branch.astype(BF16), proj)
    y = _outproj(merged, w_out.astype(BF16), x2, out_norm_w.reshape(1, d))
    return y.reshape(b, s, d)


def kernel(x, norm_w, w_in, b_qkv, sinks, conv_w, a_log, dt_bias, dn_norm_w,
           w_att_branch, w_dn_branch, w_out, final_norm_w):
    depth = norm_w.shape[0]
    assert depth == 1, "the final RMSNorm is fused into the single layer's output kernel"
    return _layer(x, norm_w[0], w_in[0], b_qkv[0], sinks[0], conv_w[0], a_log[0], dt_bias[0],
                  dn_norm_w[0], w_att_branch[0], w_dn_branch[0], w_out[0], final_norm_w)
```

```python
import functools

import jax
import jax.numpy as jnp
from jax import lax
from jax.experimental import pallas as pl
from jax.experimental.pallas import tpu as pltpu

F32 = jnp.float32
BF16 = jnp.bfloat16

D_MODEL = 2048
HEAD_DIM = 64
N_Q_HEADS = 32
N_KV_HEADS = 4
GROUP = N_Q_HEADS // N_KV_HEADS
WINDOW = 128
ATT_BLOCK = 128
ATT_Q_W = N_Q_HEADS * HEAD_DIM
ATT_KV_W = N_KV_HEADS * HEAD_DIM
ATT_W = ATT_Q_W

DN_K_HEADS = 16
DN_V_HEADS = 32
DN_K_DIM = 128
DN_V_DIM = 128
DN_QK_W = DN_K_HEADS * DN_K_DIM
DN_V_W = DN_V_HEADS * DN_V_DIM
DN_CONV_CH = 2 * DN_QK_W + DN_V_W
CONV_WIDTH = 4
CHUNK = 64
EPS = 1e-6

OFF_AQ = 0
OFF_AK = OFF_AQ + ATT_Q_W
OFF_AV = OFF_AK + ATT_KV_W
OFF_AZ = OFF_AV + ATT_KV_W
OFF_DQKV = OFF_AZ + ATT_W
OFF_DZ = OFF_DQKV + DN_CONV_CH
OFF_DB = OFF_DZ + DN_V_W
OFF_DA = OFF_DB + DN_V_HEADS
OFF_G = OFF_DA + DN_V_HEADS
IN_W = OFF_G + 2 * D_MODEL

MAIN_W = IN_W - 2 * DN_V_HEADS
P_AQ = 0
P_AZ = P_AQ + ATT_Q_W
P_DQ = P_AZ + ATT_W
P_DK = P_DQ + DN_QK_W
P_DV = P_DK + DN_QK_W
P_DZ = P_DV + DN_V_W
P_G = P_DZ + DN_V_W
P_KV = P_G + 2 * D_MODEL
SMALL_W = 128

V7X_VMEM_LIMIT = 56 * 1024 * 1024


def _sigmoid(v):
    return 0.5 * jnp.tanh(0.5 * v) + 0.5


def _silu(v):
    h = 0.5 * v
    return h * jnp.tanh(h) + h


def _mm(a, b):
    return jnp.dot(a.astype(BF16), b.astype(BF16), preferred_element_type=F32)


INPROJ_TM = 2048
INPROJ_TN = 512
NORM_ROWS = 128


def _inproj_kernel(x_ref, nw_ref, wa_ref, wt_ref, b_ref, ws_ref, o_ref, os_ref, h_ref):
    j = pl.program_id(1)

    @pl.when(j == 0)
    def _():
        def body(r, carry):
            r0 = pl.multiple_of(r * NORM_ROWS, NORM_ROWS)
            xv = x_ref[pl.ds(r0, NORM_ROWS), :]
            ms = jnp.mean(xv * xv, axis=-1, keepdims=True)
            hv = (xv * lax.rsqrt(ms + EPS)) * nw_ref[...]
            h_ref[pl.ds(r0, NORM_ROWS), :] = hv.astype(BF16)
            return carry
        lax.fori_loop(0, INPROJ_TM // NORM_ROWS, body, 0)
        os_ref[...] = jnp.dot(h_ref[...], ws_ref[...], preferred_element_type=F32)

    def project(w_ref):
        acc = jnp.dot(h_ref[...], w_ref[...].astype(BF16), preferred_element_type=F32)
        o_ref[...] = (acc + b_ref[...]).astype(o_ref.dtype)

    @pl.when(j < HEAD_BLOCKS)
    def _():
        project(wa_ref)

    @pl.when(j >= HEAD_BLOCKS)
    def _():
        project(wt_ref)


HEAD_BLOCKS = P_G // INPROJ_TN
SKIP_FROM = OFF_AK // INPROJ_TN


def _inproj(x2, norm_w, w_in, w_tail, bias, w_small):
    m = x2.shape[0]
    grid = (m // INPROJ_TM, MAIN_W // INPROJ_TN)

    def head_block(i, j):
        return 0, jnp.where(j < SKIP_FROM, j, jnp.minimum(j, HEAD_BLOCKS - 1) + 1)

    return pl.pallas_call(
        _inproj_kernel,
        grid=grid,
        in_specs=[
            pl.BlockSpec((INPROJ_TM, D_MODEL), lambda i, j: (i, 0), pipeline_mode=pl.Buffered(1)),
            pl.BlockSpec((1, D_MODEL), lambda i, j: (0, 0)),
            pl.BlockSpec((D_MODEL, INPROJ_TN), head_block),
            pl.BlockSpec((D_MODEL, INPROJ_TN), lambda i, j: (0, jnp.maximum(j - HEAD_BLOCKS, 0))),
            pl.BlockSpec((1, INPROJ_TN), lambda i, j: (0, j)),
            pl.BlockSpec((D_MODEL, SMALL_W), lambda i, j: (0, 0)),
        ],
        out_specs=[
            pl.BlockSpec((INPROJ_TM, INPROJ_TN), lambda i, j: (i, j)),
            pl.BlockSpec((INPROJ_TM, SMALL_W), lambda i, j: (i, 0)),
        ],
        out_shape=[
            jax.ShapeDtypeStruct((m, MAIN_W), BF16),
            jax.ShapeDtypeStruct((m, SMALL_W), F32),
        ],
        scratch_shapes=[pltpu.VMEM((INPROJ_TM, D_MODEL), BF16)],
        compiler_params=pltpu.CompilerParams(
            dimension_semantics=("parallel", "arbitrary"),
            vmem_limit_bytes=V7X_VMEM_LIMIT),
        name="inproj",
    )(x2, norm_w, w_in, w_tail, bias, w_small)


GATE_ROWS = 2 * CHUNK
DN_HG = 4
GATE_GROUPS = DN_K_HEADS // DN_HG
GROUP_VH = DN_V_HEADS // GATE_GROUPS


def _gates_kernel(s_ref, alog_ref, dt_ref, col_ref, row_ref):
    seq = s_ref.shape[1]
    row = lax.broadcasted_iota(jnp.int32, (CHUNK, SMALL_W), 0)
    lane = lax.broadcasted_iota(jnp.int32, (GATE_ROWS, SMALL_W), 1)

    def body(blk, carry):
        r0 = pl.multiple_of(blk * GATE_ROWS, GATE_ROWS)
        xs = s_ref[0, pl.ds(r0, GATE_ROWS), :]
        beta = _sigmoid(xs)
        t = xs + dt_ref[...]
        softplus = jnp.maximum(t, 0.0) + jnp.log(1.0 + jnp.exp(-jnp.abs(t)))
        g = -jnp.exp(alog_ref[...]) * softplus
        halves = []
        for part in range(GATE_ROWS // CHUNK):
            gp = g[part * CHUNK:(part + 1) * CHUNK]
            shift = 1
            while shift < CHUNK:
                gp = gp + jnp.where(row >= shift, pltpu.roll(gp, shift, axis=0), 0.0)
                shift *= 2
            halves.append(gp)
        gcum = jnp.concatenate(halves, axis=0)
        for grp in range(GATE_GROUPS):
            b_g = pltpu.roll(beta, (SMALL_W - grp * GROUP_VH) % SMALL_W, axis=1)
            g_g = pltpu.roll(gcum, SMALL_W - DN_V_HEADS - grp * GROUP_VH + GROUP_VH, axis=1)
            col_ref[0, grp, pl.ds(r0, GATE_ROWS), :] = jnp.where(lane < GROUP_VH, b_g, g_g)
        row_ref[0, blk] = gcum.T[DN_V_HEADS:2 * DN_V_HEADS]
        return carry

    lax.fori_loop(0, seq // GATE_ROWS, body, 0)


def _gates(small3, alog_row, dt_row):
    b, s, _ = small3.shape
    return pl.pallas_call(
        _gates_kernel,
        grid=(b,),
        in_specs=[
            pl.BlockSpec((1, s, SMALL_W), lambda i: (i, 0, 0)),
            pl.BlockSpec((1, SMALL_W), lambda i: (0, 0)),
            pl.BlockSpec((1, SMALL_W), lambda i: (0, 0)),
        ],
        out_specs=[
            pl.BlockSpec((1, GATE_GROUPS, s, SMALL_W), lambda i: (i, 0, 0, 0)),
            pl.BlockSpec((1, s // GATE_ROWS, DN_V_HEADS, GATE_ROWS), lambda i: (i, 0, 0, 0)),
        ],
        out_shape=[
            jax.ShapeDtypeStruct((b, GATE_GROUPS, s, SMALL_W), F32),
            jax.ShapeDtypeStruct((b, s // GATE_ROWS, DN_V_HEADS, GATE_ROWS), F32),
        ],
        compiler_params=pltpu.CompilerParams(dimension_semantics=("parallel",)),
        name="gates",
    )(small3, alog_row, dt_row)


PAIR_W = 2 * HEAD_DIM
KVH_W = GROUP * HEAD_DIM


def _attn_kernel(sinks_ref, q_ref, kvp_ref, kvc_ref, z_ref, o_ref):
    n = pl.program_id(1)
    L = ATT_BLOCK
    lane = lax.broadcasted_iota(jnp.int32, (L, PAIR_W), 1)
    lo = lane < HEAD_DIM

    def prep(kv):
        kv = kv.astype(F32)
        sw = pltpu.roll(kv, HEAD_DIM, axis=1)
        kk = jnp.where(lo, kv, sw).astype(BF16)
        va = jnp.where(lo, sw, 0.0).astype(BF16)
        vb = jnp.where(lo, 0.0, kv).astype(BF16)
        return kk, va, vb

    qi = lax.broadcasted_iota(jnp.int32, (L, 2 * L), 0)
    ki = lax.broadcasted_iota(jnp.int32, (L, 2 * L), 1)
    rel = qi + L - ki
    valid = (rel >= 0) & (rel < WINDOW) & ((ki >= L) | (n > 0))

    def kv_head(h):
        kcols = slice(h * PAIR_W, (h + 1) * PAIR_W)
        kk_p, va_p, vb_p = prep(kvp_ref[0, :, kcols])
        kk_c, va_c, vb_c = prep(kvc_ref[0, :, kcols])
        kk = jnp.concatenate([kk_p, kk_c], axis=0)
        va = jnp.concatenate([va_p, va_c], axis=0)
        vb = jnp.concatenate([vb_p, vb_c], axis=0)
        pair_cols = [slice(h * KVH_W + pr * PAIR_W, h * KVH_W + (pr + 1) * PAIR_W)
                     for pr in range(GROUP // 2)]
        scores = []
        for cols in pair_cols:
            q2 = q_ref[0, :, cols].astype(F32) * (HEAD_DIM ** -0.5)
            for qm in (jnp.where(lo, q2, 0.0), jnp.where(lo, 0.0, q2)):
                scores.append(lax.dot_general(qm.astype(BF16), kk, (((1,), (1,)), ((), ())),
                                              preferred_element_type=F32))
        probs, inv = [], []
        for i, s in enumerate(scores):
            sink = sinks_ref[h * GROUP + i]
            s = jnp.where(valid, s, -jnp.inf)
            m = jnp.maximum(jnp.max(s, axis=-1, keepdims=True), sink)
            p = jnp.exp(s - m)
            inv.append(1.0 / (jnp.sum(p, axis=-1, keepdims=True) + jnp.exp(sink - m)))
            probs.append(p.astype(BF16))
        for pr, cols in enumerate(pair_cols):
            o2 = (jnp.dot(probs[2 * pr], va, preferred_element_type=F32)
                  + jnp.dot(probs[2 * pr + 1], vb, preferred_element_type=F32))
            o2 = o2 * jnp.where(lo, inv[2 * pr], inv[2 * pr + 1])
            zz = z_ref[0, :, cols].astype(F32)
            o_ref[0, :, cols] = (o2 * _silu(zz)).astype(o_ref.dtype)

    for h in range(N_KV_HEADS):
        kv_head(h)


def _attention(proj3, sinks):
    b, s, _ = proj3.shape
    nb = s // ATT_BLOCK
    kvw = 2 * ATT_KV_W
    kv0 = P_KV // kvw
    z0 = P_AZ // ATT_W
    return pl.pallas_call(
        _attn_kernel,
        grid=(b, nb),
        in_specs=[
            pl.BlockSpec(memory_space=pltpu.SMEM),
            pl.BlockSpec((1, ATT_BLOCK, ATT_Q_W), lambda i, n: (i, n, 0)),
            pl.BlockSpec((1, ATT_BLOCK, kvw), lambda i, n: (i, jnp.maximum(n - 1, 0), kv0)),
            pl.BlockSpec((1, ATT_BLOCK, kvw), lambda i, n: (i, n, kv0)),
            pl.BlockSpec((1, ATT_BLOCK, ATT_W), lambda i, n: (i, n, z0)),
        ],
        out_specs=pl.BlockSpec((1, ATT_BLOCK, ATT_W), lambda i, n: (i, n, 0)),
        out_shape=jax.ShapeDtypeStruct((b, s, ATT_W), BF16),
        compiler_params=pltpu.CompilerParams(
            dimension_semantics=("parallel", "parallel")),
        name="attn",
    )(sinks, proj3, proj3, proj3, proj3)


V_PER_K = DN_V_HEADS // DN_K_HEADS
assert V_PER_K == 2 and V_PER_K * CHUNK == 128, "two value heads are packed side by side in 128 lanes"
VW = V_PER_K * DN_V_DIM
DN_TS = 1024
DN_A_CHUNKS = 2
assert DN_A_CHUNKS % (GATE_ROWS // CHUNK) == 0
A1_LEVELS = 2
RAW_W = 2 * DN_K_DIM + VW
HALO_BLK = 16


def _dnet_kernel(q_ref, k_ref, v_ref, z_ref, qh_ref, kh_ref, vh_ref, col_ref, row_ref,
                 cwq_ref, cwk_ref, cwv_ref, nw_ref, o_ref,
                 st_ref, u_ref, w_ref, qe_ref, kd_ref, qkd_ref, raw_ref, mid_ref):
    t = pl.program_id(2)
    C = CHUNK
    ri = lax.broadcasted_iota(jnp.int32, (C, 2 * C), 0)
    li = lax.broadcasted_iota(jnp.int32, (C, 2 * C), 1)
    lo = li < C
    ci = jnp.where(lo, li, li - C)
    tril = ri >= ci
    strict = ri > ci
    eye = (ri == ci).astype(F32)
    top = lax.broadcasted_iota(jnp.int32, (2 * C, 2 * C), 0) < C
    left = lax.broadcasted_iota(jnp.int32, (2 * C, 2 * C), 1) < C
    diag_blk = top == left

    @pl.when(t == 0)
    def _():
        st_ref[...] = jnp.zeros_like(st_ref)

    def head_cols(ref, hh):
        per_head = ref.shape[2] // DN_HG
        return slice(hh * per_head, (hh + 1) * per_head)

    raw_off = (0, DN_K_DIM, 2 * DN_K_DIM)
    raw_ref[0:C - HALO_BLK, :] = jnp.zeros((C - HALO_BLK, raw_ref.shape[1]), BF16)
    for ref, halo_ref, off in zip((q_ref, k_ref, v_ref), (qh_ref, kh_ref, vh_ref), raw_off):
        tail = halo_ref[0]
        tail = jnp.where(t > 0, tail, jnp.zeros_like(tail))
        for hh in range(DN_HG):
            hc = head_cols(ref, hh)
            raw_ref[C - HALO_BLK:C, hh * RAW_W + off:hh * RAW_W + off + hc.stop - hc.start] = tail[:, hc]

    def stage_rows(r, carry):
        r0 = pl.multiple_of(r * C, C)
        for ref, off in zip((q_ref, k_ref, v_ref), raw_off):
            for hh in range(DN_HG):
                hc = head_cols(ref, hh)
                raw_ref[pl.ds(r0 + C, C), hh * RAW_W + off:hh * RAW_W + off + hc.stop - hc.start] = (
                    ref[0, pl.ds(r0, C), hc])
        return carry

    lax.fori_loop(0, DN_TS // C, stage_rows, 0)

    sr = lax.broadcasted_iota(jnp.int32, ((CONV_WIDTH - 1) * C, 2 * C), 0)
    sc = lax.broadcasted_iota(jnp.int32, ((CONV_WIDTH - 1) * C, 2 * C), 1)
    shift_mat = (sc == C + jnp.bitwise_and(sr, C - 1)
                 - (jnp.right_shift(sr, C.bit_length() - 1) + 1)).astype(BF16)

    def conv_silu(hh, r0):
        rcols = slice(hh * RAW_W, (hh + 1) * RAW_W)
        w = jnp.concatenate([cwq_ref[:, head_cols(q_ref, hh)], cwk_ref[:, head_cols(k_ref, hh)],
                             cwv_ref[:, head_cols(v_ref, hh)]], axis=1)
        window = raw_ref[pl.ds(r0, 2 * C), rcols]
        taps = jnp.dot(shift_mat, window, preferred_element_type=F32)
        y = window[C:].astype(F32) * w[CONV_WIDTH - 1:CONV_WIDTH, :]
        for s in range(1, CONV_WIDTH):
            y = y + taps[(s - 1) * C:s * C] * w[CONV_WIDTH - 1 - s:CONV_WIDTH - s, :]
        return _silu(y)

    def l2n(v):
        return v * lax.rsqrt(jnp.sum(v * v, axis=-1, keepdims=True) + EPS)

    def bcast(col):
        return jnp.broadcast_to(col, (C, 2 * C))

    def block_diag(m):
        return jnp.where(diag_blk, jnp.concatenate([m, m], axis=0), 0.0).astype(BF16)

    vslices = [slice(j * DN_V_DIM, (j + 1) * DN_V_DIM) for j in range(V_PER_K)]
    lane_sel = [lo, jnp.logical_not(lo)]


    per_tile = GATE_ROWS // C
    unit_heads = [hh for _ in range(DN_A_CHUNKS) for hh in range(DN_HG)]
    unit_halves = [cc % per_tile for cc in range(DN_A_CHUNKS) for _ in range(DN_HG)]
    unit_tiles = [cc // per_tile for cc in range(DN_A_CHUNKS) for _ in range(DN_HG)]
    n_units = len(unit_heads)
    mid_cols, offset = {}, 0
    for name, width in (("p", 2 * C), ("x", 2 * C), ("q", DN_K_DIM), ("k", DN_K_DIM), ("v", VW)):
        mid_cols[name] = slice(offset, offset + width)
        offset += width

    def unit_rows(it):
        return [pl.multiple_of((it * DN_A_CHUNKS + cc) * C, C)
                for cc in range(DN_A_CHUNKS) for _ in range(DN_HG)]

    def gate_terms(it, r0s):
        g_rows = [row_ref[0, it * (DN_A_CHUNKS // per_tile) + r] for r in range(DN_A_CHUNKS // per_tile)]
        beta, gc, gr, g_last = [], [], [], []
        for i in range(n_units):
            colp = col_ref[0, 0, pl.ds(r0s[i], C), :]
            vh = [unit_heads[i] * V_PER_K + j for j in range(V_PER_K)]
            beta.append([bcast(colp[:, h:h + 1]) for h in vh])
            gc.append([bcast(colp[:, GROUP_VH + h:GROUP_VH + h + 1]) for h in vh])
            tok = slice(unit_halves[i] * C, (unit_halves[i] + 1) * C)
            gr.append(jnp.concatenate([g_rows[unit_tiles[i]][h:h + 1, tok] for h in vh], axis=1))
            g_last.append([gr[i][:, (j + 1) * C - 1:(j + 1) * C] for j in range(V_PER_K)])
        return beta, gc, gr, g_last

    def neumann_level(p, x):
        for i in range(n_units):
            rhs = jnp.concatenate([block_diag(p[i]), block_diag(x[i])], axis=1)
            pp_px = jnp.dot(p[i].astype(BF16), rhs, preferred_element_type=F32)
            x[i] = x[i] + pp_px[:, 2 * C:]
            p[i] = pp_px[:, :2 * C]

    n_levels = C.bit_length() - 2

    def stage_a1(it):
        r0s = unit_rows(it)
        q, k, v = [], [], []
        for i in range(n_units):
            y = conv_silu(unit_heads[i], r0s[i])
            q.append(l2n(y[:, raw_off[0]:raw_off[1]]) * (DN_K_DIM ** -0.5))
            k.append(l2n(y[:, raw_off[1]:raw_off[2]]))
            v.append(y[:, raw_off[2]:])
        yield
        qk_kk = []
        for i in range(n_units):
            kb = k[i].astype(BF16)
            qk_kk.append(lax.dot_general(jnp.concatenate([q[i].astype(BF16), kb], axis=0),
                                         jnp.concatenate([kb, kb], axis=0),
                                         (((1,), (1,)), ((), ())), preferred_element_type=F32))
        yield
        beta, gc, gr, _ = gate_terms(it, r0s)
        x, p = [], []
        for i in range(n_units):
            beta_p = jnp.where(lo, beta[i][0], beta[i][1])
            gc_p = jnp.where(lo, gc[i][0], gc[i][1])
            decay = jnp.exp(jnp.where(tril, gc_p - gr[i], -jnp.inf))
            a = jnp.where(strict, beta_p * qk_kk[i][C:] * decay, 0.0)
            qkd_ref[unit_heads[i], pl.ds(r0s[i], C), :] = (qk_kk[i][:C] * decay).astype(BF16)
            x.append(eye - a)
            p.append(jnp.dot(a.astype(BF16), block_diag(a), preferred_element_type=F32))
        yield
        for _ in range(A1_LEVELS):
            neumann_level(p, x)
            yield
        for i in range(n_units):
            mid_ref[it % 2, i] = jnp.concatenate([p[i], x[i], q[i], k[i], v[i]], axis=1)

    def stage_a2(it):
        r0s = unit_rows(it)
        mid = [mid_ref[it % 2, i] for i in range(n_units)]
        p = [m[:, mid_cols["p"]] for m in mid]
        x = [m[:, mid_cols["x"]] for m in mid]
        for _ in range(n_levels - A1_LEVELS):
            neumann_level(p, x)
            yield
        beta, gc, _, g_last = gate_terms(it, r0s)
        sols = []
        for i in range(n_units):
            q, k, v = (mid[i][:, mid_cols[name]] for name in ("q", "k", "v"))
            eg = [jnp.exp(gc[i][j]) for j in range(V_PER_K)]
            rhs = jnp.concatenate(
                [jnp.concatenate([v[:, vslices[j]] * beta[i][j],
                                  k * (beta[i][j] * eg[j])], axis=1) for j in range(V_PER_K)],
                axis=0).astype(BF16)
            sols.append([jnp.dot(jnp.where(lane_sel[j], x[i], 0.0).astype(BF16), rhs,
                                 preferred_element_type=F32) for j in range(V_PER_K)])
            hh, r0 = unit_heads[i], r0s[i]
            qe_ref[hh, pl.ds(r0, C), :] = jnp.concatenate(
                [q * eg[j] for j in range(V_PER_K)], axis=1).astype(BF16)
            kd_ref[hh, pl.ds(pl.multiple_of(V_PER_K * r0, V_PER_K * C), V_PER_K * C), :] = jnp.concatenate(
                [k * jnp.exp(g_last[i][j] - gc[i][j]) for j in range(V_PER_K)],
                axis=0).astype(BF16)
        yield
        for i in range(n_units):
            hh, r0 = unit_heads[i], r0s[i]
            u_ref[hh, pl.ds(r0, C), :] = jnp.concatenate(
                [s[:, :DN_V_DIM] for s in sols[i]], axis=1)
            w_ref[hh, pl.ds(r0, C), :] = jnp.concatenate(
                [s[:, DN_V_DIM:] for s in sols[i]], axis=1).astype(BF16)

    def stage_b(c):
        r0 = pl.multiple_of(c * C, C)
        zero_blk = jnp.zeros((DN_K_DIM, DN_V_DIM), BF16)
        first = lax.broadcasted_iota(jnp.int32, (C, VW), 1) < DN_V_DIM
        ws_qs, v_new = [], []
        for hh in range(DN_HG):
            s0 = st_ref[hh * V_PER_K].astype(BF16)
            s1 = st_ref[hh * V_PER_K + 1].astype(BF16)
            s_bd = jnp.concatenate([jnp.concatenate([s0, zero_blk], axis=1),
                                    jnp.concatenate([zero_blk, s1], axis=1)], axis=0)
            ws_qs.append(jnp.dot(jnp.concatenate([w_ref[hh, pl.ds(r0, C), :],
                                                  qe_ref[hh, pl.ds(r0, C), :]], axis=0),
                                 s_bd, preferred_element_type=F32))
            v_new.append((u_ref[hh, pl.ds(r0, C), :] - ws_qs[hh][:C]).astype(BF16))
        yield
        v_bd = []
        for hh in range(DN_HG):
            zeros = jnp.zeros_like(v_new[hh])
            v_bd.append(jnp.concatenate([jnp.where(first, v_new[hh], zeros),
                                         jnp.where(first, zeros, v_new[hh])], axis=0))
            kd = kd_ref[hh, pl.ds(pl.multiple_of(V_PER_K * r0, V_PER_K * C), V_PER_K * C), :]
            kv = lax.dot_general(kd, v_bd[hh], (((0,), (0,)), ((), ())),
                                 preferred_element_type=F32)
            for j in range(V_PER_K):
                gl = GROUP_VH + hh * V_PER_K + j
                g_last = col_ref[0, 0, pl.ds(r0 + C - 1, 1), gl:gl + 1]
                st_ref[hh * V_PER_K + j] = (st_ref[hh * V_PER_K + j] * jnp.exp(g_last)
                                            + kv[:, vslices[j]])
        yield
        for hh in range(DN_HG):
            o_both = ws_qs[hh][C:] + jnp.dot(qkd_ref[hh, pl.ds(r0, C), :], v_bd[hh],
                                             preferred_element_type=F32)
            for j in range(V_PER_K):
                o = o_both[:, vslices[j]]
                on = o * lax.rsqrt(jnp.mean(o * o, axis=-1, keepdims=True) + EPS) * nw_ref[...]
                ocols = slice(hh * VW + j * DN_V_DIM, hh * VW + (j + 1) * DN_V_DIM)
                zz = z_ref[0, pl.ds(r0, C), ocols].astype(F32)
                o_ref[0, pl.ds(r0, C), ocols] = (on * _silu(zz)).astype(o_ref.dtype)

    def recurrence(it):
        for cc in range(DN_A_CHUNKS):
            yield from stage_b(it * DN_A_CHUNKS + cc)
            yield

    def emit(*streams):
        live = [iter(g) for g in streams]
        while live:
            live = [g for g in live if next(g, live) is not live]

    def stage_a(it):
        yield from stage_a1(it)
        yield
        yield from stage_a2(it)

    n_iter = DN_TS // (C * DN_A_CHUNKS)
    emit(stage_a(0))

    def body(it, carry):
        emit(stage_a(it + 1), recurrence(it))
        return carry

    lax.fori_loop(0, n_iter - 1, body, 0)
    emit(recurrence(n_iter - 1))


def _deltanet(proj3, colp, rowp, conv_w, dn_norm_w):
    b, s, _ = proj3.shape
    kw = DN_HG * DN_K_DIM
    vw = DN_HG * VW
    q0, k0, v0, z0 = P_DQ // kw, P_DK // kw, P_DV // vw, P_DZ // vw
    ck0, cv0 = DN_QK_W // kw, 2 * DN_QK_W // vw
    hb = DN_TS // HALO_BLK

    def halo(t):
        return jnp.maximum(t * hb - 1, 0)

    return pl.pallas_call(
        _dnet_kernel,
        grid=(b, DN_K_HEADS // DN_HG, s // DN_TS),
        in_specs=[
            pl.BlockSpec((1, DN_TS, kw), lambda i, g, t: (i, t, q0 + g)),
            pl.BlockSpec((1, DN_TS, kw), lambda i, g, t: (i, t, k0 + g)),
            pl.BlockSpec((1, DN_TS, vw), lambda i, g, t: (i, t, v0 + g)),
            pl.BlockSpec((1, DN_TS, vw), lambda i, g, t: (i, t, z0 + g)),
            pl.BlockSpec((1, HALO_BLK, kw), lambda i, g, t: (i, halo(t), q0 + g)),
            pl.BlockSpec((1, HALO_BLK, kw), lambda i, g, t: (i, halo(t), k0 + g)),
            pl.BlockSpec((1, HALO_BLK, vw), lambda i, g, t: (i, halo(t), v0 + g)),
            pl.BlockSpec((1, 1, DN_TS, SMALL_W), lambda i, g, t: (i, g, t, 0)),
            pl.BlockSpec((1, DN_TS // GATE_ROWS, GROUP_VH, GATE_ROWS), lambda i, g, t: (i, t, g, 0)),
            pl.BlockSpec((CONV_WIDTH, kw), lambda i, g, t: (0, g)),
            pl.BlockSpec((CONV_WIDTH, kw), lambda i, g, t: (0, ck0 + g)),
            pl.BlockSpec((CONV_WIDTH, vw), lambda i, g, t: (0, cv0 + g)),
            pl.BlockSpec((1, DN_V_DIM), lambda i, g, t: (0, 0)),
        ],
        out_specs=pl.BlockSpec((1, DN_TS, vw), lambda i, g, t: (i, t, g)),
        out_shape=jax.ShapeDtypeStruct((b, s, DN_V_W), BF16),
        scratch_shapes=[
            pltpu.VMEM((DN_HG * V_PER_K, DN_K_DIM, DN_V_DIM), F32),
            pltpu.VMEM((DN_HG, DN_TS, VW), F32),
            pltpu.VMEM((DN_HG, DN_TS, VW), BF16),
            pltpu.VMEM((DN_HG, DN_TS, VW), BF16),
            pltpu.VMEM((DN_HG, V_PER_K * DN_TS, DN_K_DIM), BF16),
            pltpu.VMEM((DN_HG, DN_TS, V_PER_K * CHUNK), BF16),
            pltpu.VMEM((CHUNK + DN_TS, DN_HG * RAW_W), BF16),
            pltpu.VMEM((2, DN_A_CHUNKS * DN_HG, CHUNK, 4 * CHUNK + 2 * DN_K_DIM + VW), F32),
        ],
        compiler_params=pltpu.CompilerParams(
            dimension_semantics=("parallel", "parallel", "arbitrary"),
            vmem_limit_bytes=V7X_VMEM_LIMIT),
        name="dnet",
    )(proj3, proj3, proj3, proj3, proj3, proj3, proj3, colp, rowp, conv_w, conv_w, conv_w,
      dn_norm_w)


MERGE_TM = 512
MERGE_TN = 512


def _merge_kernel(oa_ref, od_ref, wa_ref, wd_ref, ga_ref, gd_ref, o_ref):
    ya = jnp.dot(oa_ref[...], wa_ref[...], preferred_element_type=F32)
    yd = jnp.dot(od_ref[...], wd_ref[...], preferred_element_type=F32)
    ga = _sigmoid(ga_ref[...].astype(F32))
    gd = _sigmoid(gd_ref[...].astype(F32))
    o_ref[...] = (ga * ya + gd * yd).astype(o_ref.dtype)


def _merge(oa, od, wa, wd, proj2):
    m = oa.shape[0]
    g0 = P_G // MERGE_TN
    g1 = (P_G + D_MODEL) // MERGE_TN
    return pl.pallas_call(
        _merge_kernel,
        grid=(m // MERGE_TM, D_MODEL // MERGE_TN),
        in_specs=[
            pl.BlockSpec((MERGE_TM, ATT_W), lambda i, j: (i, 0)),
            pl.BlockSpec((MERGE_TM, DN_V_W), lambda i, j: (i, 0)),
            pl.BlockSpec((ATT_W, MERGE_TN), lambda i, j: (0, j)),
            pl.BlockSpec((DN_V_W, MERGE_TN), lambda i, j: (0, j)),
            pl.BlockSpec((MERGE_TM, MERGE_TN), lambda i, j: (i, g0 + j)),
            pl.BlockSpec((MERGE_TM, MERGE_TN), lambda i, j: (i, g1 + j)),
        ],
        out_specs=pl.BlockSpec((MERGE_TM, MERGE_TN), lambda i, j: (i, j)),
        out_shape=jax.ShapeDtypeStruct((m, D_MODEL), BF16),
        compiler_params=pltpu.CompilerParams(
            dimension_semantics=("parallel", "parallel"),
            vmem_limit_bytes=V7X_VMEM_LIMIT),
        name="merge",
    )(oa, od, wa, wd, proj2, proj2)


OUT_TM = 512


def _out_kernel(m_ref, w_ref, x_ref, nw_ref, o_ref):
    y = x_ref[...] + jnp.dot(m_ref[...], w_ref[...], preferred_element_type=F32)
    ms = jnp.mean(y * y, axis=-1, keepdims=True)
    o_ref[...] = (y * lax.rsqrt(ms + EPS)) * nw_ref[...]


def _outproj(merged, w_out, x2, final_norm_w):
    m = x2.shape[0]
    return pl.pallas_call(
        _out_kernel,
        grid=(m // OUT_TM,),
        in_specs=[
            pl.BlockSpec((OUT_TM, D_MODEL), lambda i: (i, 0)),
            pl.BlockSpec((D_MODEL, D_MODEL), lambda i: (0, 0)),
            pl.BlockSpec((OUT_TM, D_MODEL), lambda i: (i, 0)),
            pl.BlockSpec((1, D_MODEL), lambda i: (0, 0)),
        ],
        out_specs=pl.BlockSpec((OUT_TM, D_MODEL), lambda i: (i, 0)),
        out_shape=jax.ShapeDtypeStruct((m, D_MODEL), F32),
        compiler_params=pltpu.CompilerParams(
            dimension_semantics=("parallel",),
            vmem_limit_bytes=V7X_VMEM_LIMIT),
        name="outproj",
    )(merged, w_out, x2, final_norm_w)


def _layer(x, norm_w, w_in, b_qkv, sinks, conv_w, a_log, dt_bias, dn_norm_w,
           w_att_branch, w_dn_branch, w_out, out_norm_w):
    b, s, d = x.shape
    m = b * s
    x2 = x.reshape(m, d)

    wk = w_in[:, OFF_AK:OFF_AV].reshape(d, N_KV_HEADS, 1, HEAD_DIM)
    wv = w_in[:, OFF_AV:OFF_AZ].reshape(d, N_KV_HEADS, 1, HEAD_DIM)
    w_kv = jnp.concatenate([wk, wv], axis=2).reshape(d, 2 * ATT_KV_W)
    w_tail = jnp.concatenate([w_in[:, OFF_G:], w_kv], axis=1).astype(BF16)
    w_small = jnp.pad(w_in[:, OFF_DB:OFF_G], ((0, 0), (0, SMALL_W - 2 * DN_V_HEADS))).astype(BF16)
    bk = b_qkv[ATT_Q_W:ATT_Q_W + ATT_KV_W].reshape(N_KV_HEADS, 1, HEAD_DIM)
    bv = b_qkv[ATT_Q_W + ATT_KV_W:].reshape(N_KV_HEADS, 1, HEAD_DIM)
    bias = jnp.concatenate([b_qkv[:ATT_Q_W], jnp.zeros((P_KV - P_AZ,), F32),
                            jnp.concatenate([bk, bv], axis=1).reshape(-1)]).reshape(1, MAIN_W)

    proj, small = _inproj(x2, norm_w.reshape(1, d), w_in, w_tail, bias, w_small)
    proj3 = proj.reshape(b, s, MAIN_W)

    lane_pad = (DN_V_HEADS, SMALL_W - 2 * DN_V_HEADS)
    alog_row = jnp.pad(a_log, lane_pad).reshape(1, SMALL_W)
    dt_row = jnp.pad(dt_bias, lane_pad).reshape(1, SMALL_W)
    colp, rowp = _gates(small.reshape(b, s, SMALL_W), alog_row, dt_row)

    o_att = _attention(proj3, sinks)
    o_dn = _deltanet(proj3, colp, rowp, conv_w, dn_norm_w.reshape(1, DN_V_DIM))

    merged = _merge(o_att.reshape(m, ATT_W), o_dn.reshape(m, DN_V_W),
                    w_att_branch.astype(BF16), w_dn_branch.astype(BF16), proj)
    y = _outproj(merged, w_out.astype(BF16), x2, out_norm_w.reshape(1, d))
    return y.reshape(b, s, d)


def kernel(x, norm_w, w_in, b_qkv, sinks, conv_w, a_log, dt_bias, dn_norm_w,
           w_att_branch, w_dn_branch, w_out, final_norm_w):
    depth = norm_w.shape[0]
    assert depth == 1, "the final RMSNorm is fused into the single layer's output kernel"
    return _layer(x, norm_w[0], w_in[0], b_qkv[0], sinks[0], conv_w[0], a_log[0], dt_bias[0],
                  dn_norm_w[0], w_att_branch[0], w_dn_branch[0], w_out[0], final_norm_w)
```

```python
import functools

import jax
import jax.numpy as jnp
from jax import lax
from jax.experimental import pallas as pl
from jax.experimental.pallas import tpu as pltpu

F32 = jnp.float32
BF16 = jnp.bfloat16

D_MODEL = 2048
HEAD_DIM = 64
N_Q_HEADS = 32
N_KV_HEADS = 4
GROUP = N_Q_HEADS // N_KV_HEADS
WINDOW = 128
ATT_BLOCK = 128
ATT_Q_W = N_Q_HEADS * HEAD_DIM
ATT_KV_W = N_KV_HEADS * HEAD_DIM
ATT_W = ATT_Q_W

DN_K_HEADS = 16
DN_V_HEADS = 32
DN_K_DIM = 128
DN_V_DIM = 128
DN_QK_W = DN_K_HEADS * DN_K_DIM
DN_V_W = DN_V_HEADS * DN_V_DIM
DN_CONV_CH = 2 * DN_QK_W + DN_V_W
CONV_WIDTH = 4
CHUNK = 64
EPS = 1e-6

OFF_AQ = 0
OFF_AK = OFF_AQ + ATT_Q_W
OFF_AV = OFF_AK + ATT_KV_W
OFF_AZ = OFF_AV + ATT_KV_W
OFF_DQKV = OFF_AZ + ATT_W
OFF_DZ = OFF_DQKV + DN_CONV_CH
OFF_DB = OFF_DZ + DN_V_W
OFF_DA = OFF_DB + DN_V_HEADS
OFF_G = OFF_DA + DN_V_HEADS
IN_W = OFF_G + 2 * D_MODEL

MAIN_W = IN_W - 2 * DN_V_HEADS
P_AQ = 0
P_AZ = P_AQ + ATT_Q_W
P_DQ = P_AZ + ATT_W
P_DK = P_DQ + DN_QK_W
P_DV = P_DK + DN_QK_W
P_DZ = P_DV + DN_V_W
P_G = P_DZ + DN_V_W
P_KV = P_G + 2 * D_MODEL
SMALL_W = 128

V7X_VMEM_LIMIT = 56 * 1024 * 1024


def _sigmoid(v):
    return 0.5 * jnp.tanh(0.5 * v) + 0.5


def _silu(v):
    h = 0.5 * v
    return h * jnp.tanh(h) + h


def _mm(a, b):
    return jnp.dot(a.astype(BF16), b.astype(BF16), preferred_element_type=F32)


INPROJ_TM = 2048
INPROJ_TN = 512
NORM_ROWS = 128


NT_DIMS = (((1,), (1,)), ((), ()))


def _inproj_kernel(x_ref, nw_ref, wt_ref, wkv_ref, b_ref, ws_ref, o_ref, os_ref, h_ref):
    j = pl.program_id(1)

    @pl.when(j == 0)
    def _():
        def body(r, carry):
            r0 = pl.multiple_of(r * NORM_ROWS, NORM_ROWS)
            xv = x_ref[pl.ds(r0, NORM_ROWS), :]
            ms = jnp.mean(xv * xv, axis=-1, keepdims=True)
            hv = (xv * lax.rsqrt(ms + EPS)) * nw_ref[...]
            h_ref[pl.ds(r0, NORM_ROWS), :] = hv.astype(BF16)
            return carry
        lax.fori_loop(0, INPROJ_TM // NORM_ROWS, body, 0)
        os_ref[...] = lax.dot_general(h_ref[...], ws_ref[...], NT_DIMS, preferred_element_type=F32)

    def project(w_ref):
        acc = lax.dot_general(h_ref[...], w_ref[...].astype(BF16), NT_DIMS,
                              preferred_element_type=F32)
        o_ref[...] = (acc + b_ref[...]).astype(o_ref.dtype)

    @pl.when(j < DIRECT_BLOCKS)
    def _():
        project(wt_ref)

    @pl.when(j >= DIRECT_BLOCKS)
    def _():
        project(wkv_ref)


DIRECT_BLOCKS = P_KV // INPROJ_TN


def _inproj(x2, norm_w, w_t, w_kv_t, bias, w_small_t):
    m = x2.shape[0]
    grid = (m // INPROJ_TM, MAIN_W // INPROJ_TN)
    n_q, n_mid = ATT_Q_W // INPROJ_TN, P_G // INPROJ_TN

    def direct_rows(i, j):
        jj = jnp.minimum(j, DIRECT_BLOCKS - 1)
        row = jnp.where(jj < n_q, OFF_AQ + INPROJ_TN * jj,
                        jnp.where(jj < n_mid, OFF_AZ + INPROJ_TN * (jj - n_q),
                                  OFF_G + INPROJ_TN * (jj - n_mid)))
        return pl.multiple_of(row, HEAD_DIM), 0

    return pl.pallas_call(
        _inproj_kernel,
        grid=grid,
        in_specs=[
            pl.BlockSpec((INPROJ_TM, D_MODEL), lambda i, j: (i, 0), pipeline_mode=pl.Buffered(1)),
            pl.BlockSpec((1, D_MODEL), lambda i, j: (0, 0)),
            pl.BlockSpec((pl.Element(INPROJ_TN), pl.Element(D_MODEL)), direct_rows),
            pl.BlockSpec((INPROJ_TN, D_MODEL), lambda i, j: (0, 0)),
            pl.BlockSpec((1, INPROJ_TN), lambda i, j: (0, j)),
            pl.BlockSpec((SMALL_W, D_MODEL), lambda i, j: (0, 0)),
        ],
        out_specs=[
            pl.BlockSpec((INPROJ_TM, INPROJ_TN), lambda i, j: (i, j)),
            pl.BlockSpec((INPROJ_TM, SMALL_W), lambda i, j: (i, 0)),
        ],
        out_shape=[
            jax.ShapeDtypeStruct((m, MAIN_W), BF16),
            jax.ShapeDtypeStruct((m, SMALL_W), F32),
        ],
        scratch_shapes=[pltpu.VMEM((INPROJ_TM, D_MODEL), BF16)],
        compiler_params=pltpu.CompilerParams(
            dimension_semantics=("parallel", "arbitrary"),
            vmem_limit_bytes=V7X_VMEM_LIMIT),
        name="inproj",
    )(x2, norm_w, w_t, w_kv_t, bias, w_small_t)


GATE_ROWS = 2 * CHUNK
DN_HG = 4
GATE_GROUPS = DN_K_HEADS // DN_HG
GROUP_VH = DN_V_HEADS // GATE_GROUPS


def _gates_kernel(s_ref, alog_ref, dt_ref, col_ref, row_ref):
    seq = s_ref.shape[1]
    row = lax.broadcasted_iota(jnp.int32, (CHUNK, SMALL_W), 0)
    lane = lax.broadcasted_iota(jnp.int32, (GATE_ROWS, SMALL_W), 1)

    def body(blk, carry):
        r0 = pl.multiple_of(blk * GATE_ROWS, GATE_ROWS)
        xs = s_ref[0, pl.ds(r0, GATE_ROWS), :]
        beta = _sigmoid(xs)
        t = xs + dt_ref[...]
        softplus = jnp.maximum(t, 0.0) + jnp.log(1.0 + jnp.exp(-jnp.abs(t)))
        g = -jnp.exp(alog_ref[...]) * softplus
        halves = []
        for part in range(GATE_ROWS // CHUNK):
            gp = g[part * CHUNK:(part + 1) * CHUNK]
            shift = 1
            while shift < CHUNK:
                gp = gp + jnp.where(row >= shift, pltpu.roll(gp, shift, axis=0), 0.0)
                shift *= 2
            halves.append(gp)
        gcum = jnp.concatenate(halves, axis=0)
        for grp in range(GATE_GROUPS):
            b_g = pltpu.roll(beta, (SMALL_W - grp * GROUP_VH) % SMALL_W, axis=1)
            g_g = pltpu.roll(gcum, SMALL_W - DN_V_HEADS - grp * GROUP_VH + GROUP_VH, axis=1)
            col_ref[0, grp, pl.ds(r0, GATE_ROWS), :] = jnp.where(lane < GROUP_VH, b_g, g_g)
        row_ref[0, blk] = gcum.T[DN_V_HEADS:2 * DN_V_HEADS]
        return carry

    lax.fori_loop(0, seq // GATE_ROWS, body, 0)


def _gates(small3, alog_row, dt_row):
    b, s, _ = small3.shape
    return pl.pallas_call(
        _gates_kernel,
        grid=(b,),
        in_specs=[
            pl.BlockSpec((1, s, SMALL_W), lambda i: (i, 0, 0)),
            pl.BlockSpec((1, SMALL_W), lambda i: (0, 0)),
            pl.BlockSpec((1, SMALL_W), lambda i: (0, 0)),
        ],
        out_specs=[
            pl.BlockSpec((1, GATE_GROUPS, s, SMALL_W), lambda i: (i, 0, 0, 0)),
            pl.BlockSpec((1, s // GATE_ROWS, DN_V_HEADS, GATE_ROWS), lambda i: (i, 0, 0, 0)),
        ],
        out_shape=[
            jax.ShapeDtypeStruct((b, GATE_GROUPS, s, SMALL_W), F32),
            jax.ShapeDtypeStruct((b, s // GATE_ROWS, DN_V_HEADS, GATE_ROWS), F32),
        ],
        compiler_params=pltpu.CompilerParams(dimension_semantics=("parallel",)),
        name="gates",
    )(small3, alog_row, dt_row)


PAIR_W = 2 * HEAD_DIM
KVH_W = GROUP * HEAD_DIM


def _attn_kernel(sinks_ref, q_ref, kvp_ref, kvc_ref, z_ref, o_ref):
    n = pl.program_id(1)
    L = ATT_BLOCK
    lane = lax.broadcasted_iota(jnp.int32, (L, PAIR_W), 1)
    lo = lane < HEAD_DIM

    def prep(kv):
        kv = kv.astype(F32)
        sw = pltpu.roll(kv, HEAD_DIM, axis=1)
        kk = jnp.where(lo, kv, sw).astype(BF16)
        va = jnp.where(lo, sw, 0.0).astype(BF16)
        vb = jnp.where(lo, 0.0, kv).astype(BF16)
        return kk, va, vb

    qi = lax.broadcasted_iota(jnp.int32, (L, 2 * L), 0)
    ki = lax.broadcasted_iota(jnp.int32, (L, 2 * L), 1)
    rel = qi + L - ki
    valid = (rel >= 0) & (rel < WINDOW) & ((ki >= L) | (n > 0))

    def kv_head(h):
        kcols = slice(h * PAIR_W, (h + 1) * PAIR_W)
        kk_p, va_p, vb_p = prep(kvp_ref[0, :, kcols])
        kk_c, va_c, vb_c = prep(kvc_ref[0, :, kcols])
        kk = jnp.concatenate([kk_p, kk_c], axis=0)
        va = jnp.concatenate([va_p, va_c], axis=0)
        vb = jnp.concatenate([vb_p, vb_c], axis=0)
        pair_cols = [slice(h * KVH_W + pr * PAIR_W, h * KVH_W + (pr + 1) * PAIR_W)
                     for pr in range(GROUP // 2)]
        scores = []
        for cols in pair_cols:
            q2 = q_ref[0, :, cols].astype(F32) * (HEAD_DIM ** -0.5)
            for qm in (jnp.where(lo, q2, 0.0), jnp.where(lo, 0.0, q2)):
                scores.append(lax.dot_general(qm.astype(BF16), kk, (((1,), (1,)), ((), ())),
                                              preferred_element_type=F32))
        probs, inv = [], []
        for i, s in enumerate(scores):
            sink = sinks_ref[h * GROUP + i]
            s = jnp.where(valid, s, -jnp.inf)
            m = jnp.maximum(jnp.max(s, axis=-1, keepdims=True), sink)
            p = jnp.exp(s - m)
            inv.append(1.0 / (jnp.sum(p, axis=-1, keepdims=True) + jnp.exp(sink - m)))
            probs.append(p.astype(BF16))
        for pr, cols in enumerate(pair_cols):
            o2 = (jnp.dot(probs[2 * pr], va, preferred_element_type=F32)
                  + jnp.dot(probs[2 * pr + 1], vb, preferred_element_type=F32))
            o2 = o2 * jnp.where(lo, inv[2 * pr], inv[2 * pr + 1])
            zz = z_ref[0, :, cols].astype(F32)
            o_ref[0, :, cols] = (o2 * _silu(zz)).astype(o_ref.dtype)

    for h in range(N_KV_HEADS):
        kv_head(h)


def _attention(proj3, sinks):
    b, s, _ = proj3.shape
    nb = s // ATT_BLOCK
    kvw = 2 * ATT_KV_W
    kv0 = P_KV // kvw
    z0 = P_AZ // ATT_W
    return pl.pallas_call(
        _attn_kernel,
        grid=(b, nb),
        in_specs=[
            pl.BlockSpec(memory_space=pltpu.SMEM),
            pl.BlockSpec((1, ATT_BLOCK, ATT_Q_W), lambda i, n: (i, n, 0)),
            pl.BlockSpec((1, ATT_BLOCK, kvw), lambda i, n: (i, jnp.maximum(n - 1, 0), kv0)),
            pl.BlockSpec((1, ATT_BLOCK, kvw), lambda i, n: (i, n, kv0)),
            pl.BlockSpec((1, ATT_BLOCK, ATT_W), lambda i, n: (i, n, z0)),
        ],
        out_specs=pl.BlockSpec((1, ATT_BLOCK, ATT_W), lambda i, n: (i, n, 0)),
        out_shape=jax.ShapeDtypeStruct((b, s, ATT_W), BF16),
        compiler_params=pltpu.CompilerParams(
            dimension_semantics=("parallel", "parallel")),
        name="attn",
    )(sinks, proj3, proj3, proj3, proj3)


V_PER_K = DN_V_HEADS // DN_K_HEADS
assert V_PER_K == 2 and V_PER_K * CHUNK == 128, "two value heads are packed side by side in 128 lanes"
VW = V_PER_K * DN_V_DIM
DN_TS = 1024
DN_A_CHUNKS = 2
assert DN_A_CHUNKS % (GATE_ROWS // CHUNK) == 0
A1_LEVELS = 2
RAW_W = 2 * DN_K_DIM + VW
HALO_BLK = 16


def _dnet_kernel(q_ref, k_ref, v_ref, z_ref, qh_ref, kh_ref, vh_ref, col_ref, row_ref,
                 cwq_ref, cwk_ref, cwv_ref, nw_ref, o_ref,
                 st_ref, u_ref, w_ref, qe_ref, kd_ref, qkd_ref, raw_ref, mid_ref):
    t = pl.program_id(2)
    C = CHUNK
    ri = lax.broadcasted_iota(jnp.int32, (C, 2 * C), 0)
    li = lax.broadcasted_iota(jnp.int32, (C, 2 * C), 1)
    lo = li < C
    ci = jnp.where(lo, li, li - C)
    tril = ri >= ci
    strict = ri > ci
    eye = (ri == ci).astype(F32)
    top = lax.broadcasted_iota(jnp.int32, (2 * C, 2 * C), 0) < C
    left = lax.broadcasted_iota(jnp.int32, (2 * C, 2 * C), 1) < C
    diag_blk = top == left

    @pl.when(t == 0)
    def _():
        st_ref[...] = jnp.zeros_like(st_ref)

    def head_cols(ref, hh):
        per_head = ref.shape[2] // DN_HG
        return slice(hh * per_head, (hh + 1) * per_head)

    raw_off = (0, DN_K_DIM, 2 * DN_K_DIM)
    raw_ref[0:C - HALO_BLK, :] = jnp.zeros((C - HALO_BLK, raw_ref.shape[1]), BF16)
    for ref, halo_ref, off in zip((q_ref, k_ref, v_ref), (qh_ref, kh_ref, vh_ref), raw_off):
        tail = halo_ref[0]
        tail = jnp.where(t > 0, tail, jnp.zeros_like(tail))
        for hh in range(DN_HG):
            hc = head_cols(ref, hh)
            raw_ref[C - HALO_BLK:C, hh * RAW_W + off:hh * RAW_W + off + hc.stop - hc.start] = tail[:, hc]

    def stage_rows(r, carry):
        r0 = pl.multiple_of(r * C, C)
        for ref, off in zip((q_ref, k_ref, v_ref), raw_off):
            for hh in range(DN_HG):
                hc = head_cols(ref, hh)
                raw_ref[pl.ds(r0 + C, C), hh * RAW_W + off:hh * RAW_W + off + hc.stop - hc.start] = (
                    ref[0, pl.ds(r0, C), hc])
        return carry

    lax.fori_loop(0, DN_TS // C, stage_rows, 0)

    sr = lax.broadcasted_iota(jnp.int32, ((CONV_WIDTH - 1) * C, 2 * C), 0)
    sc = lax.broadcasted_iota(jnp.int32, ((CONV_WIDTH - 1) * C, 2 * C), 1)
    shift_mat = (sc == C + jnp.bitwise_and(sr, C - 1)
                 - (jnp.right_shift(sr, C.bit_length() - 1) + 1)).astype(BF16)

    def conv_silu(hh, r0):
        rcols = slice(hh * RAW_W, (hh + 1) * RAW_W)
        w = jnp.concatenate([cwq_ref[:, head_cols(q_ref, hh)], cwk_ref[:, head_cols(k_ref, hh)],
                             cwv_ref[:, head_cols(v_ref, hh)]], axis=1)
        window = raw_ref[pl.ds(r0, 2 * C), rcols]
        taps = jnp.dot(shift_mat, window, preferred_element_type=F32)
        y = window[C:].astype(F32) * w[CONV_WIDTH - 1:CONV_WIDTH, :]
        for s in range(1, CONV_WIDTH):
            y = y + taps[(s - 1) * C:s * C] * w[CONV_WIDTH - 1 - s:CONV_WIDTH - s, :]
        return _silu(y)

    def l2n(v):
        return v * lax.rsqrt(jnp.sum(v * v, axis=-1, keepdims=True) + EPS)

    def bcast(col):
        return jnp.broadcast_to(col, (C, 2 * C))

    def block_diag(m):
        return jnp.where(diag_blk, jnp.concatenate([m, m], axis=0), 0.0).astype(BF16)

    vslices = [slice(j * DN_V_DIM, (j + 1) * DN_V_DIM) for j in range(V_PER_K)]
    lane_sel = [lo, jnp.logical_not(lo)]


    per_tile = GATE_ROWS // C
    unit_heads = [hh for _ in range(DN_A_CHUNKS) for hh in range(DN_HG)]
    unit_halves = [cc % per_tile for cc in range(DN_A_CHUNKS) for _ in range(DN_HG)]
    unit_tiles = [cc // per_tile for cc in range(DN_A_CHUNKS) for _ in range(DN_HG)]
    n_units = len(unit_heads)
    mid_cols, offset = {}, 0
    for name, width in (("p", 2 * C), ("x", 2 * C), ("q", DN_K_DIM), ("k", DN_K_DIM), ("v", VW)):
        mid_cols[name] = slice(offset, offset + width)
        offset += width

    def unit_rows(it):
        return [pl.multiple_of((it * DN_A_CHUNKS + cc) * C, C)
                for cc in range(DN_A_CHUNKS) for _ in range(DN_HG)]

    def gate_terms(it, r0s):
        g_rows = [row_ref[0, it * (DN_A_CHUNKS // per_tile) + r] for r in range(DN_A_CHUNKS // per_tile)]
        beta, gc, gr, g_last = [], [], [], []
        for i in range(n_units):
            colp = col_ref[0, 0, pl.ds(r0s[i], C), :]
            vh = [unit_heads[i] * V_PER_K + j for j in range(V_PER_K)]
            beta.append([bcast(colp[:, h:h + 1]) for h in vh])
            gc.append([bcast(colp[:, GROUP_VH + h:GROUP_VH + h + 1]) for h in vh])
            tok = slice(unit_halves[i] * C, (unit_halves[i] + 1) * C)
            gr.append(jnp.concatenate([g_rows[unit_tiles[i]][h:h + 1, tok] for h in vh], axis=1))
            g_last.append([gr[i][:, (j + 1) * C - 1:(j + 1) * C] for j in range(V_PER_K)])
        return beta, gc, gr, g_last

    def neumann_level(p, x):
        for i in range(n_units):
            rhs = jnp.concatenate([block_diag(p[i]), block_diag(x[i])], axis=1)
            pp_px = jnp.dot(p[i].astype(BF16), rhs, preferred_element_type=F32)
            x[i] = x[i] + pp_px[:, 2 * C:]
            p[i] = pp_px[:, :2 * C]

    n_levels = C.bit_length() - 2

    def stage_a1(it):
        r0s = unit_rows(it)
        q, k, v = [], [], []
        for i in range(n_units):
            y = conv_silu(unit_heads[i], r0s[i])
            q.append(l2n(y[:, raw_off[0]:raw_off[1]]) * (DN_K_DIM ** -0.5))
            k.append(l2n(y[:, raw_off[1]:raw_off[2]]))
            v.append(y[:, raw_off[2]:])
        yield
        qk_kk = []
        for i in range(n_units):
            kb = k[i].astype(BF16)
            qk_kk.append(lax.dot_general(jnp.concatenate([q[i].astype(BF16), kb], axis=0),
                                         jnp.concatenate([kb, kb], axis=0),
                                         (((1,), (1,)), ((), ())), preferred_element_type=F32))
        yield
        beta, gc, gr, _ = gate_terms(it, r0s)
        x, p = [], []
        for i in range(n_units):
            beta_p = jnp.where(lo, beta[i][0], beta[i][1])
            gc_p = jnp.where(lo, gc[i][0], gc[i][1])
            decay = jnp.exp(jnp.where(tril, gc_p - gr[i], -jnp.inf))
            a = jnp.where(strict, beta_p * qk_kk[i][C:] * decay, 0.0)
            qkd_ref[unit_heads[i], pl.ds(r0s[i], C), :] = (qk_kk[i][:C] * decay).astype(BF16)
            x.append(eye - a)
            p.append(jnp.dot(a.astype(BF16), block_diag(a), preferred_element_type=F32))
        yield
        for _ in range(A1_LEVELS):
            neumann_level(p, x)
            yield
        for i in range(n_units):
            mid_ref[it % 2, i] = jnp.concatenate([p[i], x[i], q[i], k[i], v[i]], axis=1)

    def stage_a2(it):
        r0s = unit_rows(it)
        mid = [mid_ref[it % 2, i] for i in range(n_units)]
        p = [m[:, mid_cols["p"]] for m in mid]
        x = [m[:, mid_cols["x"]] for m in mid]
        for _ in range(n_levels - A1_LEVELS):
            neumann_level(p, x)
            yield
        beta, gc, _, g_last = gate_terms(it, r0s)
        sols = []
        for i in range(n_units):
            q, k, v = (mid[i][:, mid_cols[name]] for name in ("q", "k", "v"))
            eg = [jnp.exp(gc[i][j]) for j in range(V_PER_K)]
            rhs = jnp.concatenate(
                [jnp.concatenate([v[:, vslices[j]] * beta[i][j],
                                  k * (beta[i][j] * eg[j])], axis=1) for j in range(V_PER_K)],
                axis=0).astype(BF16)
            sols.append([jnp.dot(jnp.where(lane_sel[j], x[i], 0.0).astype(BF16), rhs,
                                 preferred_element_type=F32) for j in range(V_PER_K)])
            hh, r0 = unit_heads[i], r0s[i]
            qe_ref[hh, pl.ds(r0, C), :] = jnp.concatenate(
                [q * eg[j] for j in range(V_PER_K)], axis=1).astype(BF16)
            kd_ref[hh, pl.ds(pl.multiple_of(V_PER_K * r0, V_PER_K * C), V_PER_K * C), :] = jnp.concatenate(
                [k * jnp.exp(g_last[i][j] - gc[i][j]) for j in range(V_PER_K)],
                axis=0).astype(BF16)
        yield
        for i in range(n_units):
            hh, r0 = unit_heads[i], r0s[i]
            u_ref[hh, pl.ds(r0, C), :] = jnp.concatenate(
                [s[:, :DN_V_DIM] for s in sols[i]], axis=1)
            w_ref[hh, pl.ds(r0, C), :] = jnp.concatenate(
                [s[:, DN_V_DIM:] for s in sols[i]], axis=1).astype(BF16)

    def stage_b(c):
        r0 = pl.multiple_of(c * C, C)
        zero_blk = jnp.zeros((DN_K_DIM, DN_V_DIM), BF16)
        first = lax.broadcasted_iota(jnp.int32, (C, VW), 1) < DN_V_DIM
        ws_qs, v_new = [], []
        for hh in range(DN_HG):
            s0 = st_ref[hh * V_PER_K].astype(BF16)
            s1 = st_ref[hh * V_PER_K + 1].astype(BF16)
            s_bd = jnp.concatenate([jnp.concatenate([s0, zero_blk], axis=1),
                                    jnp.concatenate([zero_blk, s1], axis=1)], axis=0)
            ws_qs.append(jnp.dot(jnp.concatenate([w_ref[hh, pl.ds(r0, C), :],
                                                  qe_ref[hh, pl.ds(r0, C), :]], axis=0),
                                 s_bd, preferred_element_type=F32))
            v_new.append((u_ref[hh, pl.ds(r0, C), :] - ws_qs[hh][:C]).astype(BF16))
        yield
        v_bd = []
        for hh in range(DN_HG):
            zeros = jnp.zeros_like(v_new[hh])
            v_bd.append(jnp.concatenate([jnp.where(first, v_new[hh], zeros),
                                         jnp.where(first, zeros, v_new[hh])], axis=0))
            kd = kd_ref[hh, pl.ds(pl.multiple_of(V_PER_K * r0, V_PER_K * C), V_PER_K * C), :]
            kv = lax.dot_general(kd, v_bd[hh], (((0,), (0,)), ((), ())),
                                 preferred_element_type=F32)
            for j in range(V_PER_K):
                gl = GROUP_VH + hh * V_PER_K + j
                g_last = col_ref[0, 0, pl.ds(r0 + C - 1, 1), gl:gl + 1]
                st_ref[hh * V_PER_K + j] = (st_ref[hh * V_PER_K + j] * jnp.exp(g_last)
                                            + kv[:, vslices[j]])
        yield
        for hh in range(DN_HG):
            o_both = ws_qs[hh][C:] + jnp.dot(qkd_ref[hh, pl.ds(r0, C), :], v_bd[hh],
                                             preferred_element_type=F32)
            for j in range(V_PER_K):
                o = o_both[:, vslices[j]]
                on = o * lax.rsqrt(jnp.mean(o * o, axis=-1, keepdims=True) + EPS) * nw_ref[...]
                ocols = slice(hh * VW + j * DN_V_DIM, hh * VW + (j + 1) * DN_V_DIM)
                zz = z_ref[0, pl.ds(r0, C), ocols].astype(F32)
                o_ref[0, pl.ds(r0, C), ocols] = (on * _silu(zz)).astype(o_ref.dtype)

    def recurrence(it):
        for cc in range(DN_A_CHUNKS):
            yield from stage_b(it * DN_A_CHUNKS + cc)
            yield

    def emit(*streams):
        live = [iter(g) for g in streams]
        while live:
            live = [g for g in live if next(g, live) is not live]

    def stage_a(it):
        yield from stage_a1(it)
        yield
        yield from stage_a2(it)

    n_iter = DN_TS // (C * DN_A_CHUNKS)
    emit(stage_a(0))

    def body(it, carry):
        emit(stage_a(it + 1), recurrence(it))
        return carry

    lax.fori_loop(0, n_iter - 1, body, 0)
    emit(recurrence(n_iter - 1))


def _deltanet(proj3, colp, rowp, conv_w, dn_norm_w):
    b, s, _ = proj3.shape
    kw = DN_HG * DN_K_DIM
    vw = DN_HG * VW
    q0, k0, v0, z0 = P_DQ // kw, P_DK // kw, P_DV // vw, P_DZ // vw
    ck0, cv0 = DN_QK_W // kw, 2 * DN_QK_W // vw
    hb = DN_TS // HALO_BLK

    def halo(t):
        return jnp.maximum(t * hb - 1, 0)

    return pl.pallas_call(
        _dnet_kernel,
        grid=(b, DN_K_HEADS // DN_HG, s // DN_TS),
        in_specs=[
            pl.BlockSpec((1, DN_TS, kw), lambda i, g, t: (i, t, q0 + g)),
            pl.BlockSpec((1, DN_TS, kw), lambda i, g, t: (i, t, k0 + g)),
            pl.BlockSpec((1, DN_TS, vw), lambda i, g, t: (i, t, v0 + g)),
            pl.BlockSpec((1, DN_TS, vw), lambda i, g, t: (i, t, z0 + g)),
            pl.BlockSpec((1, HALO_BLK, kw), lambda i, g, t: (i, halo(t), q0 + g)),
            pl.BlockSpec((1, HALO_BLK, kw), lambda i, g, t: (i, halo(t), k0 + g)),
            pl.BlockSpec((1, HALO_BLK, vw), lambda i, g, t: (i, halo(t), v0 + g)),
            pl.BlockSpec((1, 1, DN_TS, SMALL_W), lambda i, g, t: (i, g, t, 0)),
            pl.BlockSpec((1, DN_TS // GATE_ROWS, GROUP_VH, GATE_ROWS), lambda i, g, t: (i, t, g, 0)),
            pl.BlockSpec((CONV_WIDTH, kw), lambda i, g, t: (0, g)),
            pl.BlockSpec((CONV_WIDTH, kw), lambda i, g, t: (0, ck0 + g)),
            pl.BlockSpec((CONV_WIDTH, vw), lambda i, g, t: (0, cv0 + g)),
            pl.BlockSpec((1, DN_V_DIM), lambda i, g, t: (0, 0)),
        ],
        out_specs=pl.BlockSpec((1, DN_TS, vw), lambda i, g, t: (i, t, g)),
        out_shape=jax.ShapeDtypeStruct((b, s, DN_V_W), BF16),
        scratch_shapes=[
            pltpu.VMEM((DN_HG * V_PER_K, DN_K_DIM, DN_V_DIM), F32),
            pltpu.VMEM((DN_HG, DN_TS, VW), F32),
            pltpu.VMEM((DN_HG, DN_TS, VW), BF16),
            pltpu.VMEM((DN_HG, DN_TS, VW), BF16),
            pltpu.VMEM((DN_HG, V_PER_K * DN_TS, DN_K_DIM), BF16),
            pltpu.VMEM((DN_HG, DN_TS, V_PER_K * CHUNK), BF16),
            pltpu.VMEM((CHUNK + DN_TS, DN_HG * RAW_W), BF16),
            pltpu.VMEM((2, DN_A_CHUNKS * DN_HG, CHUNK, 4 * CHUNK + 2 * DN_K_DIM + VW), F32),
        ],
        compiler_params=pltpu.CompilerParams(
            dimension_semantics=("parallel", "parallel", "arbitrary"),
            vmem_limit_bytes=V7X_VMEM_LIMIT),
        name="dnet",
    )(proj3, proj3, proj3, proj3, proj3, proj3, proj3, colp, rowp, conv_w, conv_w, conv_w,
      dn_norm_w)


MERGE_TM = 512
MERGE_TN = 512


def _merge_kernel(oa_ref, od_ref, wa_ref, wd_ref, ga_ref, gd_ref, o_ref):
    ya = jnp.dot(oa_ref[...], wa_ref[...], preferred_element_type=F32)
    yd = jnp.dot(od_ref[...], wd_ref[...], preferred_element_type=F32)
    ga = _sigmoid(ga_ref[...].astype(F32))
    gd = _sigmoid(gd_ref[...].astype(F32))
    o_ref[...] = (ga * ya + gd * yd).astype(o_ref.dtype)


def _merge(oa, od, wa, wd, proj2):
    m = oa.shape[0]
    g0 = P_G // MERGE_TN
    g1 = (P_G + D_MODEL) // MERGE_TN
    return pl.pallas_call(
        _merge_kernel,
        grid=(m // MERGE_TM, D_MODEL // MERGE_TN),
        in_specs=[
            pl.BlockSpec((MERGE_TM, ATT_W), lambda i, j: (i, 0)),
            pl.BlockSpec((MERGE_TM, DN_V_W), lambda i, j: (i, 0)),
            pl.BlockSpec((ATT_W, MERGE_TN), lambda i, j: (0, j)),
            pl.BlockSpec((DN_V_W, MERGE_TN), lambda i, j: (0, j)),
            pl.BlockSpec((MERGE_TM, MERGE_TN), lambda i, j: (i, g0 + j)),
            pl.BlockSpec((MERGE_TM, MERGE_TN), lambda i, j: (i, g1 + j)),
        ],
        out_specs=pl.BlockSpec((MERGE_TM, MERGE_TN), lambda i, j: (i, j)),
        out_shape=jax.ShapeDtypeStruct((m, D_MODEL), BF16),
        compiler_params=pltpu.CompilerParams(
            dimension_semantics=("parallel", "parallel"),
            vmem_limit_bytes=V7X_VMEM_LIMIT),
        name="merge",
    )(oa, od, wa, wd, proj2, proj2)


OUT_TM = 512


def _out_kernel(m_ref, w_ref, x_ref, nw_ref, o_ref):
    y = x_ref[...] + jnp.dot(m_ref[...], w_ref[...], preferred_element_type=F32)
    ms = jnp.mean(y * y, axis=-1, keepdims=True)
    o_ref[...] = (y * lax.rsqrt(ms + EPS)) * nw_ref[...]


def _outproj(merged, w_out, x2, final_norm_w):
    m = x2.shape[0]
    return pl.pallas_call(
        _out_kernel,
        grid=(m // OUT_TM,),
        in_specs=[
            pl.BlockSpec((OUT_TM, D_MODEL), lambda i: (i, 0)),
            pl.BlockSpec((D_MODEL, D_MODEL), lambda i: (0, 0)),
            pl.BlockSpec((OUT_TM, D_MODEL), lambda i: (i, 0)),
            pl.BlockSpec((1, D_MODEL), lambda i: (0, 0)),
        ],
        out_specs=pl.BlockSpec((OUT_TM, D_MODEL), lambda i: (i, 0)),
        out_shape=jax.ShapeDtypeStruct((m, D_MODEL), F32),
        compiler_params=pltpu.CompilerParams(
            dimension_semantics=("parallel",),
            vmem_limit_bytes=V7X_VMEM_LIMIT),
        name="outproj",
    )(merged, w_out, x2, final_norm_w)


def _layer(x, norm_w, w_in, b_qkv, sinks, conv_w, a_log, dt_bias, dn_norm_w,
           w_att_branch, w_dn_branch, w_out, out_norm_w):
    b, s, d = x.shape
    m = b * s
    x2 = x.reshape(m, d)

    w_t = w_in.T
    wk = w_t[OFF_AK:OFF_AV].reshape(N_KV_HEADS, 1, HEAD_DIM, d)
    wv = w_t[OFF_AV:OFF_AZ].reshape(N_KV_HEADS, 1, HEAD_DIM, d)
    w_kv_t = jnp.concatenate([wk, wv], axis=1).reshape(2 * ATT_KV_W, d).astype(BF16)
    w_small_t = jnp.pad(w_t[OFF_DB:OFF_G], ((0, SMALL_W - 2 * DN_V_HEADS), (0, 0))).astype(BF16)
    bk = b_qkv[ATT_Q_W:ATT_Q_W + ATT_KV_W].reshape(N_KV_HEADS, 1, HEAD_DIM)
    bv = b_qkv[ATT_Q_W + ATT_KV_W:].reshape(N_KV_HEADS, 1, HEAD_DIM)
    bias = jnp.concatenate([b_qkv[:ATT_Q_W], jnp.zeros((P_KV - P_AZ,), F32),
                            jnp.concatenate([bk, bv], axis=1).reshape(-1)]).reshape(1, MAIN_W)

    proj, small = _inproj(x2, norm_w.reshape(1, d), w_t, w_kv_t, bias, w_small_t)
    proj3 = proj.reshape(b, s, MAIN_W)

    lane_pad = (DN_V_HEADS, SMALL_W - 2 * DN_V_HEADS)
    alog_row = jnp.pad(a_log, lane_pad).reshape(1, SMALL_W)
    dt_row = jnp.pad(dt_bias, lane_pad).reshape(1, SMALL_W)
    colp, rowp = _gates(small.reshape(b, s, SMALL_W), alog_row, dt_row)

    o_att = _attention(proj3, sinks)
    o_dn = _deltanet(proj3, colp, rowp, conv_w, dn_norm_w.reshape(1, DN_V_DIM))

    merged = _merge(o_att.reshape(m, ATT_W), o_dn.reshape(m, DN_V_W),
                    w_att_branch.astype(BF16), w_dn_branch.astype(BF16), proj)
    y = _outproj(merged, w_out.astype(BF16), x2, out_norm_w.reshape(1, d))
    return y.reshape(b, s, d)


def kernel(x, norm_w, w_in, b_qkv, sinks, conv_w, a_log, dt_bias, dn_norm_w,
           w_att_branch, w_dn_branch, w_out, final_norm_w):
    depth = norm_w.shape[0]
    assert depth == 1, "the final RMSNorm is fused into the single layer's output kernel"
    return _layer(x, norm_w[0], w_in[0], b_qkv[0], sinks[0], conv_w[0], a_log[0], dt_bias[0],
                  dn_norm_w[0], w_att_branch[0], w_dn_branch[0], w_out[0], final_norm_w)
```

```python
import functools

import jax
import jax.numpy as jnp
from jax import lax
from jax.experimental import pallas as pl
from jax.experimental.pallas import tpu as pltpu

F32 = jnp.float32
BF16 = jnp.bfloat16

D_MODEL = 2048
HEAD_DIM = 64
N_Q_HEADS = 32
N_KV_HEADS = 4
GROUP = N_Q_HEADS // N_KV_HEADS
WINDOW = 128
ATT_BLOCK = 128
ATT_Q_W = N_Q_HEADS * HEAD_DIM
ATT_KV_W = N_KV_HEADS * HEAD_DIM
ATT_W = ATT_Q_W

DN_K_HEADS = 16
DN_V_HEADS = 32
DN_K_DIM = 128
DN_V_DIM = 128
DN_QK_W = DN_K_HEADS * DN_K_DIM
DN_V_W = DN_V_HEADS * DN_V_DIM
DN_CONV_CH = 2 * DN_QK_W + DN_V_W
CONV_WIDTH = 4
CHUNK = 64
EPS = 1e-6

OFF_AQ = 0
OFF_AK = OFF_AQ + ATT_Q_W
OFF_AV = OFF_AK + ATT_KV_W
OFF_AZ = OFF_AV + ATT_KV_W
OFF_DQKV = OFF_AZ + ATT_W
OFF_DZ = OFF_DQKV + DN_CONV_CH
OFF_DB = OFF_DZ + DN_V_W
OFF_DA = OFF_DB + DN_V_HEADS
OFF_G = OFF_DA + DN_V_HEADS
IN_W = OFF_G + 2 * D_MODEL

MAIN_W = IN_W - 2 * DN_V_HEADS
P_AQ = 0
P_AZ = P_AQ + ATT_Q_W
P_DQ = P_AZ + ATT_W
P_DK = P_DQ + DN_QK_W
P_DV = P_DK + DN_QK_W
P_DZ = P_DV + DN_V_W
P_G = P_DZ + DN_V_W
P_KV = P_G + 2 * D_MODEL
SMALL_W = 128

V7X_VMEM_LIMIT = 56 * 1024 * 1024


def _sigmoid(v):
    return 0.5 * jnp.tanh(0.5 * v) + 0.5


def _silu(v):
    h = 0.5 * v
    return h * jnp.tanh(h) + h


def _mm(a, b):
    return jnp.dot(a.astype(BF16), b.astype(BF16), preferred_element_type=F32)


INPROJ_TM = 2048
INPROJ_TN = 512
NORM_ROWS = 128


NT_DIMS = (((1,), (1,)), ((), ()))


def _inproj_kernel(x_ref, nw_ref, wt_ref, wkv_ref, b_ref, ws_ref, o_ref, os_ref, h_ref):
    j = pl.program_id(1)

    @pl.when(j == 0)
    def _():
        def body(r, carry):
            r0 = pl.multiple_of(r * NORM_ROWS, NORM_ROWS)
            xv = x_ref[pl.ds(r0, NORM_ROWS), :]
            ms = jnp.mean(xv * xv, axis=-1, keepdims=True)
            hv = (xv * lax.rsqrt(ms + EPS)) * nw_ref[...]
            h_ref[pl.ds(r0, NORM_ROWS), :] = hv.astype(BF16)
            return carry
        lax.fori_loop(0, INPROJ_TM // NORM_ROWS, body, 0)
        os_ref[...] = lax.dot_general(h_ref[...], ws_ref[...].astype(BF16), NT_DIMS,
                                      preferred_element_type=F32)

    def project(w_ref):
        acc = lax.dot_general(h_ref[...], w_ref[...].astype(BF16), NT_DIMS,
                              preferred_element_type=F32)
        o_ref[...] = (acc + b_ref[...]).astype(o_ref.dtype)

    @pl.when(j < DIRECT_BLOCKS)
    def _():
        project(wt_ref)

    @pl.when(j >= DIRECT_BLOCKS)
    def _():
        project(wkv_ref)


DIRECT_BLOCKS = P_KV // INPROJ_TN


def _inproj(x2, norm_w, w_t, w_kv_t, bias, w_small_t):
    m = x2.shape[0]
    grid = (m // INPROJ_TM, MAIN_W // INPROJ_TN)
    n_q, n_mid = ATT_Q_W // INPROJ_TN, P_G // INPROJ_TN

    def direct_rows(i, j):
        jj = jnp.minimum(j, DIRECT_BLOCKS - 1)
        row = jnp.where(jj < n_q, OFF_AQ + INPROJ_TN * jj,
                        jnp.where(jj < n_mid, OFF_AZ + INPROJ_TN * (jj - n_q),
                                  OFF_G + INPROJ_TN * (jj - n_mid)))
        return pl.multiple_of(row, HEAD_DIM), 0

    return pl.pallas_call(
        _inproj_kernel,
        grid=grid,
        in_specs=[
            pl.BlockSpec((INPROJ_TM, D_MODEL), lambda i, j: (i, 0), pipeline_mode=pl.Buffered(1)),
            pl.BlockSpec((1, D_MODEL), lambda i, j: (0, 0)),
            pl.BlockSpec((pl.Element(INPROJ_TN), pl.Element(D_MODEL)), direct_rows),
            pl.BlockSpec((INPROJ_TN, D_MODEL), lambda i, j: (0, 0), pipeline_mode=pl.Buffered(1)),
            pl.BlockSpec((1, INPROJ_TN), lambda i, j: (0, j)),
            pl.BlockSpec((SMALL_W, D_MODEL), lambda i, j: (0, 0), pipeline_mode=pl.Buffered(1)),
        ],
        out_specs=[
            pl.BlockSpec((INPROJ_TM, INPROJ_TN), lambda i, j: (i, j)),
            pl.BlockSpec((INPROJ_TM, SMALL_W), lambda i, j: (i, 0)),
        ],
        out_shape=[
            jax.ShapeDtypeStruct((m, MAIN_W), BF16),
            jax.ShapeDtypeStruct((m, SMALL_W), F32),
        ],
        scratch_shapes=[pltpu.VMEM((INPROJ_TM, D_MODEL), BF16)],
        compiler_params=pltpu.CompilerParams(
            dimension_semantics=("parallel", "arbitrary"),
            vmem_limit_bytes=V7X_VMEM_LIMIT),
        name="inproj",
    )(x2, norm_w, w_t, w_kv_t, bias, w_small_t)


GATE_ROWS = 2 * CHUNK
DN_HG = 4
GATE_GROUPS = DN_K_HEADS // DN_HG
GROUP_VH = DN_V_HEADS // GATE_GROUPS


def _gates_kernel(s_ref, alog_ref, dt_ref, col_ref, row_ref):
    seq = s_ref.shape[1]
    row = lax.broadcasted_iota(jnp.int32, (CHUNK, SMALL_W), 0)
    lane = lax.broadcasted_iota(jnp.int32, (GATE_ROWS, SMALL_W), 1)

    def body(blk, carry):
        r0 = pl.multiple_of(blk * GATE_ROWS, GATE_ROWS)
        xs = s_ref[0, pl.ds(r0, GATE_ROWS), :]
        beta = _sigmoid(xs)
        t = xs + dt_ref[...]
        softplus = jnp.maximum(t, 0.0) + jnp.log(1.0 + jnp.exp(-jnp.abs(t)))
        g = -jnp.exp(alog_ref[...]) * softplus
        halves = []
        for part in range(GATE_ROWS // CHUNK):
            gp = g[part * CHUNK:(part + 1) * CHUNK]
            shift = 1
            while shift < CHUNK:
                gp = gp + jnp.where(row >= shift, pltpu.roll(gp, shift, axis=0), 0.0)
                shift *= 2
            halves.append(gp)
        gcum = jnp.concatenate(halves, axis=0)
        for grp in range(GATE_GROUPS):
            b_g = pltpu.roll(beta, (SMALL_W - grp * GROUP_VH) % SMALL_W, axis=1)
            g_g = pltpu.roll(gcum, SMALL_W - DN_V_HEADS - grp * GROUP_VH + GROUP_VH, axis=1)
            col_ref[0, grp, pl.ds(r0, GATE_ROWS), :] = jnp.where(lane < GROUP_VH, b_g, g_g)
        row_ref[0, blk] = gcum.T[DN_V_HEADS:2 * DN_V_HEADS]
        return carry

    lax.fori_loop(0, seq // GATE_ROWS, body, 0)


def _gates(small3, alog_row, dt_row):
    b, s, _ = small3.shape
    return pl.pallas_call(
        _gates_kernel,
        grid=(b,),
        in_specs=[
            pl.BlockSpec((1, s, SMALL_W), lambda i: (i, 0, 0)),
            pl.BlockSpec((1, SMALL_W), lambda i: (0, 0)),
            pl.BlockSpec((1, SMALL_W), lambda i: (0, 0)),
        ],
        out_specs=[
            pl.BlockSpec((1, GATE_GROUPS, s, SMALL_W), lambda i: (i, 0, 0, 0)),
            pl.BlockSpec((1, s // GATE_ROWS, DN_V_HEADS, GATE_ROWS), lambda i: (i, 0, 0, 0)),
        ],
        out_shape=[
            jax.ShapeDtypeStruct((b, GATE_GROUPS, s, SMALL_W), F32),
            jax.ShapeDtypeStruct((b, s // GATE_ROWS, DN_V_HEADS, GATE_ROWS), F32),
        ],
        compiler_params=pltpu.CompilerParams(dimension_semantics=("parallel",)),
        name="gates",
    )(small3, alog_row, dt_row)


PAIR_W = 2 * HEAD_DIM
KVH_W = GROUP * HEAD_DIM


def _attn_kernel(sinks_ref, q_ref, kvp_ref, kvc_ref, z_ref, o_ref):
    n = pl.program_id(1)
    L = ATT_BLOCK
    lane = lax.broadcasted_iota(jnp.int32, (L, PAIR_W), 1)
    lo = lane < HEAD_DIM

    def prep(kv):
        kv = kv.astype(F32)
        sw = pltpu.roll(kv, HEAD_DIM, axis=1)
        kk = jnp.where(lo, kv, sw).astype(BF16)
        va = jnp.where(lo, sw, 0.0).astype(BF16)
        vb = jnp.where(lo, 0.0, kv).astype(BF16)
        return kk, va, vb

    qi = lax.broadcasted_iota(jnp.int32, (L, 2 * L), 0)
    ki = lax.broadcasted_iota(jnp.int32, (L, 2 * L), 1)
    rel = qi + L - ki
    valid = (rel >= 0) & (rel < WINDOW) & ((ki >= L) | (n > 0))

    def kv_head(h):
        kcols = slice(h * PAIR_W, (h + 1) * PAIR_W)
        kk_p, va_p, vb_p = prep(kvp_ref[0, :, kcols])
        kk_c, va_c, vb_c = prep(kvc_ref[0, :, kcols])
        kk = jnp.concatenate([kk_p, kk_c], axis=0)
        va = jnp.concatenate([va_p, va_c], axis=0)
        vb = jnp.concatenate([vb_p, vb_c], axis=0)
        pair_cols = [slice(h * KVH_W + pr * PAIR_W, h * KVH_W + (pr + 1) * PAIR_W)
                     for pr in range(GROUP // 2)]
        scores = []
        for cols in pair_cols:
            q2 = q_ref[0, :, cols].astype(F32) * (HEAD_DIM ** -0.5)
            for qm in (jnp.where(lo, q2, 0.0), jnp.where(lo, 0.0, q2)):
                scores.append(lax.dot_general(qm.astype(BF16), kk, (((1,), (1,)), ((), ())),
                                              preferred_element_type=F32))
        probs, inv = [], []
        for i, s in enumerate(scores):
            sink = sinks_ref[h * GROUP + i]
            s = jnp.where(valid, s, -jnp.inf)
            m = jnp.maximum(jnp.max(s, axis=-1, keepdims=True), sink)
            p = jnp.exp(s - m)
            inv.append(1.0 / (jnp.sum(p, axis=-1, keepdims=True) + jnp.exp(sink - m)))
            probs.append(p.astype(BF16))
        for pr, cols in enumerate(pair_cols):
            o2 = (jnp.dot(probs[2 * pr], va, preferred_element_type=F32)
                  + jnp.dot(probs[2 * pr + 1], vb, preferred_element_type=F32))
            o2 = o2 * jnp.where(lo, inv[2 * pr], inv[2 * pr + 1])
            zz = z_ref[0, :, cols].astype(F32)
            o_ref[0, :, cols] = (o2 * _silu(zz)).astype(o_ref.dtype)

    for h in range(N_KV_HEADS):
        kv_head(h)


def _attention(proj3, sinks):
    b, s, _ = proj3.shape
    nb = s // ATT_BLOCK
    kvw = 2 * ATT_KV_W
    kv0 = P_KV // kvw
    z0 = P_AZ // ATT_W
    return pl.pallas_call(
        _attn_kernel,
        grid=(b, nb),
        in_specs=[
            pl.BlockSpec(memory_space=pltpu.SMEM),
            pl.BlockSpec((1, ATT_BLOCK, ATT_Q_W), lambda i, n: (i, n, 0)),
            pl.BlockSpec((1, ATT_BLOCK, kvw), lambda i, n: (i, jnp.maximum(n - 1, 0), kv0)),
            pl.BlockSpec((1, ATT_BLOCK, kvw), lambda i, n: (i, n, kv0)),
            pl.BlockSpec((1, ATT_BLOCK, ATT_W), lambda i, n: (i, n, z0)),
        ],
        out_specs=pl.BlockSpec((1, ATT_BLOCK, ATT_W), lambda i, n: (i, n, 0)),
        out_shape=jax.ShapeDtypeStruct((b, s, ATT_W), BF16),
        compiler_params=pltpu.CompilerParams(
            dimension_semantics=("parallel", "parallel")),
        name="attn",
    )(sinks, proj3, proj3, proj3, proj3)


V_PER_K = DN_V_HEADS // DN_K_HEADS
assert V_PER_K == 2 and V_PER_K * CHUNK == 128, "two value heads are packed side by side in 128 lanes"
VW = V_PER_K * DN_V_DIM
DN_TS = 1024
DN_A_CHUNKS = 2
assert DN_A_CHUNKS % (GATE_ROWS // CHUNK) == 0
A1_LEVELS = 2
RAW_W = 2 * DN_K_DIM + VW
HALO_BLK = 16


def _dnet_kernel(q_ref, k_ref, v_ref, z_ref, qh_ref, kh_ref, vh_ref, col_ref, row_ref,
                 cwq_ref, cwk_ref, cwv_ref, nw_ref, o_ref,
                 st_ref, u_ref, w_ref, qe_ref, kd_ref, qkd_ref, raw_ref, mid_ref):
    t = pl.program_id(2)
    C = CHUNK
    ri = lax.broadcasted_iota(jnp.int32, (C, 2 * C), 0)
    li = lax.broadcasted_iota(jnp.int32, (C, 2 * C), 1)
    lo = li < C
    ci = jnp.where(lo, li, li - C)
    tril = ri >= ci
    strict = ri > ci
    eye = (ri == ci).astype(F32)
    top = lax.broadcasted_iota(jnp.int32, (2 * C, 2 * C), 0) < C
    left = lax.broadcasted_iota(jnp.int32, (2 * C, 2 * C), 1) < C
    diag_blk = top == left

    @pl.when(t == 0)
    def _():
        st_ref[...] = jnp.zeros_like(st_ref)

    def head_cols(ref, hh):
        per_head = ref.shape[2] // DN_HG
        return slice(hh * per_head, (hh + 1) * per_head)

    raw_off = (0, DN_K_DIM, 2 * DN_K_DIM)
    raw_ref[0:C - HALO_BLK, :] = jnp.zeros((C - HALO_BLK, raw_ref.shape[1]), BF16)
    for ref, halo_ref, off in zip((q_ref, k_ref, v_ref), (qh_ref, kh_ref, vh_ref), raw_off):
        tail = halo_ref[0]
        tail = jnp.where(t > 0, tail, jnp.zeros_like(tail))
        for hh in range(DN_HG):
            hc = head_cols(ref, hh)
            raw_ref[C - HALO_BLK:C, hh * RAW_W + off:hh * RAW_W + off + hc.stop - hc.start] = tail[:, hc]

    def stage_rows(r, carry):
        r0 = pl.multiple_of(r * C, C)
        for ref, off in zip((q_ref, k_ref, v_ref), raw_off):
            for hh in range(DN_HG):
                hc = head_cols(ref, hh)
                raw_ref[pl.ds(r0 + C, C), hh * RAW_W + off:hh * RAW_W + off + hc.stop - hc.start] = (
                    ref[0, pl.ds(r0, C), hc])
        return carry

    lax.fori_loop(0, DN_TS // C, stage_rows, 0)

    sr = lax.broadcasted_iota(jnp.int32, ((CONV_WIDTH - 1) * C, 2 * C), 0)
    sc = lax.broadcasted_iota(jnp.int32, ((CONV_WIDTH - 1) * C, 2 * C), 1)
    shift_mat = (sc == C + jnp.bitwise_and(sr, C - 1)
                 - (jnp.right_shift(sr, C.bit_length() - 1) + 1)).astype(BF16)

    def conv_silu(hh, r0):
        rcols = slice(hh * RAW_W, (hh + 1) * RAW_W)
        w = jnp.concatenate([cwq_ref[:, head_cols(q_ref, hh)], cwk_ref[:, head_cols(k_ref, hh)],
                             cwv_ref[:, head_cols(v_ref, hh)]], axis=1)
        window = raw_ref[pl.ds(r0, 2 * C), rcols]
        taps = jnp.dot(shift_mat, window, preferred_element_type=F32)
        y = window[C:].astype(F32) * w[CONV_WIDTH - 1:CONV_WIDTH, :]
        for s in range(1, CONV_WIDTH):
            y = y + taps[(s - 1) * C:s * C] * w[CONV_WIDTH - 1 - s:CONV_WIDTH - s, :]
        return _silu(y)

    def l2n(v):
        return v * lax.rsqrt(jnp.sum(v * v, axis=-1, keepdims=True) + EPS)

    def bcast(col):
        return jnp.broadcast_to(col, (C, 2 * C))

    def block_diag(m):
        return jnp.where(diag_blk, jnp.concatenate([m, m], axis=0), 0.0).astype(BF16)

    vslices = [slice(j * DN_V_DIM, (j + 1) * DN_V_DIM) for j in range(V_PER_K)]
    lane_sel = [lo, jnp.logical_not(lo)]


    per_tile = GATE_ROWS // C
    unit_heads = [hh for _ in range(DN_A_CHUNKS) for hh in range(DN_HG)]
    unit_halves = [cc % per_tile for cc in range(DN_A_CHUNKS) for _ in range(DN_HG)]
    unit_tiles = [cc // per_tile for cc in range(DN_A_CHUNKS) for _ in range(DN_HG)]
    n_units = len(unit_heads)
    mid_cols, offset = {}, 0
    for name, width in (("p", 2 * C), ("x", 2 * C), ("q", DN_K_DIM), ("k", DN_K_DIM), ("v", VW)):
        mid_cols[name] = slice(offset, offset + width)
        offset += width

    def unit_rows(it):
        return [pl.multiple_of((it * DN_A_CHUNKS + cc) * C, C)
                for cc in range(DN_A_CHUNKS) for _ in range(DN_HG)]

    def gate_terms(it, r0s):
        g_rows = [row_ref[0, it * (DN_A_CHUNKS // per_tile) + r] for r in range(DN_A_CHUNKS // per_tile)]
        beta, gc, gr, g_last = [], [], [], []
        for i in range(n_units):
            colp = col_ref[0, 0, pl.ds(r0s[i], C), :]
            vh = [unit_heads[i] * V_PER_K + j for j in range(V_PER_K)]
            beta.append([bcast(colp[:, h:h + 1]) for h in vh])
            gc.append([bcast(colp[:, GROUP_VH + h:GROUP_VH + h + 1]) for h in vh])
            tok = slice(unit_halves[i] * C, (unit_halves[i] + 1) * C)
            gr.append(jnp.concatenate([g_rows[unit_tiles[i]][h:h + 1, tok] for h in vh], axis=1))
            g_last.append([gr[i][:, (j + 1) * C - 1:(j + 1) * C] for j in range(V_PER_K)])
        return beta, gc, gr, g_last

    def neumann_level(p, x):
        for i in range(n_units):
            rhs = jnp.concatenate([block_diag(p[i]), block_diag(x[i])], axis=1)
            pp_px = jnp.dot(p[i].astype(BF16), rhs, preferred_element_type=F32)
            x[i] = x[i] + pp_px[:, 2 * C:]
            p[i] = pp_px[:, :2 * C]

    n_levels = C.bit_length() - 2

    def stage_a1(it):
        r0s = unit_rows(it)
        q, k, v = [], [], []
        for i in range(n_units):
            y = conv_silu(unit_heads[i], r0s[i])
            q.append(l2n(y[:, raw_off[0]:raw_off[1]]) * (DN_K_DIM ** -0.5))
            k.append(l2n(y[:, raw_off[1]:raw_off[2]]))
            v.append(y[:, raw_off[2]:])
        yield
        qk_kk = []
        for i in range(n_units):
            kb = k[i].astype(BF16)
            qk_kk.append(lax.dot_general(jnp.concatenate([q[i].astype(BF16), kb], axis=0),
                                         jnp.concatenate([kb, kb], axis=0),
                                         (((1,), (1,)), ((), ())), preferred_element_type=F32))
        yield
        beta, gc, gr, _ = gate_terms(it, r0s)
        x, p = [], []
        for i in range(n_units):
            beta_p = jnp.where(lo, beta[i][0], beta[i][1])
            gc_p = jnp.where(lo, gc[i][0], gc[i][1])
            decay = jnp.exp(jnp.where(tril, gc_p - gr[i], -jnp.inf))
            a = jnp.where(strict, beta_p * qk_kk[i][C:] * decay, 0.0)
            qkd_ref[unit_heads[i], pl.ds(r0s[i], C), :] = (qk_kk[i][:C] * decay).astype(BF16)
            x.append(eye - a)
            p.append(jnp.dot(a.astype(BF16), block_diag(a), preferred_element_type=F32))
        yield
        for _ in range(A1_LEVELS):
            neumann_level(p, x)
            yield
        for i in range(n_units):
            mid_ref[it % 2, i] = jnp.concatenate([p[i], x[i], q[i], k[i], v[i]], axis=1)

    def stage_a2(it):
        r0s = unit_rows(it)
        mid = [mid_ref[it % 2, i] for i in range(n_units)]
        p = [m[:, mid_cols["p"]] for m in mid]
        x = [m[:, mid_cols["x"]] for m in mid]
        for _ in range(n_levels - A1_LEVELS):
            neumann_level(p, x)
            yield
        beta, gc, _, g_last = gate_terms(it, r0s)
        sols = []
        for i in range(n_units):
            q, k, v = (mid[i][:, mid_cols[name]] for name in ("q", "k", "v"))
            eg = [jnp.exp(gc[i][j]) for j in range(V_PER_K)]
            rhs = jnp.concatenate(
                [jnp.concatenate([v[:, vslices[j]] * beta[i][j],
                                  k * (beta[i][j] * eg[j])], axis=1) for j in range(V_PER_K)],
                axis=0).astype(BF16)
            sols.append([jnp.dot(jnp.where(lane_sel[j], x[i], 0.0).astype(BF16), rhs,
                                 preferred_element_type=F32) for j in range(V_PER_K)])
            hh, r0 = unit_heads[i], r0s[i]
            qe_ref[hh, pl.ds(r0, C), :] = jnp.concatenate(
                [q * eg[j] for j in range(V_PER_K)], axis=1).astype(BF16)
            kd_ref[hh, pl.ds(pl.multiple_of(V_PER_K * r0, V_PER_K * C), V_PER_K * C), :] = jnp.concatenate(
                [k * jnp.exp(g_last[i][j] - gc[i][j]) for j in range(V_PER_K)],
                axis=0).astype(BF16)
        yield
        for i in range(n_units):
            hh, r0 = unit_heads[i], r0s[i]
            u_ref[hh, pl.ds(r0, C), :] = jnp.concatenate(
                [s[:, :DN_V_DIM] for s in sols[i]], axis=1)
            w_ref[hh, pl.ds(r0, C), :] = jnp.concatenate(
                [s[:, DN_V_DIM:] for s in sols[i]], axis=1).astype(BF16)

    def stage_b(c):
        r0 = pl.multiple_of(c * C, C)
        zero_blk = jnp.zeros((DN_K_DIM, DN_V_DIM), BF16)
        first = lax.broadcasted_iota(jnp.int32, (C, VW), 1) < DN_V_DIM
        ws_qs, v_new = [], []
        for hh in range(DN_HG):
            s0 = st_ref[hh * V_PER_K].astype(BF16)
            s1 = st_ref[hh * V_PER_K + 1].astype(BF16)
            s_bd = jnp.concatenate([jnp.concatenate([s0, zero_blk], axis=1),
                                    jnp.concatenate([zero_blk, s1], axis=1)], axis=0)
            ws_qs.append(jnp.dot(jnp.concatenate([w_ref[hh, pl.ds(r0, C), :],
                                                  qe_ref[hh, pl.ds(r0, C), :]], axis=0),
                                 s_bd, preferred_element_type=F32))
            v_new.append((u_ref[hh, pl.ds(r0, C), :] - ws_qs[hh][:C]).astype(BF16))
        yield
        v_bd = []
        for hh in range(DN_HG):
            zeros = jnp.zeros_like(v_new[hh])
            v_bd.append(jnp.concatenate([jnp.where(first, v_new[hh], zeros),
                                         jnp.where(first, zeros, v_new[hh])], axis=0))
            kd = kd_ref[hh, pl.ds(pl.multiple_of(V_PER_K * r0, V_PER_K * C), V_PER_K * C), :]
            kv = lax.dot_general(kd, v_bd[hh], (((0,), (0,)), ((), ())),
                                 preferred_element_type=F32)
            for j in range(V_PER_K):
                gl = GROUP_VH + hh * V_PER_K + j
                g_last = col_ref[0, 0, pl.ds(r0 + C - 1, 1), gl:gl + 1]
                st_ref[hh * V_PER_K + j] = (st_ref[hh * V_PER_K + j] * jnp.exp(g_last)
                                            + kv[:, vslices[j]])
        yield
        for hh in range(DN_HG):
            o_both = ws_qs[hh][C:] + jnp.dot(qkd_ref[hh, pl.ds(r0, C), :], v_bd[hh],
                                             preferred_element_type=F32)
            for j in range(V_PER_K):
                o = o_both[:, vslices[j]]
                on = o * lax.rsqrt(jnp.mean(o * o, axis=-1, keepdims=True) + EPS) * nw_ref[...]
                ocols = slice(hh * VW + j * DN_V_DIM, hh * VW + (j + 1) * DN_V_DIM)
                zz = z_ref[0, pl.ds(r0, C), ocols].astype(F32)
                o_ref[0, pl.ds(r0, C), ocols] = (on * _silu(zz)).astype(o_ref.dtype)

    def recurrence(it):
        for cc in range(DN_A_CHUNKS):
            yield from stage_b(it * DN_A_CHUNKS + cc)
            yield

    def emit(*streams):
        live = [iter(g) for g in streams]
        while live:
            live = [g for g in live if next(g, live) is not live]

    def stage_a(it):
        yield from stage_a1(it)
        yield
        yield from stage_a2(it)

    n_iter = DN_TS // (C * DN_A_CHUNKS)
    emit(stage_a(0))

    def body(it, carry):
        emit(stage_a(it + 1), recurrence(it))
        return carry

    lax.fori_loop(0, n_iter - 1, body, 0)
    emit(recurrence(n_iter - 1))


def _deltanet(proj3, colp, rowp, conv_w, dn_norm_w):
    b, s, _ = proj3.shape
    kw = DN_HG * DN_K_DIM
    vw = DN_HG * VW
    q0, k0, v0, z0 = P_DQ // kw, P_DK // kw, P_DV // vw, P_DZ // vw
    ck0, cv0 = DN_QK_W // kw, 2 * DN_QK_W // vw
    hb = DN_TS // HALO_BLK

    def halo(t):
        return jnp.maximum(t * hb - 1, 0)

    return pl.pallas_call(
        _dnet_kernel,
        grid=(b, DN_K_HEADS // DN_HG, s // DN_TS),
        in_specs=[
            pl.BlockSpec((1, DN_TS, kw), lambda i, g, t: (i, t, q0 + g)),
            pl.BlockSpec((1, DN_TS, kw), lambda i, g, t: (i, t, k0 + g)),
            pl.BlockSpec((1, DN_TS, vw), lambda i, g, t: (i, t, v0 + g)),
            pl.BlockSpec((1, DN_TS, vw), lambda i, g, t: (i, t, z0 + g)),
            pl.BlockSpec((1, HALO_BLK, kw), lambda i, g, t: (i, halo(t), q0 + g)),
            pl.BlockSpec((1, HALO_BLK, kw), lambda i, g, t: (i, halo(t), k0 + g)),
            pl.BlockSpec((1, HALO_BLK, vw), lambda i, g, t: (i, halo(t), v0 + g)),
            pl.BlockSpec((1, 1, DN_TS, SMALL_W), lambda i, g, t: (i, g, t, 0)),
            pl.BlockSpec((1, DN_TS // GATE_ROWS, GROUP_VH, GATE_ROWS), lambda i, g, t: (i, t, g, 0)),
            pl.BlockSpec((CONV_WIDTH, kw), lambda i, g, t: (0, g)),
            pl.BlockSpec((CONV_WIDTH, kw), lambda i, g, t: (0, ck0 + g)),
            pl.BlockSpec((CONV_WIDTH, vw), lambda i, g, t: (0, cv0 + g)),
            pl.BlockSpec((1, DN_V_DIM), lambda i, g, t: (0, 0)),
        ],
        out_specs=pl.BlockSpec((1, DN_TS, vw), lambda i, g, t: (i, t, g)),
        out_shape=jax.ShapeDtypeStruct((b, s, DN_V_W), BF16),
        scratch_shapes=[
            pltpu.VMEM((DN_HG * V_PER_K, DN_K_DIM, DN_V_DIM), F32),
            pltpu.VMEM((DN_HG, DN_TS, VW), F32),
            pltpu.VMEM((DN_HG, DN_TS, VW), BF16),
            pltpu.VMEM((DN_HG, DN_TS, VW), BF16),
            pltpu.VMEM((DN_HG, V_PER_K * DN_TS, DN_K_DIM), BF16),
            pltpu.VMEM((DN_HG, DN_TS, V_PER_K * CHUNK), BF16),
            pltpu.VMEM((CHUNK + DN_TS, DN_HG * RAW_W), BF16),
            pltpu.VMEM((2, DN_A_CHUNKS * DN_HG, CHUNK, 4 * CHUNK + 2 * DN_K_DIM + VW), F32),
        ],
        compiler_params=pltpu.CompilerParams(
            dimension_semantics=("parallel", "parallel", "arbitrary"),
            vmem_limit_bytes=V7X_VMEM_LIMIT),
        name="dnet",
    )(proj3, proj3, proj3, proj3, proj3, proj3, proj3, colp, rowp, conv_w, conv_w, conv_w,
      dn_norm_w)


MERGE_TM = 512
MERGE_TN = 1024


def _merge_kernel(oa_ref, od_ref, wa_ref, wd_ref, ga_ref, gd_ref, o_ref):
    ya = jnp.dot(oa_ref[...], wa_ref[...], preferred_element_type=F32)
    yd = jnp.dot(od_ref[...], wd_ref[...], preferred_element_type=F32)
    ga = _sigmoid(ga_ref[...].astype(F32))
    gd = _sigmoid(gd_ref[...].astype(F32))
    o_ref[...] = (ga * ya + gd * yd).astype(o_ref.dtype)


def _merge(oa, od, wa, wd, proj2):
    m = oa.shape[0]
    g0 = P_G // MERGE_TN
    g1 = (P_G + D_MODEL) // MERGE_TN
    return pl.pallas_call(
        _merge_kernel,
        grid=(m // MERGE_TM, D_MODEL // MERGE_TN),
        in_specs=[
            pl.BlockSpec((MERGE_TM, ATT_W), lambda i, j: (i, 0)),
            pl.BlockSpec((MERGE_TM, DN_V_W), lambda i, j: (i, 0)),
            pl.BlockSpec((ATT_W, MERGE_TN), lambda i, j: (0, j)),
            pl.BlockSpec((DN_V_W, MERGE_TN), lambda i, j: (0, j)),
            pl.BlockSpec((MERGE_TM, MERGE_TN), lambda i, j: (i, g0 + j)),
            pl.BlockSpec((MERGE_TM, MERGE_TN), lambda i, j: (i, g1 + j)),
        ],
        out_specs=pl.BlockSpec((MERGE_TM, MERGE_TN), lambda i, j: (i, j)),
        out_shape=jax.ShapeDtypeStruct((m, D_MODEL), BF16),
        compiler_params=pltpu.CompilerParams(
            dimension_semantics=("parallel", "parallel"),
            vmem_limit_bytes=V7X_VMEM_LIMIT),
        name="merge",
    )(oa, od, wa, wd, proj2, proj2)


OUT_TM = 512


def _out_kernel(m_ref, w_ref, x_ref, nw_ref, o_ref):
    y = x_ref[...] + jnp.dot(m_ref[...], w_ref[...], preferred_element_type=F32)
    ms = jnp.mean(y * y, axis=-1, keepdims=True)
    o_ref[...] = (y * lax.rsqrt(ms + EPS)) * nw_ref[...]


def _outproj(merged, w_out, x2, final_norm_w):
    m = x2.shape[0]
    return pl.pallas_call(
        _out_kernel,
        grid=(m // OUT_TM,),
        in_specs=[
            pl.BlockSpec((OUT_TM, D_MODEL), lambda i: (i, 0)),
            pl.BlockSpec((D_MODEL, D_MODEL), lambda i: (0, 0)),
            pl.BlockSpec((OUT_TM, D_MODEL), lambda i: (i, 0)),
            pl.BlockSpec((1, D_MODEL), lambda i: (0, 0)),
        ],
        out_specs=pl.BlockSpec((OUT_TM, D_MODEL), lambda i: (i, 0)),
        out_shape=jax.ShapeDtypeStruct((m, D_MODEL), F32),
        compiler_params=pltpu.CompilerParams(
            dimension_semantics=("parallel",),
            vmem_limit_bytes=V7X_VMEM_LIMIT),
        name="outproj",
    )(merged, w_out, x2, final_norm_w)


def _layer(x, norm_w, w_in, b_qkv, sinks, conv_w, a_log, dt_bias, dn_norm_w,
           w_att_branch, w_dn_branch, w_out, out_norm_w):
    b, s, d = x.shape
    m = b * s
    x2 = x.reshape(m, d)

    w_t = w_in.T
    wk = w_t[OFF_AK:OFF_AV].reshape(N_KV_HEADS, 1, HEAD_DIM, d)
    wv = w_t[OFF_AV:OFF_AZ].reshape(N_KV_HEADS, 1, HEAD_DIM, d)
    w_kv_t = jnp.concatenate([wk, wv], axis=1).reshape(2 * ATT_KV_W, d)
    w_small_t = jnp.pad(w_t[OFF_DB:OFF_G], ((0, SMALL_W - 2 * DN_V_HEADS), (0, 0)))
    bk = b_qkv[ATT_Q_W:ATT_Q_W + ATT_KV_W].reshape(N_KV_HEADS, 1, HEAD_DIM)
    bv = b_qkv[ATT_Q_W + ATT_KV_W:].reshape(N_KV_HEADS, 1, HEAD_DIM)
    bias = jnp.concatenate([b_qkv[:ATT_Q_W], jnp.zeros((P_KV - P_AZ,), F32),
                            jnp.concatenate([bk, bv], axis=1).reshape(-1)]).reshape(1, MAIN_W)

    proj, small = _inproj(x2, norm_w.reshape(1, d), w_t, w_kv_t, bias, w_small_t)
    proj3 = proj.reshape(b, s, MAIN_W)

    lane_pad = (DN_V_HEADS, SMALL_W - 2 * DN_V_HEADS)
    alog_row = jnp.pad(a_log, lane_pad).reshape(1, SMALL_W)
    dt_row = jnp.pad(dt_bias, lane_pad).reshape(1, SMALL_W)
    colp, rowp = _gates(small.reshape(b, s, SMALL_W), alog_row, dt_row)

    o_att = _attention(proj3, sinks)
    o_dn = _deltanet(proj3, colp, rowp, conv_w, dn_norm_w.reshape(1, DN_V_DIM))

    merged = _merge(o_att.reshape(m, ATT_W), o_dn.reshape(m, DN_V_W),
                    w_att_branch.astype(BF16), w_dn_branch.astype(BF16), proj)
    y = _outproj(merged, w_out.astype(BF16), x2, out_norm_w.reshape(1, d))
    return y.reshape(b, s, d)


def kernel(x, norm_w, w_in, b_qkv, sinks, conv_w, a_log, dt_bias, dn_norm_w,
           w_att_branch, w_dn_branch, w_out, final_norm_w):
    depth = norm_w.shape[0]
    assert depth == 1, "the final RMSNorm is fused into the single layer's output kernel"
    return _layer(x, norm_w[0], w_in[0], b_qkv[0], sinks[0], conv_w[0], a_log[0], dt_bias[0],
                  dn_norm_w[0], w_att_branch[0], w_dn_branch[0], w_out[0], final_norm_w)
```

```python
import functools

import jax
import jax.numpy as jnp
from jax import lax
from jax.experimental import pallas as pl
from jax.experimental.pallas import tpu as pltpu

F32 = jnp.float32
BF16 = jnp.bfloat16

D_MODEL = 2048
HEAD_DIM = 64
N_Q_HEADS = 32
N_KV_HEADS = 4
GROUP = N_Q_HEADS // N_KV_HEADS
WINDOW = 128
ATT_BLOCK = 128
ATT_Q_W = N_Q_HEADS * HEAD_DIM
ATT_KV_W = N_KV_HEADS * HEAD_DIM
ATT_W = ATT_Q_W

DN_K_HEADS = 16
DN_V_HEADS = 32
DN_K_DIM = 128
DN_V_DIM = 128
DN_QK_W = DN_K_HEADS * DN_K_DIM
DN_V_W = DN_V_HEADS * DN_V_DIM
DN_CONV_CH = 2 * DN_QK_W + DN_V_W
CONV_WIDTH = 4
CHUNK = 64
EPS = 1e-6

OFF_AQ = 0
OFF_AK = OFF_AQ + ATT_Q_W
OFF_AV = OFF_AK + ATT_KV_W
OFF_AZ = OFF_AV + ATT_KV_W
OFF_DQKV = OFF_AZ + ATT_W
OFF_DZ = OFF_DQKV + DN_CONV_CH
OFF_DB = OFF_DZ + DN_V_W
OFF_DA = OFF_DB + DN_V_HEADS
OFF_G = OFF_DA + DN_V_HEADS
IN_W = OFF_G + 2 * D_MODEL

MAIN_W = IN_W - 2 * DN_V_HEADS
P_AQ = 0
P_AZ = P_AQ + ATT_Q_W
P_DQ = P_AZ + ATT_W
P_DK = P_DQ + DN_QK_W
P_DV = P_DK + DN_QK_W
P_DZ = P_DV + DN_V_W
P_G = P_DZ + DN_V_W
P_KV = P_G + 2 * D_MODEL
SMALL_W = 128

V7X_VMEM_LIMIT = 56 * 1024 * 1024


def _sigmoid(v):
    return 0.5 * jnp.tanh(0.5 * v) + 0.5


def _silu(v):
    h = 0.5 * v
    return h * jnp.tanh(h) + h


def _mm(a, b):
    return jnp.dot(a.astype(BF16), b.astype(BF16), preferred_element_type=F32)


INPROJ_TM = 2048
INPROJ_TN = 512
NORM_ROWS = 128


NT_DIMS = (((1,), (1,)), ((), ()))


def _inproj_kernel(x_ref, nw_ref, wt_ref, wkv_ref, b_ref, ws_ref, o_ref, os_ref, h_ref):
    j = pl.program_id(1)

    @pl.when(j == 0)
    def _():
        def body(r, carry):
            r0 = pl.multiple_of(r * NORM_ROWS, NORM_ROWS)
            xv = x_ref[pl.ds(r0, NORM_ROWS), :]
            ms = jnp.mean(xv * xv, axis=-1, keepdims=True)
            hv = (xv * lax.rsqrt(ms + EPS)) * nw_ref[...]
            h_ref[pl.ds(r0, NORM_ROWS), :] = hv.astype(BF16)
            return carry
        lax.fori_loop(0, INPROJ_TM // NORM_ROWS, body, 0)
        os_ref[...] = lax.dot_general(h_ref[...], ws_ref[...].astype(BF16), NT_DIMS,
                                      preferred_element_type=F32)

    def project(w_ref):
        acc = lax.dot_general(h_ref[...], w_ref[...].astype(BF16), NT_DIMS,
                              preferred_element_type=F32)
        o_ref[...] = (acc + b_ref[...]).astype(o_ref.dtype)

    @pl.when(j < DIRECT_BLOCKS)
    def _():
        project(wt_ref)

    @pl.when(j >= DIRECT_BLOCKS)
    def _():
        project(wkv_ref)


DIRECT_BLOCKS = P_KV // INPROJ_TN


def _inproj(x2, norm_w, w_t, w_kv_t, bias, w_small_t):
    m = x2.shape[0]
    grid = (m // INPROJ_TM, MAIN_W // INPROJ_TN)
    n_q, n_mid = ATT_Q_W // INPROJ_TN, P_G // INPROJ_TN

    def direct_rows(i, j):
        jj = jnp.minimum(j, DIRECT_BLOCKS - 1)
        row = jnp.where(jj < n_q, OFF_AQ + INPROJ_TN * jj,
                        jnp.where(jj < n_mid, OFF_AZ + INPROJ_TN * (jj - n_q),
                                  OFF_G + INPROJ_TN * (jj - n_mid)))
        return pl.multiple_of(row, HEAD_DIM), 0

    return pl.pallas_call(
        _inproj_kernel,
        grid=grid,
        in_specs=[
            pl.BlockSpec((INPROJ_TM, D_MODEL), lambda i, j: (i, 0), pipeline_mode=pl.Buffered(1)),
            pl.BlockSpec((1, D_MODEL), lambda i, j: (0, 0)),
            pl.BlockSpec((pl.Element(INPROJ_TN), pl.Element(D_MODEL)), direct_rows),
            pl.BlockSpec((INPROJ_TN, D_MODEL), lambda i, j: (0, 0), pipeline_mode=pl.Buffered(1)),
            pl.BlockSpec((1, INPROJ_TN), lambda i, j: (0, j)),
            pl.BlockSpec((SMALL_W, D_MODEL), lambda i, j: (0, 0), pipeline_mode=pl.Buffered(1)),
        ],
        out_specs=[
            pl.BlockSpec((INPROJ_TM, INPROJ_TN), lambda i, j: (i, j)),
            pl.BlockSpec((INPROJ_TM, SMALL_W), lambda i, j: (i, 0)),
        ],
        out_shape=[
            jax.ShapeDtypeStruct((m, MAIN_W), BF16),
            jax.ShapeDtypeStruct((m, SMALL_W), F32),
        ],
        scratch_shapes=[pltpu.VMEM((INPROJ_TM, D_MODEL), BF16)],
        compiler_params=pltpu.CompilerParams(
            dimension_semantics=("parallel", "arbitrary"),
            vmem_limit_bytes=V7X_VMEM_LIMIT),
        name="inproj",
    )(x2, norm_w, w_t, w_kv_t, bias, w_small_t)


GATE_ROWS = 2 * CHUNK
DN_HG = 4
GATE_GROUPS = DN_K_HEADS // DN_HG
GROUP_VH = DN_V_HEADS // GATE_GROUPS


def _gates_kernel(s_ref, alog_ref, dt_ref, col_ref, row_ref):
    seq = s_ref.shape[1]
    row = lax.broadcasted_iota(jnp.int32, (CHUNK, SMALL_W), 0)
    lane = lax.broadcasted_iota(jnp.int32, (GATE_ROWS, SMALL_W), 1)

    def body(blk, carry):
        r0 = pl.multiple_of(blk * GATE_ROWS, GATE_ROWS)
        xs = s_ref[0, pl.ds(r0, GATE_ROWS), :]
        beta = _sigmoid(xs)
        t = xs + dt_ref[...]
        softplus = jnp.maximum(t, 0.0) + jnp.log(1.0 + jnp.exp(-jnp.abs(t)))
        g = -jnp.exp(alog_ref[...]) * softplus
        halves = []
        for part in range(GATE_ROWS // CHUNK):
            gp = g[part * CHUNK:(part + 1) * CHUNK]
            shift = 1
            while shift < CHUNK:
                gp = gp + jnp.where(row >= shift, pltpu.roll(gp, shift, axis=0), 0.0)
                shift *= 2
            halves.append(gp)
        gcum = jnp.concatenate(halves, axis=0)
        for grp in range(GATE_GROUPS):
            b_g = pltpu.roll(beta, (SMALL_W - grp * GROUP_VH) % SMALL_W, axis=1)
            g_g = pltpu.roll(gcum, SMALL_W - DN_V_HEADS - grp * GROUP_VH + GROUP_VH, axis=1)
            col_ref[0, grp, pl.ds(r0, GATE_ROWS), :] = jnp.where(lane < GROUP_VH, b_g, g_g)
        row_ref[0, blk] = gcum.T[DN_V_HEADS:2 * DN_V_HEADS]
        return carry

    lax.fori_loop(0, seq // GATE_ROWS, body, 0)


def _gates(small3, alog_row, dt_row):
    b, s, _ = small3.shape
    return pl.pallas_call(
        _gates_kernel,
        grid=(b,),
        in_specs=[
            pl.BlockSpec((1, s, SMALL_W), lambda i: (i, 0, 0)),
            pl.BlockSpec((1, SMALL_W), lambda i: (0, 0)),
            pl.BlockSpec((1, SMALL_W), lambda i: (0, 0)),
        ],
        out_specs=[
            pl.BlockSpec((1, GATE_GROUPS, s, SMALL_W), lambda i: (i, 0, 0, 0)),
            pl.BlockSpec((1, s // GATE_ROWS, DN_V_HEADS, GATE_ROWS), lambda i: (i, 0, 0, 0)),
        ],
        out_shape=[
            jax.ShapeDtypeStruct((b, GATE_GROUPS, s, SMALL_W), F32),
            jax.ShapeDtypeStruct((b, s // GATE_ROWS, DN_V_HEADS, GATE_ROWS), F32),
        ],
        compiler_params=pltpu.CompilerParams(dimension_semantics=("parallel",)),
        name="gates",
    )(small3, alog_row, dt_row)


PAIR_W = 2 * HEAD_DIM
KVH_W = GROUP * HEAD_DIM


def _attn_kernel(sinks_ref, q_ref, kvp_ref, kvc_ref, z_ref, o_ref):
    n = pl.program_id(1)
    L = ATT_BLOCK
    lane = lax.broadcasted_iota(jnp.int32, (L, PAIR_W), 1)
    lo = lane < HEAD_DIM

    def prep(kv):
        kv = kv.astype(F32)
        sw = pltpu.roll(kv, HEAD_DIM, axis=1)
        kk = jnp.where(lo, kv, sw).astype(BF16)
        va = jnp.where(lo, sw, 0.0).astype(BF16)
        vb = jnp.where(lo, 0.0, kv).astype(BF16)
        return kk, va, vb

    qi = lax.broadcasted_iota(jnp.int32, (L, 2 * L), 0)
    ki = lax.broadcasted_iota(jnp.int32, (L, 2 * L), 1)
    rel = qi + L - ki
    valid = (rel >= 0) & (rel < WINDOW) & ((ki >= L) | (n > 0))

    def kv_head(h):
        kcols = slice(h * PAIR_W, (h + 1) * PAIR_W)
        kk_p, va_p, vb_p = prep(kvp_ref[0, :, kcols])
        kk_c, va_c, vb_c = prep(kvc_ref[0, :, kcols])
        kk = jnp.concatenate([kk_p, kk_c], axis=0)
        va = jnp.concatenate([va_p, va_c], axis=0)
        vb = jnp.concatenate([vb_p, vb_c], axis=0)
        pair_cols = [slice(h * KVH_W + pr * PAIR_W, h * KVH_W + (pr + 1) * PAIR_W)
                     for pr in range(GROUP // 2)]
        scores = []
        for cols in pair_cols:
            q2 = q_ref[0, :, cols].astype(F32) * (HEAD_DIM ** -0.5)
            for qm in (jnp.where(lo, q2, 0.0), jnp.where(lo, 0.0, q2)):
                scores.append(lax.dot_general(qm.astype(BF16), kk, (((1,), (1,)), ((), ())),
                                              preferred_element_type=F32))
        probs, inv = [], []
        for i, s in enumerate(scores):
            sink = sinks_ref[h * GROUP + i]
            s = jnp.where(valid, s, -jnp.inf)
            m = jnp.maximum(jnp.max(s, axis=-1, keepdims=True), sink)
            p = jnp.exp(s - m)
            inv.append(1.0 / (jnp.sum(p, axis=-1, keepdims=True) + jnp.exp(sink - m)))
            probs.append(p.astype(BF16))
        for pr, cols in enumerate(pair_cols):
            o2 = (jnp.dot(probs[2 * pr], va, preferred_element_type=F32)
                  + jnp.dot(probs[2 * pr + 1], vb, preferred_element_type=F32))
            o2 = o2 * jnp.where(lo, inv[2 * pr], inv[2 * pr + 1])
            zz = z_ref[0, :, cols].astype(F32)
            o_ref[0, :, cols] = (o2 * _silu(zz)).astype(o_ref.dtype)

    for h in range(N_KV_HEADS):
        kv_head(h)


def _attention(proj3, sinks):
    b, s, _ = proj3.shape
    nb = s // ATT_BLOCK
    kvw = 2 * ATT_KV_W
    kv0 = P_KV // kvw
    z0 = P_AZ // ATT_W
    return pl.pallas_call(
        _attn_kernel,
        grid=(b, nb),
        in_specs=[
            pl.BlockSpec(memory_space=pltpu.SMEM),
            pl.BlockSpec((1, ATT_BLOCK, ATT_Q_W), lambda i, n: (i, n, 0)),
            pl.BlockSpec((1, ATT_BLOCK, kvw), lambda i, n: (i, jnp.maximum(n - 1, 0), kv0)),
            pl.BlockSpec((1, ATT_BLOCK, kvw), lambda i, n: (i, n, kv0)),
            pl.BlockSpec((1, ATT_BLOCK, ATT_W), lambda i, n: (i, n, z0)),
        ],
        out_specs=pl.BlockSpec((1, ATT_BLOCK, ATT_W), lambda i, n: (i, n, 0)),
        out_shape=jax.ShapeDtypeStruct((b, s, ATT_W), BF16),
        compiler_params=pltpu.CompilerParams(
            dimension_semantics=("parallel", "parallel")),
        name="attn",
    )(sinks, proj3, proj3, proj3, proj3)


V_PER_K = DN_V_HEADS // DN_K_HEADS
assert V_PER_K == 2 and V_PER_K * CHUNK == 128, "two value heads are packed side by side in 128 lanes"
VW = V_PER_K * DN_V_DIM
DN_TS = 2048
DN_A_CHUNKS = 2
assert DN_A_CHUNKS % (GATE_ROWS // CHUNK) == 0
PAIR_ROWS = DN_A_CHUNKS * CHUNK
RING = 2
A1_LEVELS = 2
RAW_W = 2 * DN_K_DIM + VW
HALO_BLK = 16


def _dnet_kernel(q_ref, k_ref, v_ref, z_ref, qh_ref, kh_ref, vh_ref, col_ref, row_ref,
                 cwq_ref, cwk_ref, cwv_ref, nw_ref, o_ref,
                 st_ref, u_ref, w_ref, qe_ref, kd_ref, qkd_ref, raw_ref, mid_ref):
    t = pl.program_id(2)
    C = CHUNK
    ri = lax.broadcasted_iota(jnp.int32, (C, 2 * C), 0)
    li = lax.broadcasted_iota(jnp.int32, (C, 2 * C), 1)
    lo = li < C
    ci = jnp.where(lo, li, li - C)
    tril = ri >= ci
    strict = ri > ci
    eye = (ri == ci).astype(F32)
    top = lax.broadcasted_iota(jnp.int32, (2 * C, 2 * C), 0) < C
    left = lax.broadcasted_iota(jnp.int32, (2 * C, 2 * C), 1) < C
    diag_blk = top == left

    @pl.when(t == 0)
    def _():
        st_ref[...] = jnp.zeros_like(st_ref)

    def head_cols(ref, hh):
        per_head = ref.shape[2] // DN_HG
        return slice(hh * per_head, (hh + 1) * per_head)

    raw_off = (0, DN_K_DIM, 2 * DN_K_DIM)
    raw_ref[0:C - HALO_BLK, :] = jnp.zeros((C - HALO_BLK, raw_ref.shape[1]), BF16)
    for ref, halo_ref, off in zip((q_ref, k_ref, v_ref), (qh_ref, kh_ref, vh_ref), raw_off):
        tail = halo_ref[0]
        tail = jnp.where(t > 0, tail, jnp.zeros_like(tail))
        for hh in range(DN_HG):
            hc = head_cols(ref, hh)
            raw_ref[C - HALO_BLK:C, hh * RAW_W + off:hh * RAW_W + off + hc.stop - hc.start] = tail[:, hc]

    def stage_rows(r, carry):
        r0 = pl.multiple_of(r * C, C)
        for ref, off in zip((q_ref, k_ref, v_ref), raw_off):
            for hh in range(DN_HG):
                hc = head_cols(ref, hh)
                raw_ref[pl.ds(r0 + C, C), hh * RAW_W + off:hh * RAW_W + off + hc.stop - hc.start] = (
                    ref[0, pl.ds(r0, C), hc])
        return carry

    lax.fori_loop(0, DN_TS // C, stage_rows, 0)

    sr = lax.broadcasted_iota(jnp.int32, ((CONV_WIDTH - 1) * C, 2 * C), 0)
    sc = lax.broadcasted_iota(jnp.int32, ((CONV_WIDTH - 1) * C, 2 * C), 1)
    shift_mat = (sc == C + jnp.bitwise_and(sr, C - 1)
                 - (jnp.right_shift(sr, C.bit_length() - 1) + 1)).astype(BF16)

    def conv_silu(hh, r0):
        rcols = slice(hh * RAW_W, (hh + 1) * RAW_W)
        w = jnp.concatenate([cwq_ref[:, head_cols(q_ref, hh)], cwk_ref[:, head_cols(k_ref, hh)],
                             cwv_ref[:, head_cols(v_ref, hh)]], axis=1)
        window = raw_ref[pl.ds(r0, 2 * C), rcols]
        taps = jnp.dot(shift_mat, window, preferred_element_type=F32)
        y = window[C:].astype(F32) * w[CONV_WIDTH - 1:CONV_WIDTH, :]
        for s in range(1, CONV_WIDTH):
            y = y + taps[(s - 1) * C:s * C] * w[CONV_WIDTH - 1 - s:CONV_WIDTH - s, :]
        return _silu(y)

    def l2n(v):
        return v * lax.rsqrt(jnp.sum(v * v, axis=-1, keepdims=True) + EPS)

    def bcast(col):
        return jnp.broadcast_to(col, (C, 2 * C))

    def block_diag(m):
        return jnp.where(diag_blk, jnp.concatenate([m, m], axis=0), 0.0).astype(BF16)

    vslices = [slice(j * DN_V_DIM, (j + 1) * DN_V_DIM) for j in range(V_PER_K)]
    lane_sel = [lo, jnp.logical_not(lo)]


    per_tile = GATE_ROWS // C
    unit_heads = [hh for _ in range(DN_A_CHUNKS) for hh in range(DN_HG)]
    unit_halves = [cc % per_tile for cc in range(DN_A_CHUNKS) for _ in range(DN_HG)]
    unit_tiles = [cc // per_tile for cc in range(DN_A_CHUNKS) for _ in range(DN_HG)]
    n_units = len(unit_heads)
    mid_cols, offset = {}, 0
    for name, width in (("p", 2 * C), ("x", 2 * C), ("q", DN_K_DIM), ("k", DN_K_DIM), ("v", VW)):
        mid_cols[name] = slice(offset, offset + width)
        offset += width

    def unit_rows(it):
        return [pl.multiple_of((it * DN_A_CHUNKS + cc) * C, C)
                for cc in range(DN_A_CHUNKS) for _ in range(DN_HG)]

    def ring_row(it, cc):
        return pl.multiple_of((it % RING) * PAIR_ROWS + cc * C, C)

    def unit_ring_rows(it):
        return [ring_row(it, cc) for cc in range(DN_A_CHUNKS) for _ in range(DN_HG)]

    def gate_terms(it, r0s):
        g_rows = [row_ref[0, it * (DN_A_CHUNKS // per_tile) + r] for r in range(DN_A_CHUNKS // per_tile)]
        beta, gc, gr, g_last = [], [], [], []
        for i in range(n_units):
            colp = col_ref[0, 0, pl.ds(r0s[i], C), :]
            vh = [unit_heads[i] * V_PER_K + j for j in range(V_PER_K)]
            beta.append([bcast(colp[:, h:h + 1]) for h in vh])
            gc.append([bcast(colp[:, GROUP_VH + h:GROUP_VH + h + 1]) for h in vh])
            tok = slice(unit_halves[i] * C, (unit_halves[i] + 1) * C)
            gr.append(jnp.concatenate([g_rows[unit_tiles[i]][h:h + 1, tok] for h in vh], axis=1))
            g_last.append([gr[i][:, (j + 1) * C - 1:(j + 1) * C] for j in range(V_PER_K)])
        return beta, gc, gr, g_last

    def neumann_level(p, x):
        for i in range(n_units):
            rhs = jnp.concatenate([block_diag(p[i]), block_diag(x[i])], axis=1)
            pp_px = jnp.dot(p[i].astype(BF16), rhs, preferred_element_type=F32)
            x[i] = x[i] + pp_px[:, 2 * C:]
            p[i] = pp_px[:, :2 * C]

    n_levels = C.bit_length() - 2

    def stage_a1(it):
        r0s = unit_rows(it)
        q, k, v = [], [], []
        for i in range(n_units):
            y = conv_silu(unit_heads[i], r0s[i])
            q.append(l2n(y[:, raw_off[0]:raw_off[1]]) * (DN_K_DIM ** -0.5))
            k.append(l2n(y[:, raw_off[1]:raw_off[2]]))
            v.append(y[:, raw_off[2]:])
        yield
        qk_kk = []
        for i in range(n_units):
            kb = k[i].astype(BF16)
            qk_kk.append(lax.dot_general(jnp.concatenate([q[i].astype(BF16), kb], axis=0),
                                         jnp.concatenate([kb, kb], axis=0),
                                         (((1,), (1,)), ((), ())), preferred_element_type=F32))
        yield
        beta, gc, gr, _ = gate_terms(it, r0s)
        ring = unit_ring_rows(it)
        x, p = [], []
        for i in range(n_units):
            beta_p = jnp.where(lo, beta[i][0], beta[i][1])
            gc_p = jnp.where(lo, gc[i][0], gc[i][1])
            decay = jnp.exp(jnp.where(tril, gc_p - gr[i], -jnp.inf))
            a = jnp.where(strict, beta_p * qk_kk[i][C:] * decay, 0.0)
            qkd_ref[unit_heads[i], pl.ds(ring[i], C), :] = (qk_kk[i][:C] * decay).astype(BF16)
            x.append(eye - a)
            p.append(jnp.dot(a.astype(BF16), block_diag(a), preferred_element_type=F32))
        yield
        for _ in range(A1_LEVELS):
            neumann_level(p, x)
            yield
        for i in range(n_units):
            mid_ref[it % 2, i] = jnp.concatenate([p[i], x[i], q[i], k[i], v[i]], axis=1)

    def stage_a2(it):
        r0s = unit_rows(it)
        ring = unit_ring_rows(it)
        mid = [mid_ref[it % 2, i] for i in range(n_units)]
        p = [m[:, mid_cols["p"]] for m in mid]
        x = [m[:, mid_cols["x"]] for m in mid]
        for _ in range(n_levels - A1_LEVELS):
            neumann_level(p, x)
            yield
        beta, gc, _, g_last = gate_terms(it, r0s)
        sols = []
        for i in range(n_units):
            q, k, v = (mid[i][:, mid_cols[name]] for name in ("q", "k", "v"))
            eg = [jnp.exp(gc[i][j]) for j in range(V_PER_K)]
            rhs = jnp.concatenate(
                [jnp.concatenate([v[:, vslices[j]] * beta[i][j],
                                  k * (beta[i][j] * eg[j])], axis=1) for j in range(V_PER_K)],
                axis=0).astype(BF16)
            sols.append([jnp.dot(jnp.where(lane_sel[j], x[i], 0.0).astype(BF16), rhs,
                                 preferred_element_type=F32) for j in range(V_PER_K)])
            hh, r0 = unit_heads[i], ring[i]
            qe_ref[hh, pl.ds(r0, C), :] = jnp.concatenate(
                [q * eg[j] for j in range(V_PER_K)], axis=1).astype(BF16)
            kd_ref[hh, pl.ds(pl.multiple_of(V_PER_K * r0, V_PER_K * C), V_PER_K * C), :] = jnp.concatenate(
                [k * jnp.exp(g_last[i][j] - gc[i][j]) for j in range(V_PER_K)],
                axis=0).astype(BF16)
        yield
        for i in range(n_units):
            hh, r0 = unit_heads[i], ring[i]
            u_ref[hh, pl.ds(r0, C), :] = jnp.concatenate(
                [s[:, :DN_V_DIM] for s in sols[i]], axis=1)
            w_ref[hh, pl.ds(r0, C), :] = jnp.concatenate(
                [s[:, DN_V_DIM:] for s in sols[i]], axis=1).astype(BF16)

    def stage_b(it, cc):
        r0 = pl.multiple_of((it * DN_A_CHUNKS + cc) * C, C)
        rr = ring_row(it, cc)
        zero_blk = jnp.zeros((DN_K_DIM, DN_V_DIM), BF16)
        first = lax.broadcasted_iota(jnp.int32, (C, VW), 1) < DN_V_DIM
        ws_qs, v_new = [], []
        for hh in range(DN_HG):
            s0 = st_ref[hh * V_PER_K].astype(BF16)
            s1 = st_ref[hh * V_PER_K + 1].astype(BF16)
            s_bd = jnp.concatenate([jnp.concatenate([s0, zero_blk], axis=1),
                                    jnp.concatenate([zero_blk, s1], axis=1)], axis=0)
            ws_qs.append(jnp.dot(jnp.concatenate([w_ref[hh, pl.ds(rr, C), :],
                                                  qe_ref[hh, pl.ds(rr, C), :]], axis=0),
                                 s_bd, preferred_element_type=F32))
            v_new.append((u_ref[hh, pl.ds(rr, C), :] - ws_qs[hh][:C]).astype(BF16))
        yield
        v_bd = []
        for hh in range(DN_HG):
            zeros = jnp.zeros_like(v_new[hh])
            v_bd.append(jnp.concatenate([jnp.where(first, v_new[hh], zeros),
                                         jnp.where(first, zeros, v_new[hh])], axis=0))
            kd = kd_ref[hh, pl.ds(pl.multiple_of(V_PER_K * rr, V_PER_K * C), V_PER_K * C), :]
            kv = lax.dot_general(kd, v_bd[hh], (((0,), (0,)), ((), ())),
                                 preferred_element_type=F32)
            for j in range(V_PER_K):
                gl = GROUP_VH + hh * V_PER_K + j
                g_last = col_ref[0, 0, pl.ds(r0 + C - 1, 1), gl:gl + 1]
                st_ref[hh * V_PER_K + j] = (st_ref[hh * V_PER_K + j] * jnp.exp(g_last)
                                            + kv[:, vslices[j]])
        yield
        for hh in range(DN_HG):
            o_both = ws_qs[hh][C:] + jnp.dot(qkd_ref[hh, pl.ds(rr, C), :], v_bd[hh],
                                             preferred_element_type=F32)
            for j in range(V_PER_K):
                o = o_both[:, vslices[j]]
                on = o * lax.rsqrt(jnp.mean(o * o, axis=-1, keepdims=True) + EPS) * nw_ref[...]
                ocols = slice(hh * VW + j * DN_V_DIM, hh * VW + (j + 1) * DN_V_DIM)
                zz = z_ref[0, pl.ds(r0, C), ocols].astype(F32)
                o_ref[0, pl.ds(r0, C), ocols] = (on * _silu(zz)).astype(o_ref.dtype)

    def recurrence(it):
        for cc in range(DN_A_CHUNKS):
            yield from stage_b(it, cc)
            yield

    def emit(main, side=(), main_per_side=1):
        side = iter(side)
        for step, _ in enumerate(main):
            if step % main_per_side == main_per_side - 1:
                next(side, None)
        for _ in side:
            pass

    def stage_a(it):
        yield from stage_a1(it)
        yield
        yield from stage_a2(it)

    n_iter = DN_TS // (C * DN_A_CHUNKS)
    emit(stage_a(0))

    def body(it, carry):
        emit(stage_a(it + 1), recurrence(it))
        return carry

    lax.fori_loop(0, n_iter - 1, body, 0)
    emit(recurrence(n_iter - 1))


def _deltanet(proj3, colp, rowp, conv_w, dn_norm_w):
    b, s, _ = proj3.shape
    kw = DN_HG * DN_K_DIM
    vw = DN_HG * VW
    q0, k0, v0, z0 = P_DQ // kw, P_DK // kw, P_DV // vw, P_DZ // vw
    ck0, cv0 = DN_QK_W // kw, 2 * DN_QK_W // vw
    hb = DN_TS // HALO_BLK

    def halo(t):
        return jnp.maximum(t * hb - 1, 0)

    return pl.pallas_call(
        _dnet_kernel,
        grid=(b, DN_K_HEADS // DN_HG, s // DN_TS),
        in_specs=[
            pl.BlockSpec((1, DN_TS, kw), lambda i, g, t: (i, t, q0 + g)),
            pl.BlockSpec((1, DN_TS, kw), lambda i, g, t: (i, t, k0 + g)),
            pl.BlockSpec((1, DN_TS, vw), lambda i, g, t: (i, t, v0 + g)),
            pl.BlockSpec((1, DN_TS, vw), lambda i, g, t: (i, t, z0 + g)),
            pl.BlockSpec((1, HALO_BLK, kw), lambda i, g, t: (i, halo(t), q0 + g)),
            pl.BlockSpec((1, HALO_BLK, kw), lambda i, g, t: (i, halo(t), k0 + g)),
            pl.BlockSpec((1, HALO_BLK, vw), lambda i, g, t: (i, halo(t), v0 + g)),
            pl.BlockSpec((1, 1, DN_TS, SMALL_W), lambda i, g, t: (i, g, t, 0)),
            pl.BlockSpec((1, DN_TS // GATE_ROWS, GROUP_VH, GATE_ROWS), lambda i, g, t: (i, t, g, 0)),
            pl.BlockSpec((CONV_WIDTH, kw), lambda i, g, t: (0, g)),
            pl.BlockSpec((CONV_WIDTH, kw), lambda i, g, t: (0, ck0 + g)),
            pl.BlockSpec((CONV_WIDTH, vw), lambda i, g, t: (0, cv0 + g)),
            pl.BlockSpec((1, DN_V_DIM), lambda i, g, t: (0, 0)),
        ],
        out_specs=pl.BlockSpec((1, DN_TS, vw), lambda i, g, t: (i, t, g)),
        out_shape=jax.ShapeDtypeStruct((b, s, DN_V_W), BF16),
        scratch_shapes=[
            pltpu.VMEM((DN_HG * V_PER_K, DN_K_DIM, DN_V_DIM), F32),
            pltpu.VMEM((DN_HG, RING * PAIR_ROWS, VW), F32),
            pltpu.VMEM((DN_HG, RING * PAIR_ROWS, VW), BF16),
            pltpu.VMEM((DN_HG, RING * PAIR_ROWS, VW), BF16),
            pltpu.VMEM((DN_HG, V_PER_K * RING * PAIR_ROWS, DN_K_DIM), BF16),
            pltpu.VMEM((DN_HG, RING * PAIR_ROWS, V_PER_K * CHUNK), BF16),
            pltpu.VMEM((CHUNK + DN_TS, DN_HG * RAW_W), BF16),
            pltpu.VMEM((2, DN_A_CHUNKS * DN_HG, CHUNK, 4 * CHUNK + 2 * DN_K_DIM + VW), F32),
        ],
        compiler_params=pltpu.CompilerParams(
            dimension_semantics=("parallel", "parallel", "arbitrary"),
            vmem_limit_bytes=V7X_VMEM_LIMIT),
        name="dnet",
    )(proj3, proj3, proj3, proj3, proj3, proj3, proj3, colp, rowp, conv_w, conv_w, conv_w,
      dn_norm_w)


MERGE_TM = 512
MERGE_TN = 1024


def _merge_kernel(oa_ref, od_ref, wa_ref, wd_ref, ga_ref, gd_ref, o_ref):
    ya = jnp.dot(oa_ref[...], wa_ref[...], preferred_element_type=F32)
    yd = jnp.dot(od_ref[...], wd_ref[...], preferred_element_type=F32)
    ga = _sigmoid(ga_ref[...].astype(F32))
    gd = _sigmoid(gd_ref[...].astype(F32))
    o_ref[...] = (ga * ya + gd * yd).astype(o_ref.dtype)


def _merge(oa, od, wa, wd, proj2):
    m = oa.shape[0]
    g0 = P_G // MERGE_TN
    g1 = (P_G + D_MODEL) // MERGE_TN
    return pl.pallas_call(
        _merge_kernel,
        grid=(m // MERGE_TM, D_MODEL // MERGE_TN),
        in_specs=[
            pl.BlockSpec((MERGE_TM, ATT_W), lambda i, j: (i, 0)),
            pl.BlockSpec((MERGE_TM, DN_V_W), lambda i, j: (i, 0)),
            pl.BlockSpec((ATT_W, MERGE_TN), lambda i, j: (0, j)),
            pl.BlockSpec((DN_V_W, MERGE_TN), lambda i, j: (0, j)),
            pl.BlockSpec((MERGE_TM, MERGE_TN), lambda i, j: (i, g0 + j)),
            pl.BlockSpec((MERGE_TM, MERGE_TN), lambda i, j: (i, g1 + j)),
        ],
        out_specs=pl.BlockSpec((MERGE_TM, MERGE_TN), lambda i, j: (i, j)),
        out_shape=jax.ShapeDtypeStruct((m, D_MODEL), BF16),
        compiler_params=pltpu.CompilerParams(
            dimension_semantics=("parallel", "parallel"),
            vmem_limit_bytes=V7X_VMEM_LIMIT),
        name="merge",
    )(oa, od, wa, wd, proj2, proj2)


OUT_TM = 512


def _out_kernel(m_ref, w_ref, x_ref, nw_ref, o_ref):
    y = x_ref[...] + jnp.dot(m_ref[...], w_ref[...], preferred_element_type=F32)
    ms = jnp.mean(y * y, axis=-1, keepdims=True)
    o_ref[...] = (y * lax.rsqrt(ms + EPS)) * nw_ref[...]


def _outproj(merged, w_out, x2, final_norm_w):
    m = x2.shape[0]
    return pl.pallas_call(
        _out_kernel,
        grid=(m // OUT_TM,),
        in_specs=[
            pl.BlockSpec((OUT_TM, D_MODEL), lambda i: (i, 0)),
            pl.BlockSpec((D_MODEL, D_MODEL), lambda i: (0, 0)),
            pl.BlockSpec((OUT_TM, D_MODEL), lambda i: (i, 0)),
            pl.BlockSpec((1, D_MODEL), lambda i: (0, 0)),
        ],
        out_specs=pl.BlockSpec((OUT_TM, D_MODEL), lambda i: (i, 0)),
        out_shape=jax.ShapeDtypeStruct((m, D_MODEL), F32),
        compiler_params=pltpu.CompilerParams(
            dimension_semantics=("parallel",),
            vmem_limit_bytes=V7X_VMEM_LIMIT),
        name="outproj",
    )(merged, w_out, x2, final_norm_w)


def _layer(x, norm_w, w_in, b_qkv, sinks, conv_w, a_log, dt_bias, dn_norm_w,
           w_att_branch, w_dn_branch, w_out, out_norm_w):
    b, s, d = x.shape
    m = b * s
    x2 = x.reshape(m, d)

    w_t = w_in.T
    wk = w_t[OFF_AK:OFF_AV].reshape(N_KV_HEADS, 1, HEAD_DIM, d)
    wv = w_t[OFF_AV:OFF_AZ].reshape(N_KV_HEADS, 1, HEAD_DIM, d)
    w_kv_t = jnp.concatenate([wk, wv], axis=1).reshape(2 * ATT_KV_W, d)
    w_small_t = jnp.pad(w_t[OFF_DB:OFF_G], ((0, SMALL_W - 2 * DN_V_HEADS), (0, 0)))
    bk = b_qkv[ATT_Q_W:ATT_Q_W + ATT_KV_W].reshape(N_KV_HEADS, 1, HEAD_DIM)
    bv = b_qkv[ATT_Q_W + ATT_KV_W:].reshape(N_KV_HEADS, 1, HEAD_DIM)
    bias = jnp.concatenate([b_qkv[:ATT_Q_W], jnp.zeros((P_KV - P_AZ,), F32),
                            jnp.concatenate([bk, bv], axis=1).reshape(-1)]).reshape(1, MAIN_W)

    proj, small = _inproj(x2, norm_w.reshape(1, d), w_t, w_kv_t, bias, w_small_t)
    proj3 = proj.reshape(b, s, MAIN_W)

    lane_pad = (DN_V_HEADS, SMALL_W - 2 * DN_V_HEADS)
    alog_row = jnp.pad(a_log, lane_pad).reshape(1, SMALL_W)
    dt_row = jnp.pad(dt_bias, lane_pad).reshape(1, SMALL_W)
    colp, rowp = _gates(small.reshape(b, s, SMALL_W), alog_row, dt_row)

    o_att = _attention(proj3, sinks)
    o_dn = _deltanet(proj3, colp, rowp, conv_w, dn_norm_w.reshape(1, DN_V_DIM))

    merged = _merge(o_att.reshape(m, ATT_W), o_dn.reshape(m, DN_V_W),
                    w_att_branch.astype(BF16), w_dn_branch.astype(BF16), proj)
    y = _outproj(merged, w_out.astype(BF16), x2, out_norm_w.reshape(1, d))
    return y.reshape(b, s, d)


def kernel(x, norm_w, w_in, b_qkv, sinks, conv_w, a_log, dt_bias, dn_norm_w,
           w_att_branch, w_dn_branch, w_out, final_norm_w):
    depth = norm_w.shape[0]
    assert depth == 1, "the final RMSNorm is fused into the single layer's output kernel"
    return _layer(x, norm_w[0], w_in[0], b_qkv[0], sinks[0], conv_w[0], a_log[0], dt_bias[0],
                  dn_norm_w[0], w_att_branch[0], w_dn_branch[0], w_out[0], final_norm_w)
```

```python
import functools

import jax
import jax.numpy as jnp
from jax import lax
from jax.experimental import pallas as pl
from jax.experimental.pallas import tpu as pltpu

F32 = jnp.float32
BF16 = jnp.bfloat16

D_MODEL = 2048
HEAD_DIM = 64
N_Q_HEADS = 32
N_KV_HEADS = 4
GROUP = N_Q_HEADS // N_KV_HEADS
WINDOW = 128
ATT_BLOCK = 128
ATT_Q_W = N_Q_HEADS * HEAD_DIM
ATT_KV_W = N_KV_HEADS * HEAD_DIM
ATT_W = ATT_Q_W

DN_K_HEADS = 16
DN_V_HEADS = 32
DN_K_DIM = 128
DN_V_DIM = 128
DN_QK_W = DN_K_HEADS * DN_K_DIM
DN_V_W = DN_V_HEADS * DN_V_DIM
DN_CONV_CH = 2 * DN_QK_W + DN_V_W
CONV_WIDTH = 4
CHUNK = 64
EPS = 1e-6

OFF_AQ = 0
OFF_AK = OFF_AQ + ATT_Q_W
OFF_AV = OFF_AK + ATT_KV_W
OFF_AZ = OFF_AV + ATT_KV_W
OFF_DQKV = OFF_AZ + ATT_W
OFF_DZ = OFF_DQKV + DN_CONV_CH
OFF_DB = OFF_DZ + DN_V_W
OFF_DA = OFF_DB + DN_V_HEADS
OFF_G = OFF_DA + DN_V_HEADS
IN_W = OFF_G + 2 * D_MODEL

MAIN_W = IN_W - 2 * DN_V_HEADS
P_AQ = 0
P_AZ = P_AQ + ATT_Q_W
P_DQ = P_AZ + ATT_W
P_DK = P_DQ + DN_QK_W
P_DV = P_DK + DN_QK_W
P_DZ = P_DV + DN_V_W
P_G = P_DZ + DN_V_W
P_KV = P_G + 2 * D_MODEL
SMALL_W = 128

V7X_VMEM_LIMIT = 56 * 1024 * 1024


def _sigmoid(v):
    return 0.5 * jnp.tanh(0.5 * v) + 0.5


def _silu(v):
    h = 0.5 * v
    return h * jnp.tanh(h) + h


def _mm(a, b):
    return jnp.dot(a.astype(BF16), b.astype(BF16), preferred_element_type=F32)


INPROJ_TM = 2048
INPROJ_TN = 512
NORM_ROWS = 128


NT_DIMS = (((1,), (1,)), ((), ()))


def _inproj_kernel(x_ref, nw_ref, wt_ref, wkv_ref, b_ref, ws_ref, o_ref, os_ref, h_ref):
    j = pl.program_id(1)

    @pl.when(j == 0)
    def _():
        def body(r, carry):
            r0 = pl.multiple_of(r * NORM_ROWS, NORM_ROWS)
            xv = x_ref[pl.ds(r0, NORM_ROWS), :]
            ms = jnp.mean(xv * xv, axis=-1, keepdims=True)
            hv = (xv * lax.rsqrt(ms + EPS)) * nw_ref[...]
            h_ref[pl.ds(r0, NORM_ROWS), :] = hv.astype(BF16)
            return carry
        lax.fori_loop(0, INPROJ_TM // NORM_ROWS, body, 0)
        os_ref[...] = lax.dot_general(h_ref[...], ws_ref[...].astype(BF16), NT_DIMS,
                                      preferred_element_type=F32)

    def project(w_ref):
        acc = lax.dot_general(h_ref[...], w_ref[...].astype(BF16), NT_DIMS,
                              preferred_element_type=F32)
        o_ref[...] = (acc + b_ref[...]).astype(o_ref.dtype)

    @pl.when(j < DIRECT_BLOCKS)
    def _():
        project(wt_ref)

    @pl.when(j >= DIRECT_BLOCKS)
    def _():
        project(wkv_ref)


DIRECT_BLOCKS = P_KV // INPROJ_TN


def _inproj(x2, norm_w, w_t, w_kv_t, bias, w_small_t):
    m = x2.shape[0]
    grid = (m // INPROJ_TM, MAIN_W // INPROJ_TN)
    n_q, n_mid = ATT_Q_W // INPROJ_TN, P_G // INPROJ_TN

    def direct_rows(i, j):
        jj = jnp.minimum(j, DIRECT_BLOCKS - 1)
        row = jnp.where(jj < n_q, OFF_AQ + INPROJ_TN * jj,
                        jnp.where(jj < n_mid, OFF_AZ + INPROJ_TN * (jj - n_q),
                                  OFF_G + INPROJ_TN * (jj - n_mid)))
        return pl.multiple_of(row, HEAD_DIM), 0

    return pl.pallas_call(
        _inproj_kernel,
        grid=grid,
        in_specs=[
            pl.BlockSpec((INPROJ_TM, D_MODEL), lambda i, j: (i, 0), pipeline_mode=pl.Buffered(1)),
            pl.BlockSpec((1, D_MODEL), lambda i, j: (0, 0)),
            pl.BlockSpec((pl.Element(INPROJ_TN), pl.Element(D_MODEL)), direct_rows),
            pl.BlockSpec((INPROJ_TN, D_MODEL), lambda i, j: (0, 0), pipeline_mode=pl.Buffered(1)),
            pl.BlockSpec((1, INPROJ_TN), lambda i, j: (0, j)),
            pl.BlockSpec((SMALL_W, D_MODEL), lambda i, j: (0, 0), pipeline_mode=pl.Buffered(1)),
        ],
        out_specs=[
            pl.BlockSpec((INPROJ_TM, INPROJ_TN), lambda i, j: (i, j)),
            pl.BlockSpec((INPROJ_TM, SMALL_W), lambda i, j: (i, 0)),
        ],
        out_shape=[
            jax.ShapeDtypeStruct((m, MAIN_W), BF16),
            jax.ShapeDtypeStruct((m, SMALL_W), F32),
        ],
        scratch_shapes=[pltpu.VMEM((INPROJ_TM, D_MODEL), BF16)],
        compiler_params=pltpu.CompilerParams(
            dimension_semantics=("parallel", "arbitrary"),
            vmem_limit_bytes=V7X_VMEM_LIMIT),
        name="inproj",
    )(x2, norm_w, w_t, w_kv_t, bias, w_small_t)


GATE_ROWS = 2 * CHUNK
DN_HG = 4
GATE_GROUPS = DN_K_HEADS // DN_HG
GROUP_VH = DN_V_HEADS // GATE_GROUPS


def _gates_kernel(s_ref, alog_ref, dt_ref, col_ref, row_ref):
    seq = s_ref.shape[1]
    row = lax.broadcasted_iota(jnp.int32, (CHUNK, SMALL_W), 0)
    lane = lax.broadcasted_iota(jnp.int32, (GATE_ROWS, SMALL_W), 1)

    def body(blk, carry):
        r0 = pl.multiple_of(blk * GATE_ROWS, GATE_ROWS)
        xs = s_ref[0, pl.ds(r0, GATE_ROWS), :]
        beta = _sigmoid(xs)
        t = xs + dt_ref[...]
        softplus = jnp.maximum(t, 0.0) + jnp.log(1.0 + jnp.exp(-jnp.abs(t)))
        g = -jnp.exp(alog_ref[...]) * softplus
        halves = []
        for part in range(GATE_ROWS // CHUNK):
            gp = g[part * CHUNK:(part + 1) * CHUNK]
            shift = 1
            while shift < CHUNK:
                gp = gp + jnp.where(row >= shift, pltpu.roll(gp, shift, axis=0), 0.0)
                shift *= 2
            halves.append(gp)
        gcum = jnp.concatenate(halves, axis=0)
        for grp in range(GATE_GROUPS):
            b_g = pltpu.roll(beta, (SMALL_W - grp * GROUP_VH) % SMALL_W, axis=1)
            g_g = pltpu.roll(gcum, SMALL_W - DN_V_HEADS - grp * GROUP_VH + GROUP_VH, axis=1)
            col_ref[0, grp, pl.ds(r0, GATE_ROWS), :] = jnp.where(lane < GROUP_VH, b_g, g_g)
        row_ref[0, blk] = gcum.T[DN_V_HEADS:2 * DN_V_HEADS]
        return carry

    lax.fori_loop(0, seq // GATE_ROWS, body, 0)


def _gates(small3, alog_row, dt_row):
    b, s, _ = small3.shape
    return pl.pallas_call(
        _gates_kernel,
        grid=(b,),
        in_specs=[
            pl.BlockSpec((1, s, SMALL_W), lambda i: (i, 0, 0)),
            pl.BlockSpec((1, SMALL_W), lambda i: (0, 0)),
            pl.BlockSpec((1, SMALL_W), lambda i: (0, 0)),
        ],
        out_specs=[
            pl.BlockSpec((1, GATE_GROUPS, s, SMALL_W), lambda i: (i, 0, 0, 0)),
            pl.BlockSpec((1, s // GATE_ROWS, DN_V_HEADS, GATE_ROWS), lambda i: (i, 0, 0, 0)),
        ],
        out_shape=[
            jax.ShapeDtypeStruct((b, GATE_GROUPS, s, SMALL_W), F32),
            jax.ShapeDtypeStruct((b, s // GATE_ROWS, DN_V_HEADS, GATE_ROWS), F32),
        ],
        compiler_params=pltpu.CompilerParams(dimension_semantics=("parallel",)),
        name="gates",
    )(small3, alog_row, dt_row)


PAIR_W = 2 * HEAD_DIM
KVH_W = GROUP * HEAD_DIM


def _attn_kernel(sinks_ref, q_ref, kvp_ref, kvc_ref, z_ref, o_ref):
    n = pl.program_id(1)
    L = ATT_BLOCK
    lane = lax.broadcasted_iota(jnp.int32, (L, PAIR_W), 1)
    lo = lane < HEAD_DIM

    def prep(kv):
        kv = kv.astype(F32)
        sw = pltpu.roll(kv, HEAD_DIM, axis=1)
        kk = jnp.where(lo, kv, sw).astype(BF16)
        va = jnp.where(lo, sw, 0.0).astype(BF16)
        vb = jnp.where(lo, 0.0, kv).astype(BF16)
        return kk, va, vb

    assert WINDOW == ATT_BLOCK
    qi = lax.broadcasted_iota(jnp.int32, (L, L), 0)
    li = lax.broadcasted_iota(jnp.int32, (L, L), 1)
    from_prev = li > qi
    prev_bias = jnp.where(n > 0, 0.0, -jnp.inf)

    def kv_head(h):
        kcols = slice(h * PAIR_W, (h + 1) * PAIR_W)
        kk_p, va_p, vb_p = prep(kvp_ref[0, :, kcols])
        kk_c, va_c, vb_c = prep(kvc_ref[0, :, kcols])
        kk = jnp.concatenate([kk_p, kk_c], axis=0)
        va = jnp.concatenate([va_p, va_c], axis=0)
        vb = jnp.concatenate([vb_p, vb_c], axis=0)
        pair_cols = [slice(h * KVH_W + pr * PAIR_W, h * KVH_W + (pr + 1) * PAIR_W)
                     for pr in range(GROUP // 2)]
        scores = []
        for cols in pair_cols:
            q2 = q_ref[0, :, cols].astype(F32) * (HEAD_DIM ** -0.5)
            for qm in (jnp.where(lo, q2, 0.0), jnp.where(lo, 0.0, q2)):
                scores.append(lax.dot_general(qm.astype(BF16), kk, (((1,), (1,)), ((), ())),
                                              preferred_element_type=F32))
        probs, inv = [], []
        for i, s in enumerate(scores):
            sink = sinks_ref[h * GROUP + i]
            s = jnp.where(from_prev, s[:, :L] + prev_bias, s[:, L:])
            m = jnp.maximum(jnp.max(s, axis=-1, keepdims=True), sink)
            p = jnp.exp(s - m)
            inv.append(1.0 / (jnp.sum(p, axis=-1, keepdims=True) + jnp.exp(sink - m)))
            probs.append(jnp.concatenate([jnp.where(from_prev, p, 0.0),
                                          jnp.where(from_prev, 0.0, p)], axis=1).astype(BF16))
        for pr, cols in enumerate(pair_cols):
            o2 = (jnp.dot(probs[2 * pr], va, preferred_element_type=F32)
                  + jnp.dot(probs[2 * pr + 1], vb, preferred_element_type=F32))
            o2 = o2 * jnp.where(lo, inv[2 * pr], inv[2 * pr + 1])
            zz = z_ref[0, :, cols].astype(F32)
            o_ref[0, :, cols] = (o2 * _silu(zz)).astype(o_ref.dtype)

    for h in range(N_KV_HEADS):
        kv_head(h)


def _attention(proj3, sinks):
    b, s, _ = proj3.shape
    nb = s // ATT_BLOCK
    kvw = 2 * ATT_KV_W
    kv0 = P_KV // kvw
    z0 = P_AZ // ATT_W
    return pl.pallas_call(
        _attn_kernel,
        grid=(b, nb),
        in_specs=[
            pl.BlockSpec(memory_space=pltpu.SMEM),
            pl.BlockSpec((1, ATT_BLOCK, ATT_Q_W), lambda i, n: (i, n, 0)),
            pl.BlockSpec((1, ATT_BLOCK, kvw), lambda i, n: (i, jnp.maximum(n - 1, 0), kv0)),
            pl.BlockSpec((1, ATT_BLOCK, kvw), lambda i, n: (i, n, kv0)),
            pl.BlockSpec((1, ATT_BLOCK, ATT_W), lambda i, n: (i, n, z0)),
        ],
        out_specs=pl.BlockSpec((1, ATT_BLOCK, ATT_W), lambda i, n: (i, n, 0)),
        out_shape=jax.ShapeDtypeStruct((b, s, ATT_W), BF16),
        compiler_params=pltpu.CompilerParams(
            dimension_semantics=("parallel", "parallel")),
        name="attn",
    )(sinks, proj3, proj3, proj3, proj3)


V_PER_K = DN_V_HEADS // DN_K_HEADS
assert V_PER_K == 2 and V_PER_K * CHUNK == 128, "two value heads are packed side by side in 128 lanes"
VW = V_PER_K * DN_V_DIM
DN_TS = 2048
DN_A_CHUNKS = 2
assert DN_A_CHUNKS % (GATE_ROWS // CHUNK) == 0
PAIR_ROWS = DN_A_CHUNKS * CHUNK
RING = 2
A1_LEVELS = 3
RAW_W = 2 * DN_K_DIM + VW
HALO_BLK = 16


def _dnet_kernel(q_ref, k_ref, v_ref, z_ref, qh_ref, kh_ref, vh_ref, col_ref, row_ref,
                 cwq_ref, cwk_ref, cwv_ref, nw_ref, o_ref,
                 st_ref, u_ref, w_ref, qe_ref, kd_ref, qkd_ref, raw_ref, mid_ref):
    t = pl.program_id(2)
    C = CHUNK
    ri = lax.broadcasted_iota(jnp.int32, (C, 2 * C), 0)
    li = lax.broadcasted_iota(jnp.int32, (C, 2 * C), 1)
    lo = li < C
    ci = jnp.where(lo, li, li - C)
    tril = ri >= ci
    strict = ri > ci
    eye = (ri == ci).astype(F32)
    top = lax.broadcasted_iota(jnp.int32, (2 * C, 2 * C), 0) < C
    left = lax.broadcasted_iota(jnp.int32, (2 * C, 2 * C), 1) < C
    diag_blk = top == left

    @pl.when(t == 0)
    def _():
        st_ref[...] = jnp.zeros_like(st_ref)

    def head_cols(ref, hh):
        per_head = ref.shape[2] // DN_HG
        return slice(hh * per_head, (hh + 1) * per_head)

    raw_off = (0, DN_K_DIM, 2 * DN_K_DIM)
    raw_ref[0:C - HALO_BLK, :] = jnp.zeros((C - HALO_BLK, raw_ref.shape[1]), BF16)
    for ref, halo_ref, off in zip((q_ref, k_ref, v_ref), (qh_ref, kh_ref, vh_ref), raw_off):
        tail = halo_ref[0]
        tail = jnp.where(t > 0, tail, jnp.zeros_like(tail))
        for hh in range(DN_HG):
            hc = head_cols(ref, hh)
            raw_ref[C - HALO_BLK:C, hh * RAW_W + off:hh * RAW_W + off + hc.stop - hc.start] = tail[:, hc]

    def stage_rows(r, carry):
        r0 = pl.multiple_of(r * C, C)
        for ref, off in zip((q_ref, k_ref, v_ref), raw_off):
            for hh in range(DN_HG):
                hc = head_cols(ref, hh)
                raw_ref[pl.ds(r0 + C, C), hh * RAW_W + off:hh * RAW_W + off + hc.stop - hc.start] = (
                    ref[0, pl.ds(r0, C), hc])
        return carry

    lax.fori_loop(0, DN_TS // C, stage_rows, 0)

    sr = lax.broadcasted_iota(jnp.int32, ((CONV_WIDTH - 1) * C, 2 * C), 0)
    sc = lax.broadcasted_iota(jnp.int32, ((CONV_WIDTH - 1) * C, 2 * C), 1)
    shift_mat = (sc == C + jnp.bitwise_and(sr, C - 1)
                 - (jnp.right_shift(sr, C.bit_length() - 1) + 1)).astype(BF16)

    def conv_silu(hh, r0):
        rcols = slice(hh * RAW_W, (hh + 1) * RAW_W)
        w = jnp.concatenate([cwq_ref[:, head_cols(q_ref, hh)], cwk_ref[:, head_cols(k_ref, hh)],
                             cwv_ref[:, head_cols(v_ref, hh)]], axis=1)
        window = raw_ref[pl.ds(r0, 2 * C), rcols]
        taps = jnp.dot(shift_mat, window, preferred_element_type=F32)
        y = window[C:].astype(F32) * w[CONV_WIDTH - 1:CONV_WIDTH, :]
        for s in range(1, CONV_WIDTH):
            y = y + taps[(s - 1) * C:s * C] * w[CONV_WIDTH - 1 - s:CONV_WIDTH - s, :]
        return _silu(y)

    def l2n(v):
        return v * lax.rsqrt(jnp.sum(v * v, axis=-1, keepdims=True) + EPS)

    def bcast(col):
        return jnp.broadcast_to(col, (C, 2 * C))

    def block_diag(m):
        return jnp.where(diag_blk, jnp.concatenate([m, m], axis=0), 0.0).astype(BF16)

    vslices = [slice(j * DN_V_DIM, (j + 1) * DN_V_DIM) for j in range(V_PER_K)]
    lane_sel = [lo, jnp.logical_not(lo)]


    per_tile = GATE_ROWS // C
    unit_heads = [hh for _ in range(DN_A_CHUNKS) for hh in range(DN_HG)]
    unit_halves = [cc % per_tile for cc in range(DN_A_CHUNKS) for _ in range(DN_HG)]
    unit_tiles = [cc // per_tile for cc in range(DN_A_CHUNKS) for _ in range(DN_HG)]
    n_units = len(unit_heads)
    mid_cols, offset = {}, 0
    for name, width in (("p", 2 * C), ("x", 2 * C), ("q", DN_K_DIM), ("k", DN_K_DIM), ("v", VW)):
        mid_cols[name] = slice(offset, offset + width)
        offset += width

    def unit_rows(it):
        return [pl.multiple_of((it * DN_A_CHUNKS + cc) * C, C)
                for cc in range(DN_A_CHUNKS) for _ in range(DN_HG)]

    def ring_row(it, cc):
        return pl.multiple_of((it % RING) * PAIR_ROWS + cc * C, C)

    def unit_ring_rows(it):
        return [ring_row(it, cc) for cc in range(DN_A_CHUNKS) for _ in range(DN_HG)]

    def gate_terms(it, r0s):
        g_rows = [row_ref[0, it * (DN_A_CHUNKS // per_tile) + r] for r in range(DN_A_CHUNKS // per_tile)]
        beta, gc, gr, g_last = [], [], [], []
        for i in range(n_units):
            colp = col_ref[0, 0, pl.ds(r0s[i], C), :]
            vh = [unit_heads[i] * V_PER_K + j for j in range(V_PER_K)]
            beta.append([bcast(colp[:, h:h + 1]) for h in vh])
            gc.append([bcast(colp[:, GROUP_VH + h:GROUP_VH + h + 1]) for h in vh])
            tok = slice(unit_halves[i] * C, (unit_halves[i] + 1) * C)
            gr.append(jnp.concatenate([g_rows[unit_tiles[i]][h:h + 1, tok] for h in vh], axis=1))
            g_last.append([gr[i][:, (j + 1) * C - 1:(j + 1) * C] for j in range(V_PER_K)])
        return beta, gc, gr, g_last

    def neumann_level(p, x):
        for i in range(n_units):
            rhs = jnp.concatenate([block_diag(p[i]), block_diag(x[i])], axis=1)
            pp_px = jnp.dot(p[i].astype(BF16), rhs, preferred_element_type=F32)
            x[i] = x[i] + pp_px[:, 2 * C:]
            p[i] = pp_px[:, :2 * C]

    n_levels = C.bit_length() - 2

    def stage_a1(it):
        r0s = unit_rows(it)
        q, k, v = [], [], []
        for i in range(n_units):
            y = conv_silu(unit_heads[i], r0s[i])
            q.append(l2n(y[:, raw_off[0]:raw_off[1]]) * (DN_K_DIM ** -0.5))
            k.append(l2n(y[:, raw_off[1]:raw_off[2]]))
            v.append(y[:, raw_off[2]:])
        yield
        qk_kk = []
        for i in range(n_units):
            kb = k[i].astype(BF16)
            qk_kk.append(lax.dot_general(jnp.concatenate([q[i].astype(BF16), kb], axis=0),
                                         jnp.concatenate([kb, kb], axis=0),
                                         (((1,), (1,)), ((), ())), preferred_element_type=F32))
        yield
        beta, gc, gr, _ = gate_terms(it, r0s)
        ring = unit_ring_rows(it)
        x, p = [], []
        for i in range(n_units):
            beta_p = jnp.where(lo, beta[i][0], beta[i][1])
            gc_p = jnp.where(lo, gc[i][0], gc[i][1])
            decay = jnp.exp(jnp.where(tril, gc_p - gr[i], -jnp.inf))
            a = jnp.where(strict, beta_p * qk_kk[i][C:] * decay, 0.0)
            qkd_ref[unit_heads[i], pl.ds(ring[i], C), :] = (qk_kk[i][:C] * decay).astype(BF16)
            x.append(eye - a)
            p.append(jnp.dot(a.astype(BF16), block_diag(a), preferred_element_type=F32))
        yield
        for _ in range(A1_LEVELS):
            neumann_level(p, x)
            yield
        for i in range(n_units):
            mid_ref[it % 2, i] = jnp.concatenate([p[i], x[i], q[i], k[i], v[i]], axis=1)

    def stage_a2(it):
        r0s = unit_rows(it)
        ring = unit_ring_rows(it)
        mid = [mid_ref[it % 2, i] for i in range(n_units)]
        p = [m[:, mid_cols["p"]] for m in mid]
        x = [m[:, mid_cols["x"]] for m in mid]
        for _ in range(n_levels - A1_LEVELS):
            neumann_level(p, x)
            yield
        beta, gc, _, g_last = gate_terms(it, r0s)
        sols = []
        for i in range(n_units):
            q, k, v = (mid[i][:, mid_cols[name]] for name in ("q", "k", "v"))
            eg = [jnp.exp(gc[i][j]) for j in range(V_PER_K)]
            rhs = jnp.concatenate(
                [jnp.concatenate([v[:, vslices[j]] * beta[i][j],
                                  k * (beta[i][j] * eg[j])], axis=1) for j in range(V_PER_K)],
                axis=0).astype(BF16)
            sols.append([jnp.dot(jnp.where(lane_sel[j], x[i], 0.0).astype(BF16), rhs,
                                 preferred_element_type=F32) for j in range(V_PER_K)])
            hh, r0 = unit_heads[i], ring[i]
            qe_ref[hh, pl.ds(r0, C), :] = jnp.concatenate(
                [q * eg[j] for j in range(V_PER_K)], axis=1).astype(BF16)
            kd_ref[hh, pl.ds(pl.multiple_of(V_PER_K * r0, V_PER_K * C), V_PER_K * C), :] = jnp.concatenate(
                [k * jnp.exp(g_last[i][j] - gc[i][j]) for j in range(V_PER_K)],
                axis=0).astype(BF16)
        yield
        for i in range(n_units):
            hh, r0 = unit_heads[i], ring[i]
            u_ref[hh, pl.ds(r0, C), :] = jnp.concatenate(
                [s[:, :DN_V_DIM] for s in sols[i]], axis=1)
            w_ref[hh, pl.ds(r0, C), :] = jnp.concatenate(
                [s[:, DN_V_DIM:] for s in sols[i]], axis=1).astype(BF16)

    def stage_b(it, cc):
        r0 = pl.multiple_of((it * DN_A_CHUNKS + cc) * C, C)
        rr = ring_row(it, cc)
        zero_blk = jnp.zeros((DN_K_DIM, DN_V_DIM), BF16)
        first = lax.broadcasted_iota(jnp.int32, (C, VW), 1) < DN_V_DIM
        ws_qs, v_new = [], []
        for hh in range(DN_HG):
            s0 = st_ref[hh * V_PER_K].astype(BF16)
            s1 = st_ref[hh * V_PER_K + 1].astype(BF16)
            s_bd = jnp.concatenate([jnp.concatenate([s0, zero_blk], axis=1),
                                    jnp.concatenate([zero_blk, s1], axis=1)], axis=0)
            ws_qs.append(jnp.dot(jnp.concatenate([w_ref[hh, pl.ds(rr, C), :],
                                                  qe_ref[hh, pl.ds(rr, C), :]], axis=0),
                                 s_bd, preferred_element_type=F32))
            v_new.append((u_ref[hh, pl.ds(rr, C), :] - ws_qs[hh][:C]).astype(BF16))
        yield
        v_bd = []
        for hh in range(DN_HG):
            zeros = jnp.zeros_like(v_new[hh])
            v_bd.append(jnp.concatenate([jnp.where(first, v_new[hh], zeros),
                                         jnp.where(first, zeros, v_new[hh])], axis=0))
            kd = kd_ref[hh, pl.ds(pl.multiple_of(V_PER_K * rr, V_PER_K * C), V_PER_K * C), :]
            kv = lax.dot_general(kd, v_bd[hh], (((0,), (0,)), ((), ())),
                                 preferred_element_type=F32)
            for j in range(V_PER_K):
                gl = GROUP_VH + hh * V_PER_K + j
                g_last = col_ref[0, 0, pl.ds(r0 + C - 1, 1), gl:gl + 1]
                st_ref[hh * V_PER_K + j] = (st_ref[hh * V_PER_K + j] * jnp.exp(g_last)
                                            + kv[:, vslices[j]])
        yield
        for hh in range(DN_HG):
            o_both = ws_qs[hh][C:] + jnp.dot(qkd_ref[hh, pl.ds(rr, C), :], v_bd[hh],
                                             preferred_element_type=F32)
            for j in range(V_PER_K):
                o = o_both[:, vslices[j]]
                on = o * lax.rsqrt(jnp.mean(o * o, axis=-1, keepdims=True) + EPS) * nw_ref[...]
                ocols = slice(hh * VW + j * DN_V_DIM, hh * VW + (j + 1) * DN_V_DIM)
                zz = z_ref[0, pl.ds(r0, C), ocols].astype(F32)
                o_ref[0, pl.ds(r0, C), ocols] = (on * _silu(zz)).astype(o_ref.dtype)

    def recurrence(it):
        for cc in range(DN_A_CHUNKS):
            yield from stage_b(it, cc)
            yield

    def emit(main, side=(), main_per_side=1):
        side = iter(side)
        for step, _ in enumerate(main):
            if step % main_per_side == main_per_side - 1:
                next(side, None)
        for _ in side:
            pass

    def stage_a(it):
        yield from stage_a1(it)
        yield
        yield from stage_a2(it)

    n_iter = DN_TS // (C * DN_A_CHUNKS)
    emit(stage_a(0))

    def body(it, carry):
        emit(stage_a(it + 1), recurrence(it))
        return carry

    lax.fori_loop(0, n_iter - 1, body, 0)
    emit(recurrence(n_iter - 1))


def _deltanet(proj3, colp, rowp, conv_w, dn_norm_w):
    b, s, _ = proj3.shape
    kw = DN_HG * DN_K_DIM
    vw = DN_HG * VW
    q0, k0, v0, z0 = P_DQ // kw, P_DK // kw, P_DV // vw, P_DZ // vw
    ck0, cv0 = DN_QK_W // kw, 2 * DN_QK_W // vw
    hb = DN_TS // HALO_BLK

    def halo(t):
        return jnp.maximum(t * hb - 1, 0)

    return pl.pallas_call(
        _dnet_kernel,
        grid=(b, DN_K_HEADS // DN_HG, s // DN_TS),
        in_specs=[
            pl.BlockSpec((1, DN_TS, kw), lambda i, g, t: (i, t, q0 + g)),
            pl.BlockSpec((1, DN_TS, kw), lambda i, g, t: (i, t, k0 + g)),
            pl.BlockSpec((1, DN_TS, vw), lambda i, g, t: (i, t, v0 + g)),
            pl.BlockSpec((1, DN_TS, vw), lambda i, g, t: (i, t, z0 + g)),
            pl.BlockSpec((1, HALO_BLK, kw), lambda i, g, t: (i, halo(t), q0 + g)),
            pl.BlockSpec((1, HALO_BLK, kw), lambda i, g, t: (i, halo(t), k0 + g)),
            pl.BlockSpec((1, HALO_BLK, vw), lambda i, g, t: (i, halo(t), v0 + g)),
            pl.BlockSpec((1, 1, DN_TS, SMALL_W), lambda i, g, t: (i, g, t, 0)),
            pl.BlockSpec((1, DN_TS // GATE_ROWS, GROUP_VH, GATE_ROWS), lambda i, g, t: (i, t, g, 0)),
            pl.BlockSpec((CONV_WIDTH, kw), lambda i, g, t: (0, g)),
            pl.BlockSpec((CONV_WIDTH, kw), lambda i, g, t: (0, ck0 + g)),
            pl.BlockSpec((CONV_WIDTH, vw), lambda i, g, t: (0, cv0 + g)),
            pl.BlockSpec((1, DN_V_DIM), lambda i, g, t: (0, 0)),
        ],
        out_specs=pl.BlockSpec((1, DN_TS, vw), lambda i, g, t: (i, t, g)),
        out_shape=jax.ShapeDtypeStruct((b, s, DN_V_W), BF16),
        scratch_shapes=[
            pltpu.VMEM((DN_HG * V_PER_K, DN_K_DIM, DN_V_DIM), F32),
            pltpu.VMEM((DN_HG, RING * PAIR_ROWS, VW), F32),
            pltpu.VMEM((DN_HG, RING * PAIR_ROWS, VW), BF16),
            pltpu.VMEM((DN_HG, RING * PAIR_ROWS, VW), BF16),
            pltpu.VMEM((DN_HG, V_PER_K * RING * PAIR_ROWS, DN_K_DIM), BF16),
            pltpu.VMEM((DN_HG, RING * PAIR_ROWS, V_PER_K * CHUNK), BF16),
            pltpu.VMEM((CHUNK + DN_TS, DN_HG * RAW_W), BF16),
            pltpu.VMEM((2, DN_A_CHUNKS * DN_HG, CHUNK, 4 * CHUNK + 2 * DN_K_DIM + VW), F32),
        ],
        compiler_params=pltpu.CompilerParams(
            dimension_semantics=("parallel", "parallel", "arbitrary"),
            vmem_limit_bytes=V7X_VMEM_LIMIT),
        name="dnet",
    )(proj3, proj3, proj3, proj3, proj3, proj3, proj3, colp, rowp, conv_w, conv_w, conv_w,
      dn_norm_w)


MERGE_TM = 512
MERGE_TN = 1024


def _merge_kernel(oa_ref, od_ref, wa_ref, wd_ref, ga_ref, gd_ref, o_ref):
    ya = jnp.dot(oa_ref[...], wa_ref[...], preferred_element_type=F32)
    yd = jnp.dot(od_ref[...], wd_ref[...], preferred_element_type=F32)
    ga = _sigmoid(ga_ref[...].astype(F32))
    gd = _sigmoid(gd_ref[...].astype(F32))
    o_ref[...] = (ga * ya + gd * yd).astype(o_ref.dtype)


def _merge(oa, od, wa, wd, proj2):
    m = oa.shape[0]
    g0 = P_G // MERGE_TN
    g1 = (P_G + D_MODEL) // MERGE_TN
    return pl.pallas_call(
        _merge_kernel,
        grid=(m // MERGE_TM, D_MODEL // MERGE_TN),
        in_specs=[
            pl.BlockSpec((MERGE_TM, ATT_W), lambda i, j: (i, 0)),
            pl.BlockSpec((MERGE_TM, DN_V_W), lambda i, j: (i, 0)),
            pl.BlockSpec((ATT_W, MERGE_TN), lambda i, j: (0, j)),
            pl.BlockSpec((DN_V_W, MERGE_TN), lambda i, j: (0, j)),
            pl.BlockSpec((MERGE_TM, MERGE_TN), lambda i, j: (i, g0 + j)),
            pl.BlockSpec((MERGE_TM, MERGE_TN), lambda i, j: (i, g1 + j)),
        ],
        out_specs=pl.BlockSpec((MERGE_TM, MERGE_TN), lambda i, j: (i, j)),
        out_shape=jax.ShapeDtypeStruct((m, D_MODEL), BF16),
        compiler_params=pltpu.CompilerParams(
            dimension_semantics=("parallel", "parallel"),
            vmem_limit_bytes=V7X_VMEM_LIMIT),
        name="merge",
    )(oa, od, wa, wd, proj2, proj2)


OUT_TM = 512


def _out_kernel(m_ref, w_ref, x_ref, nw_ref, o_ref):
    y = x_ref[...] + jnp.dot(m_ref[...], w_ref[...], preferred_element_type=F32)
    ms = jnp.mean(y * y, axis=-1, keepdims=True)
    o_ref[...] = (y * lax.rsqrt(ms + EPS)) * nw_ref[...]


def _outproj(merged, w_out, x2, final_norm_w):
    m = x2.shape[0]
    return pl.pallas_call(
        _out_kernel,
        grid=(m // OUT_TM,),
        in_specs=[
            pl.BlockSpec((OUT_TM, D_MODEL), lambda i: (i, 0)),
            pl.BlockSpec((D_MODEL, D_MODEL), lambda i: (0, 0)),
            pl.BlockSpec((OUT_TM, D_MODEL), lambda i: (i, 0)),
            pl.BlockSpec((1, D_MODEL), lambda i: (0, 0)),
        ],
        out_specs=pl.BlockSpec((OUT_TM, D_MODEL), lambda i: (i, 0)),
        out_shape=jax.ShapeDtypeStruct((m, D_MODEL), F32),
        compiler_params=pltpu.CompilerParams(
            dimension_semantics=("parallel",),
            vmem_limit_bytes=V7X_VMEM_LIMIT),
        name="outproj",
    )(merged, w_out, x2, final_norm_w)


def _layer(x, norm_w, w_in, b_qkv, sinks, conv_w, a_log, dt_bias, dn_norm_w,
           w_att_branch, w_dn_branch, w_out, out_norm_w):
    b, s, d = x.shape
    m = b * s
    x2 = x.reshape(m, d)

    w_t = w_in.T
    wk = w_t[OFF_AK:OFF_AV].reshape(N_KV_HEADS, 1, HEAD_DIM, d)
    wv = w_t[OFF_AV:OFF_AZ].reshape(N_KV_HEADS, 1, HEAD_DIM, d)
    w_kv_t = jnp.concatenate([wk, wv], axis=1).reshape(2 * ATT_KV_W, d)
    w_small_t = jnp.pad(w_t[OFF_DB:OFF_G], ((0, SMALL_W - 2 * DN_V_HEADS), (0, 0)))
    bk = b_qkv[ATT_Q_W:ATT_Q_W + ATT_KV_W].reshape(N_KV_HEADS, 1, HEAD_DIM)
    bv = b_qkv[ATT_Q_W + ATT_KV_W:].reshape(N_KV_HEADS, 1, HEAD_DIM)
    bias = jnp.concatenate([b_qkv[:ATT_Q_W], jnp.zeros((P_KV - P_AZ,), F32),
                            jnp.concatenate([bk, bv], axis=1).reshape(-1)]).reshape(1, MAIN_W)

    proj, small = _inproj(x2, norm_w.reshape(1, d), w_t, w_kv_t, bias, w_small_t)
    proj3 = proj.reshape(b, s, MAIN_W)

    lane_pad = (DN_V_HEADS, SMALL_W - 2 * DN_V_HEADS)
    alog_row = jnp.pad(a_log, lane_pad).reshape(1, SMALL_W)
    dt_row = jnp.pad(dt_bias, lane_pad).reshape(1, SMALL_W)
    colp, rowp = _gates(small.reshape(b, s, SMALL_W), alog_row, dt_row)

    o_att = _attention(proj3, sinks)
    o_dn = _deltanet(proj3, colp, rowp, conv_w, dn_norm_w.reshape(1, DN_V_DIM))

    merged = _merge(o_att.reshape(m, ATT_W), o_dn.reshape(m, DN_V_W),
                    w_att_branch.astype(BF16), w_dn_branch.astype(BF16), proj)
    y = _outproj(merged, w_out.astype(BF16), x2, out_norm_w.reshape(1, d))
    return y.reshape(b, s, d)


def kernel(x, norm_w, w_in, b_qkv, sinks, conv_w, a_log, dt_bias, dn_norm_w,
           w_att_branch, w_dn_branch, w_out, final_norm_w):
    depth = norm_w.shape[0]
    assert depth == 1, "the final RMSNorm is fused into the single layer's output kernel"
    return _layer(x, norm_w[0], w_in[0], b_qkv[0], sinks[0], conv_w[0], a_log[0], dt_bias[0],
                  dn_norm_w[0], w_att_branch[0], w_dn_branch[0], w_out[0], final_norm_w)
```

```python
import jax
import jax.numpy as jnp
from jax import lax
from jax.experimental import pallas as pl
from jax.experimental.pallas import tpu as pltpu

F32 = jnp.float32
BF16 = jnp.bfloat16

D_MODEL = 2048
HEAD_DIM = 64
N_Q_HEADS = 32
N_KV_HEADS = 4
GROUP = N_Q_HEADS // N_KV_HEADS
WINDOW = 128
ATT_BLOCK = 128
ATT_Q_W = N_Q_HEADS * HEAD_DIM
ATT_KV_W = N_KV_HEADS * HEAD_DIM
ATT_W = ATT_Q_W

DN_K_HEADS = 16
DN_V_HEADS = 32
DN_K_DIM = 128
DN_V_DIM = 128
DN_QK_W = DN_K_HEADS * DN_K_DIM
DN_V_W = DN_V_HEADS * DN_V_DIM
DN_CONV_CH = 2 * DN_QK_W + DN_V_W
CONV_WIDTH = 4
CHUNK = 64
EPS = 1e-6

OFF_AQ = 0
OFF_AK = OFF_AQ + ATT_Q_W
OFF_AV = OFF_AK + ATT_KV_W
OFF_AZ = OFF_AV + ATT_KV_W
OFF_DQKV = OFF_AZ + ATT_W
OFF_DZ = OFF_DQKV + DN_CONV_CH
OFF_DB = OFF_DZ + DN_V_W
OFF_DA = OFF_DB + DN_V_HEADS
OFF_G = OFF_DA + DN_V_HEADS
IN_W = OFF_G + 2 * D_MODEL

MAIN_W = IN_W - 2 * DN_V_HEADS
P_AQ = 0
P_AZ = P_AQ + ATT_Q_W
P_DQ = P_AZ + ATT_W
P_DK = P_DQ + DN_QK_W
P_DV = P_DK + DN_QK_W
P_DZ = P_DV + DN_V_W
P_G = P_DZ + DN_V_W
P_KV = P_G + 2 * D_MODEL
SMALL_W = 128

V7X_VMEM_LIMIT = 56 * 1024 * 1024


def _sigmoid(v):
    return 0.5 * jnp.tanh(0.5 * v) + 0.5


def _silu(v):
    h = 0.5 * v
    return h * jnp.tanh(h) + h


INPROJ_TM = 2048
INPROJ_TN = 512
NORM_ROWS = 128


NT_DIMS = (((1,), (1,)), ((), ()))


def _inproj_kernel(x_ref, nw_ref, wt_ref, wkv_ref, b_ref, ws_ref, o_ref, os_ref, h_ref):
    j = pl.program_id(1)

    @pl.when(j == 0)
    def _():
        def body(r, carry):
            r0 = pl.multiple_of(r * NORM_ROWS, NORM_ROWS)
            xv = x_ref[pl.ds(r0, NORM_ROWS), :]
            ms = jnp.mean(xv * xv, axis=-1, keepdims=True)
            hv = (xv * lax.rsqrt(ms + EPS)) * nw_ref[...]
            h_ref[pl.ds(r0, NORM_ROWS), :] = hv.astype(BF16)
            return carry
        lax.fori_loop(0, INPROJ_TM // NORM_ROWS, body, 0)
        os_ref[...] = lax.dot_general(h_ref[...], ws_ref[...].astype(BF16), NT_DIMS,
                                      preferred_element_type=F32)

    def project(w_ref):
        acc = lax.dot_general(h_ref[...], w_ref[...].astype(BF16), NT_DIMS,
                              preferred_element_type=F32)
        o_ref[...] = (acc + b_ref[...]).astype(o_ref.dtype)

    @pl.when(j < DIRECT_BLOCKS)
    def _():
        project(wt_ref)

    @pl.when(j >= DIRECT_BLOCKS)
    def _():
        project(wkv_ref)


DIRECT_BLOCKS = P_KV // INPROJ_TN


def _inproj(x2, norm_w, w_t, w_kv_t, bias, w_small_t):
    m = x2.shape[0]
    grid = (m // INPROJ_TM, MAIN_W // INPROJ_TN)
    n_q, n_mid = ATT_Q_W // INPROJ_TN, P_G // INPROJ_TN

    def direct_rows(i, j):
        jj = jnp.minimum(j, DIRECT_BLOCKS - 1)
        row = jnp.where(jj < n_q, OFF_AQ + INPROJ_TN * jj,
                        jnp.where(jj < n_mid, OFF_AZ + INPROJ_TN * (jj - n_q),
                                  OFF_G + INPROJ_TN * (jj - n_mid)))
        return pl.multiple_of(row, HEAD_DIM), 0

    return pl.pallas_call(
        _inproj_kernel,
        grid=grid,
        in_specs=[
            pl.BlockSpec((INPROJ_TM, D_MODEL), lambda i, j: (i, 0), pipeline_mode=pl.Buffered(1)),
            pl.BlockSpec((1, D_MODEL), lambda i, j: (0, 0)),
            pl.BlockSpec((pl.Element(INPROJ_TN), pl.Element(D_MODEL)), direct_rows),
            pl.BlockSpec((INPROJ_TN, D_MODEL), lambda i, j: (0, 0), pipeline_mode=pl.Buffered(1)),
            pl.BlockSpec((1, INPROJ_TN), lambda i, j: (0, j)),
            pl.BlockSpec((SMALL_W, D_MODEL), lambda i, j: (0, 0), pipeline_mode=pl.Buffered(1)),
        ],
        out_specs=[
            pl.BlockSpec((INPROJ_TM, INPROJ_TN), lambda i, j: (i, j)),
            pl.BlockSpec((INPROJ_TM, SMALL_W), lambda i, j: (i, 0)),
        ],
        out_shape=[
            jax.ShapeDtypeStruct((m, MAIN_W), BF16),
            jax.ShapeDtypeStruct((m, SMALL_W), F32),
        ],
        scratch_shapes=[pltpu.VMEM((INPROJ_TM, D_MODEL), BF16)],
        compiler_params=pltpu.CompilerParams(
            dimension_semantics=("parallel", "arbitrary"),
            vmem_limit_bytes=V7X_VMEM_LIMIT),
        name="inproj",
    )(x2, norm_w, w_t, w_kv_t, bias, w_small_t)


GATE_ROWS = 2 * CHUNK
DN_HG = 4
GATE_GROUPS = DN_K_HEADS // DN_HG
GROUP_VH = DN_V_HEADS // GATE_GROUPS


def _gates_kernel(s_ref, alog_ref, dt_ref, col_ref, row_ref):
    seq = s_ref.shape[1]
    row = lax.broadcasted_iota(jnp.int32, (CHUNK, SMALL_W), 0)
    lane = lax.broadcasted_iota(jnp.int32, (GATE_ROWS, SMALL_W), 1)

    def body(blk, carry):
        r0 = pl.multiple_of(blk * GATE_ROWS, GATE_ROWS)
        xs = s_ref[0, pl.ds(r0, GATE_ROWS), :]
        beta = _sigmoid(xs)
        t = xs + dt_ref[...]
        softplus = jnp.maximum(t, 0.0) + jnp.log(1.0 + jnp.exp(-jnp.abs(t)))
        g = -jnp.exp(alog_ref[...]) * softplus
        halves = []
        for part in range(GATE_ROWS // CHUNK):
            gp = g[part * CHUNK:(part + 1) * CHUNK]
            shift = 1
            while shift < CHUNK:
                gp = gp + jnp.where(row >= shift, pltpu.roll(gp, shift, axis=0), 0.0)
                shift *= 2
            halves.append(gp)
        gcum = jnp.concatenate(halves, axis=0)
        for grp in range(GATE_GROUPS):
            b_g = pltpu.roll(beta, (SMALL_W - grp * GROUP_VH) % SMALL_W, axis=1)
            g_g = pltpu.roll(gcum, SMALL_W - DN_V_HEADS - grp * GROUP_VH + GROUP_VH, axis=1)
            col_ref[0, grp, pl.ds(r0, GATE_ROWS), :] = jnp.where(lane < GROUP_VH, b_g, g_g)
        row_ref[0, blk] = gcum.T[DN_V_HEADS:2 * DN_V_HEADS]
        return carry

    lax.fori_loop(0, seq // GATE_ROWS, body, 0)


def _gates(small3, alog_row, dt_row):
    b, s, _ = small3.shape
    return pl.pallas_call(
        _gates_kernel,
        grid=(b,),
        in_specs=[
            pl.BlockSpec((1, s, SMALL_W), lambda i: (i, 0, 0)),
            pl.BlockSpec((1, SMALL_W), lambda i: (0, 0)),
            pl.BlockSpec((1, SMALL_W), lambda i: (0, 0)),
        ],
        out_specs=[
            pl.BlockSpec((1, GATE_GROUPS, s, SMALL_W), lambda i: (i, 0, 0, 0)),
            pl.BlockSpec((1, s // GATE_ROWS, DN_V_HEADS, GATE_ROWS), lambda i: (i, 0, 0, 0)),
        ],
        out_shape=[
            jax.ShapeDtypeStruct((b, GATE_GROUPS, s, SMALL_W), F32),
            jax.ShapeDtypeStruct((b, s // GATE_ROWS, DN_V_HEADS, GATE_ROWS), F32),
        ],
        compiler_params=pltpu.CompilerParams(dimension_semantics=("parallel",)),
        name="gates",
    )(small3, alog_row, dt_row)


PAIR_W = 2 * HEAD_DIM
KVH_W = GROUP * HEAD_DIM


def _attn_kernel(sinks_ref, q_ref, kvp_ref, kvc_ref, z_ref, o_ref):
    n = pl.program_id(1)
    L = ATT_BLOCK
    lane = lax.broadcasted_iota(jnp.int32, (L, PAIR_W), 1)
    lo = lane < HEAD_DIM

    def prep(kv):
        kv = kv.astype(F32)
        sw = pltpu.roll(kv, HEAD_DIM, axis=1)
        kk = jnp.where(lo, kv, sw).astype(BF16)
        va = jnp.where(lo, sw, 0.0).astype(BF16)
        vb = jnp.where(lo, 0.0, kv).astype(BF16)
        return kk, va, vb

    assert WINDOW == ATT_BLOCK
    qi = lax.broadcasted_iota(jnp.int32, (L, L), 0)
    li = lax.broadcasted_iota(jnp.int32, (L, L), 1)
    from_prev = li > qi
    prev_bias = jnp.where(n > 0, 0.0, -jnp.inf)

    def kv_head(h):
        kcols = slice(h * PAIR_W, (h + 1) * PAIR_W)
        kk_p, va_p, vb_p = prep(kvp_ref[0, :, kcols])
        kk_c, va_c, vb_c = prep(kvc_ref[0, :, kcols])
        kk = jnp.concatenate([kk_p, kk_c], axis=0)
        va = jnp.concatenate([va_p, va_c], axis=0)
        vb = jnp.concatenate([vb_p, vb_c], axis=0)
        pair_cols = [slice(h * KVH_W + pr * PAIR_W, h * KVH_W + (pr + 1) * PAIR_W)
                     for pr in range(GROUP // 2)]
        scores = []
        for cols in pair_cols:
            q2 = q_ref[0, :, cols].astype(F32) * (HEAD_DIM ** -0.5)
            for qm in (jnp.where(lo, q2, 0.0), jnp.where(lo, 0.0, q2)):
                scores.append(lax.dot_general(qm.astype(BF16), kk, (((1,), (1,)), ((), ())),
                                              preferred_element_type=F32))
        probs, inv = [], []
        for i, s in enumerate(scores):
            sink = sinks_ref[h * GROUP + i]
            s = jnp.where(from_prev, s[:, :L] + prev_bias, s[:, L:])
            m = jnp.maximum(jnp.max(s, axis=-1, keepdims=True), sink)
            p = jnp.exp(s - m)
            inv.append(1.0 / (jnp.sum(p, axis=-1, keepdims=True) + jnp.exp(sink - m)))
            probs.append(jnp.concatenate([jnp.where(from_prev, p, 0.0),
                                          jnp.where(from_prev, 0.0, p)], axis=1).astype(BF16))
        for pr, cols in enumerate(pair_cols):
            o2 = (jnp.dot(probs[2 * pr], va, preferred_element_type=F32)
                  + jnp.dot(probs[2 * pr + 1], vb, preferred_element_type=F32))
            o2 = o2 * jnp.where(lo, inv[2 * pr], inv[2 * pr + 1])
            zz = z_ref[0, :, cols].astype(F32)
            o_ref[0, :, cols] = (o2 * _silu(zz)).astype(o_ref.dtype)

    for h in range(N_KV_HEADS):
        kv_head(h)


def _attention(proj3, sinks):
    b, s, _ = proj3.shape
    nb = s // ATT_BLOCK
    kvw = 2 * ATT_KV_W
    kv0 = P_KV // kvw
    z0 = P_AZ // ATT_W
    return pl.pallas_call(
        _attn_kernel,
        grid=(b, nb),
        in_specs=[
            pl.BlockSpec(memory_space=pltpu.SMEM),
            pl.BlockSpec((1, ATT_BLOCK, ATT_Q_W), lambda i, n: (i, n, 0)),
            pl.BlockSpec((1, ATT_BLOCK, kvw), lambda i, n: (i, jnp.maximum(n - 1, 0), kv0)),
            pl.BlockSpec((1, ATT_BLOCK, kvw), lambda i, n: (i, n, kv0)),
            pl.BlockSpec((1, ATT_BLOCK, ATT_W), lambda i, n: (i, n, z0)),
        ],
        out_specs=pl.BlockSpec((1, ATT_BLOCK, ATT_W), lambda i, n: (i, n, 0)),
        out_shape=jax.ShapeDtypeStruct((b, s, ATT_W), BF16),
        compiler_params=pltpu.CompilerParams(
            dimension_semantics=("parallel", "parallel")),
        name="attn",
    )(sinks, proj3, proj3, proj3, proj3)


V_PER_K = DN_V_HEADS // DN_K_HEADS
assert V_PER_K == 2 and V_PER_K * CHUNK == 128, "two value heads are packed side by side in 128 lanes"
VW = V_PER_K * DN_V_DIM
DN_TS = 2048
DN_A_CHUNKS = 2
assert DN_A_CHUNKS % (GATE_ROWS // CHUNK) == 0
PAIR_ROWS = DN_A_CHUNKS * CHUNK
RING = 2
A1_LEVELS = 3
RAW_W = 2 * DN_K_DIM + VW
HALO_BLK = 16


def _dnet_kernel(q_ref, k_ref, v_ref, z_ref, qh_ref, kh_ref, vh_ref, col_ref, row_ref,
                 cwq_ref, cwk_ref, cwv_ref, nw_ref, o_ref,
                 st_ref, u_ref, w_ref, qe_ref, kd_ref, qkd_ref, raw_ref, mid_ref):
    t = pl.program_id(2)
    C = CHUNK
    ri = lax.broadcasted_iota(jnp.int32, (C, 2 * C), 0)
    li = lax.broadcasted_iota(jnp.int32, (C, 2 * C), 1)
    lo = li < C
    ci = jnp.where(lo, li, li - C)
    tril = ri >= ci
    strict = ri > ci
    eye = (ri == ci).astype(F32)
    top = lax.broadcasted_iota(jnp.int32, (2 * C, 2 * C), 0) < C
    left = lax.broadcasted_iota(jnp.int32, (2 * C, 2 * C), 1) < C
    diag_blk = top == left

    @pl.when(t == 0)
    def _():
        st_ref[...] = jnp.zeros_like(st_ref)

    def head_cols(ref, hh):
        per_head = ref.shape[2] // DN_HG
        return slice(hh * per_head, (hh + 1) * per_head)

    raw_off = (0, DN_K_DIM, 2 * DN_K_DIM)
    raw_ref[0:C - HALO_BLK, :] = jnp.zeros((C - HALO_BLK, raw_ref.shape[1]), BF16)
    for ref, halo_ref, off in zip((q_ref, k_ref, v_ref), (qh_ref, kh_ref, vh_ref), raw_off):
        tail = halo_ref[0]
        tail = jnp.where(t > 0, tail, jnp.zeros_like(tail))
        for hh in range(DN_HG):
            hc = head_cols(ref, hh)
            raw_ref[C - HALO_BLK:C, hh * RAW_W + off:hh * RAW_W + off + hc.stop - hc.start] = tail[:, hc]

    def stage_rows(r, carry):
        r0 = pl.multiple_of(r * C, C)
        for ref, off in zip((q_ref, k_ref, v_ref), raw_off):
            for hh in range(DN_HG):
                hc = head_cols(ref, hh)
                raw_ref[pl.ds(r0 + C, C), hh * RAW_W + off:hh * RAW_W + off + hc.stop - hc.start] = (
                    ref[0, pl.ds(r0, C), hc])
        return carry

    lax.fori_loop(0, DN_TS // C, stage_rows, 0)

    sr = lax.broadcasted_iota(jnp.int32, ((CONV_WIDTH - 1) * C, 2 * C), 0)
    sc = lax.broadcasted_iota(jnp.int32, ((CONV_WIDTH - 1) * C, 2 * C), 1)
    shift_mat = (sc == C + jnp.bitwise_and(sr, C - 1)
                 - (jnp.right_shift(sr, C.bit_length() - 1) + 1)).astype(BF16)

    def conv_silu(hh, r0):
        rcols = slice(hh * RAW_W, (hh + 1) * RAW_W)
        w = jnp.concatenate([cwq_ref[:, head_cols(q_ref, hh)], cwk_ref[:, head_cols(k_ref, hh)],
                             cwv_ref[:, head_cols(v_ref, hh)]], axis=1)
        window = raw_ref[pl.ds(r0, 2 * C), rcols]
        taps = jnp.dot(shift_mat, window, preferred_element_type=F32)
        y = window[C:].astype(F32) * w[CONV_WIDTH - 1:CONV_WIDTH, :]
        for s in range(1, CONV_WIDTH):
            y = y + taps[(s - 1) * C:s * C] * w[CONV_WIDTH - 1 - s:CONV_WIDTH - s, :]
        return _silu(y)

    def l2n(v):
        return v * lax.rsqrt(jnp.sum(v * v, axis=-1, keepdims=True) + EPS)

    def bcast(col):
        return jnp.broadcast_to(col, (C, 2 * C))

    def block_diag(m):
        return jnp.where(diag_blk, jnp.concatenate([m, m], axis=0), 0.0).astype(BF16)

    vslices = [slice(j * DN_V_DIM, (j + 1) * DN_V_DIM) for j in range(V_PER_K)]
    lane_sel = [lo, jnp.logical_not(lo)]


    per_tile = GATE_ROWS // C
    unit_heads = [hh for _ in range(DN_A_CHUNKS) for hh in range(DN_HG)]
    unit_halves = [cc % per_tile for cc in range(DN_A_CHUNKS) for _ in range(DN_HG)]
    unit_tiles = [cc // per_tile for cc in range(DN_A_CHUNKS) for _ in range(DN_HG)]
    n_units = len(unit_heads)
    mid_cols, offset = {}, 0
    for name, width in (("p", 2 * C), ("x", 2 * C), ("q", DN_K_DIM), ("k", DN_K_DIM), ("v", VW)):
        mid_cols[name] = slice(offset, offset + width)
        offset += width

    def unit_rows(it):
        return [pl.multiple_of((it * DN_A_CHUNKS + cc) * C, C)
                for cc in range(DN_A_CHUNKS) for _ in range(DN_HG)]

    def ring_row(it, cc):
        return pl.multiple_of((it % RING) * PAIR_ROWS + cc * C, C)

    def unit_ring_rows(it):
        return [ring_row(it, cc) for cc in range(DN_A_CHUNKS) for _ in range(DN_HG)]

    def gate_terms(it, r0s):
        g_rows = [row_ref[0, it * (DN_A_CHUNKS // per_tile) + r] for r in range(DN_A_CHUNKS // per_tile)]
        beta, gc, gr, g_last = [], [], [], []
        for i in range(n_units):
            colp = col_ref[0, 0, pl.ds(r0s[i], C), :]
            vh = [unit_heads[i] * V_PER_K + j for j in range(V_PER_K)]
            beta.append([bcast(colp[:, h:h + 1]) for h in vh])
            gc.append([bcast(colp[:, GROUP_VH + h:GROUP_VH + h + 1]) for h in vh])
            tok = slice(unit_halves[i] * C, (unit_halves[i] + 1) * C)
            gr.append(jnp.concatenate([g_rows[unit_tiles[i]][h:h + 1, tok] for h in vh], axis=1))
            g_last.append([gr[i][:, (j + 1) * C - 1:(j + 1) * C] for j in range(V_PER_K)])
        return beta, gc, gr, g_last

    def neumann_level(p, x):
        for i in range(n_units):
            rhs = jnp.concatenate([block_diag(p[i]), block_diag(x[i])], axis=1)
            pp_px = jnp.dot(p[i].astype(BF16), rhs, preferred_element_type=F32)
            x[i] = x[i] + pp_px[:, 2 * C:]
            p[i] = pp_px[:, :2 * C]

    n_levels = C.bit_length() - 2

    def stage_a1(it):
        r0s = unit_rows(it)
        q, k, v = [], [], []
        for i in range(n_units):
            y = conv_silu(unit_heads[i], r0s[i])
            q.append(l2n(y[:, raw_off[0]:raw_off[1]]) * (DN_K_DIM ** -0.5))
            k.append(l2n(y[:, raw_off[1]:raw_off[2]]))
            v.append(y[:, raw_off[2]:])
        yield
        qk_kk = []
        for i in range(n_units):
            kb = k[i].astype(BF16)
            qk_kk.append(lax.dot_general(jnp.concatenate([q[i].astype(BF16), kb], axis=0),
                                         jnp.concatenate([kb, kb], axis=0),
                                         (((1,), (1,)), ((), ())), preferred_element_type=F32))
        yield
        beta, gc, gr, _ = gate_terms(it, r0s)
        ring = unit_ring_rows(it)
        x, p = [], []
        for i in range(n_units):
            beta_p = jnp.where(lo, beta[i][0], beta[i][1])
            gc_p = jnp.where(lo, gc[i][0], gc[i][1])
            decay = jnp.exp(jnp.where(tril, gc_p - gr[i], -jnp.inf))
            a = jnp.where(strict, beta_p * qk_kk[i][C:] * decay, 0.0)
            qkd_ref[unit_heads[i], pl.ds(ring[i], C), :] = (qk_kk[i][:C] * decay).astype(BF16)
            x.append(eye - a)
            p.append(jnp.dot(a.astype(BF16), block_diag(a), preferred_element_type=F32))
        yield
        for _ in range(A1_LEVELS):
            neumann_level(p, x)
            yield
        for i in range(n_units):
            mid_ref[it % 2, i] = jnp.concatenate([p[i], x[i], q[i], k[i], v[i]], axis=1)

    def stage_a2(it):
        r0s = unit_rows(it)
        ring = unit_ring_rows(it)
        mid = [mid_ref[it % 2, i] for i in range(n_units)]
        p = [m[:, mid_cols["p"]] for m in mid]
        x = [m[:, mid_cols["x"]] for m in mid]
        for _ in range(n_levels - A1_LEVELS):
            neumann_level(p, x)
            yield
        beta, gc, _, g_last = gate_terms(it, r0s)
        sols = []
        for i in range(n_units):
            q, k, v = (mid[i][:, mid_cols[name]] for name in ("q", "k", "v"))
            eg = [jnp.exp(gc[i][j]) for j in range(V_PER_K)]
            rhs = jnp.concatenate(
                [jnp.concatenate([v[:, vslices[j]] * beta[i][j],
                                  k * (beta[i][j] * eg[j])], axis=1) for j in range(V_PER_K)],
                axis=0).astype(BF16)
            sols.append([jnp.dot(jnp.where(lane_sel[j], x[i], 0.0).astype(BF16), rhs,
                                 preferred_element_type=F32) for j in range(V_PER_K)])
            hh, r0 = unit_heads[i], ring[i]
            qe_ref[hh, pl.ds(r0, C), :] = jnp.concatenate(
                [q * eg[j] for j in range(V_PER_K)], axis=1).astype(BF16)
            kd_ref[hh, pl.ds(pl.multiple_of(V_PER_K * r0, V_PER_K * C), V_PER_K * C), :] = jnp.concatenate(
                [k * jnp.exp(g_last[i][j] - gc[i][j]) for j in range(V_PER_K)],
                axis=0).astype(BF16)
        yield
        for i in range(n_units):
            hh, r0 = unit_heads[i], ring[i]
            u_ref[hh, pl.ds(r0, C), :] = jnp.concatenate(
                [s[:, :DN_V_DIM] for s in sols[i]], axis=1)
            w_ref[hh, pl.ds(r0, C), :] = jnp.concatenate(
                [s[:, DN_V_DIM:] for s in sols[i]], axis=1).astype(BF16)

    def stage_b(it, cc):
        r0 = pl.multiple_of((it * DN_A_CHUNKS + cc) * C, C)
        rr = ring_row(it, cc)
        zero_blk = jnp.zeros((DN_K_DIM, DN_V_DIM), BF16)
        first = lax.broadcasted_iota(jnp.int32, (C, VW), 1) < DN_V_DIM
        ws_qs, v_new = [], []
        for hh in range(DN_HG):
            s0 = st_ref[hh * V_PER_K].astype(BF16)
            s1 = st_ref[hh * V_PER_K + 1].astype(BF16)
            s_bd = jnp.concatenate([jnp.concatenate([s0, zero_blk], axis=1),
                                    jnp.concatenate([zero_blk, s1], axis=1)], axis=0)
            ws_qs.append(jnp.dot(jnp.concatenate([w_ref[hh, pl.ds(rr, C), :],
                                                  qe_ref[hh, pl.ds(rr, C), :]], axis=0),
                                 s_bd, preferred_element_type=F32))
            v_new.append((u_ref[hh, pl.ds(rr, C), :] - ws_qs[hh][:C]).astype(BF16))
        yield
        v_bd = []
        for hh in range(DN_HG):
            zeros = jnp.zeros_like(v_new[hh])
            v_bd.append(jnp.concatenate([jnp.where(first, v_new[hh], zeros),
                                         jnp.where(first, zeros, v_new[hh])], axis=0))
            kd = kd_ref[hh, pl.ds(pl.multiple_of(V_PER_K * rr, V_PER_K * C), V_PER_K * C), :]
            kv = lax.dot_general(kd, v_bd[hh], (((0,), (0,)), ((), ())),
                                 preferred_element_type=F32)
            for j in range(V_PER_K):
                gl = GROUP_VH + hh * V_PER_K + j
                g_last = col_ref[0, 0, pl.ds(r0 + C - 1, 1), gl:gl + 1]
                st_ref[hh * V_PER_K + j] = (st_ref[hh * V_PER_K + j] * jnp.exp(g_last)
                                            + kv[:, vslices[j]])
        yield
        for hh in range(DN_HG):
            o_both = ws_qs[hh][C:] + jnp.dot(qkd_ref[hh, pl.ds(rr, C), :], v_bd[hh],
                                             preferred_element_type=F32)
            for j in range(V_PER_K):
                o = o_both[:, vslices[j]]
                on = o * lax.rsqrt(jnp.mean(o * o, axis=-1, keepdims=True) + EPS) * nw_ref[...]
                ocols = slice(hh * VW + j * DN_V_DIM, hh * VW + (j + 1) * DN_V_DIM)
                zz = z_ref[0, pl.ds(r0, C), ocols].astype(F32)
                o_ref[0, pl.ds(r0, C), ocols] = (on * _silu(zz)).astype(o_ref.dtype)

    def recurrence(it):
        for cc in range(DN_A_CHUNKS):
            yield from stage_b(it, cc)
            yield

    def emit(main, side=(), main_per_side=1):
        side = iter(side)
        for step, _ in enumerate(main):
            if step % main_per_side == main_per_side - 1:
                next(side, None)
        for _ in side:
            pass

    def stage_a(it):
        yield from stage_a1(it)
        yield
        yield from stage_a2(it)

    n_iter = DN_TS // (C * DN_A_CHUNKS)
    emit(stage_a(0))

    def body(it, carry):
        emit(stage_a(it + 1), recurrence(it))
        return carry

    lax.fori_loop(0, n_iter - 1, body, 0)
    emit(recurrence(n_iter - 1))


def _deltanet(proj3, colp, rowp, conv_w, dn_norm_w):
    b, s, _ = proj3.shape
    kw = DN_HG * DN_K_DIM
    vw = DN_HG * VW
    q0, k0, v0, z0 = P_DQ // kw, P_DK // kw, P_DV // vw, P_DZ // vw
    ck0, cv0 = DN_QK_W // kw, 2 * DN_QK_W // vw
    hb = DN_TS // HALO_BLK

    def halo(t):
        return jnp.maximum(t * hb - 1, 0)

    return pl.pallas_call(
        _dnet_kernel,
        grid=(b, DN_K_HEADS // DN_HG, s // DN_TS),
        in_specs=[
            pl.BlockSpec((1, DN_TS, kw), lambda i, g, t: (i, t, q0 + g)),
            pl.BlockSpec((1, DN_TS, kw), lambda i, g, t: (i, t, k0 + g)),
            pl.BlockSpec((1, DN_TS, vw), lambda i, g, t: (i, t, v0 + g)),
            pl.BlockSpec((1, DN_TS, vw), lambda i, g, t: (i, t, z0 + g)),
            pl.BlockSpec((1, HALO_BLK, kw), lambda i, g, t: (i, halo(t), q0 + g)),
            pl.BlockSpec((1, HALO_BLK, kw), lambda i, g, t: (i, halo(t), k0 + g)),
            pl.BlockSpec((1, HALO_BLK, vw), lambda i, g, t: (i, halo(t), v0 + g)),
            pl.BlockSpec((1, 1, DN_TS, SMALL_W), lambda i, g, t: (i, g, t, 0)),
            pl.BlockSpec((1, DN_TS // GATE_ROWS, GROUP_VH, GATE_ROWS), lambda i, g, t: (i, t, g, 0)),
            pl.BlockSpec((CONV_WIDTH, kw), lambda i, g, t: (0, g)),
            pl.BlockSpec((CONV_WIDTH, kw), lambda i, g, t: (0, ck0 + g)),
            pl.BlockSpec((CONV_WIDTH, vw), lambda i, g, t: (0, cv0 + g)),
            pl.BlockSpec((1, DN_V_DIM), lambda i, g, t: (0, 0)),
        ],
        out_specs=pl.BlockSpec((1, DN_TS, vw), lambda i, g, t: (i, t, g)),
        out_shape=jax.ShapeDtypeStruct((b, s, DN_V_W), BF16),
        scratch_shapes=[
            pltpu.VMEM((DN_HG * V_PER_K, DN_K_DIM, DN_V_DIM), F32),
            pltpu.VMEM((DN_HG, RING * PAIR_ROWS, VW), F32),
            pltpu.VMEM((DN_HG, RING * PAIR_ROWS, VW), BF16),
            pltpu.VMEM((DN_HG, RING * PAIR_ROWS, VW), BF16),
            pltpu.VMEM((DN_HG, V_PER_K * RING * PAIR_ROWS, DN_K_DIM), BF16),
            pltpu.VMEM((DN_HG, RING * PAIR_ROWS, V_PER_K * CHUNK), BF16),
            pltpu.VMEM((CHUNK + DN_TS, DN_HG * RAW_W), BF16),
            pltpu.VMEM((2, DN_A_CHUNKS * DN_HG, CHUNK, 4 * CHUNK + 2 * DN_K_DIM + VW), F32),
        ],
        compiler_params=pltpu.CompilerParams(
            dimension_semantics=("parallel", "parallel", "arbitrary"),
            vmem_limit_bytes=V7X_VMEM_LIMIT),
        name="dnet",
    )(proj3, proj3, proj3, proj3, proj3, proj3, proj3, colp, rowp, conv_w, conv_w, conv_w,
      dn_norm_w)


MERGE_TM = 512
MERGE_TN = 1024


def _merge_kernel(oa_ref, od_ref, wa_ref, wd_ref, ga_ref, gd_ref, o_ref):
    ya = jnp.dot(oa_ref[...], wa_ref[...], preferred_element_type=F32)
    yd = jnp.dot(od_ref[...], wd_ref[...], preferred_element_type=F32)
    ga = _sigmoid(ga_ref[...].astype(F32))
    gd = _sigmoid(gd_ref[...].astype(F32))
    o_ref[...] = (ga * ya + gd * yd).astype(o_ref.dtype)


def _merge(oa, od, wa, wd, proj2):
    m = oa.shape[0]
    g0 = P_G // MERGE_TN
    g1 = (P_G + D_MODEL) // MERGE_TN
    return pl.pallas_call(
        _merge_kernel,
        grid=(m // MERGE_TM, D_MODEL // MERGE_TN),
        in_specs=[
            pl.BlockSpec((MERGE_TM, ATT_W), lambda i, j: (i, 0)),
            pl.BlockSpec((MERGE_TM, DN_V_W), lambda i, j: (i, 0)),
            pl.BlockSpec((ATT_W, MERGE_TN), lambda i, j: (0, j)),
            pl.BlockSpec((DN_V_W, MERGE_TN), lambda i, j: (0, j)),
            pl.BlockSpec((MERGE_TM, MERGE_TN), lambda i, j: (i, g0 + j)),
            pl.BlockSpec((MERGE_TM, MERGE_TN), lambda i, j: (i, g1 + j)),
        ],
        out_specs=pl.BlockSpec((MERGE_TM, MERGE_TN), lambda i, j: (i, j)),
        out_shape=jax.ShapeDtypeStruct((m, D_MODEL), BF16),
        compiler_params=pltpu.CompilerParams(
            dimension_semantics=("parallel", "parallel"),
            vmem_limit_bytes=V7X_VMEM_LIMIT),
        name="merge",
    )(oa, od, wa, wd, proj2, proj2)


OUT_TM = 512


def _out_kernel(m_ref, w_ref, x_ref, nw_ref, o_ref):
    y = x_ref[...] + jnp.dot(m_ref[...], w_ref[...], preferred_element_type=F32)
    ms = jnp.mean(y * y, axis=-1, keepdims=True)
    o_ref[...] = (y * lax.rsqrt(ms + EPS)) * nw_ref[...]


def _outproj(merged, w_out, x2, final_norm_w):
    m = x2.shape[0]
    return pl.pallas_call(
        _out_kernel,
        grid=(m // OUT_TM,),
        in_specs=[
            pl.BlockSpec((OUT_TM, D_MODEL), lambda i: (i, 0)),
            pl.BlockSpec((D_MODEL, D_MODEL), lambda i: (0, 0)),
            pl.BlockSpec((OUT_TM, D_MODEL), lambda i: (i, 0)),
            pl.BlockSpec((1, D_MODEL), lambda i: (0, 0)),
        ],
        out_specs=pl.BlockSpec((OUT_TM, D_MODEL), lambda i: (i, 0)),
        out_shape=jax.ShapeDtypeStruct((m, D_MODEL), F32),
        compiler_params=pltpu.CompilerParams(
            dimension_semantics=("parallel",),
            vmem_limit_bytes=V7X_VMEM_LIMIT),
        name="outproj",
    )(merged, w_out, x2, final_norm_w)


def _layer(x, norm_w, w_in, b_qkv, sinks, conv_w, a_log, dt_bias, dn_norm_w,
           w_att_branch, w_dn_branch, w_out, out_norm_w):
    b, s, d = x.shape
    m = b * s
    x2 = x.reshape(m, d)

    w_t = w_in.T
    wk = w_t[OFF_AK:OFF_AV].reshape(N_KV_HEADS, 1, HEAD_DIM, d)
    wv = w_t[OFF_AV:OFF_AZ].reshape(N_KV_HEADS, 1, HEAD_DIM, d)
    w_kv_t = jnp.concatenate([wk, wv], axis=1).reshape(2 * ATT_KV_W, d)
    w_small_t = jnp.pad(w_t[OFF_DB:OFF_G], ((0, SMALL_W - 2 * DN_V_HEADS), (0, 0)))
    bk = b_qkv[ATT_Q_W:ATT_Q_W + ATT_KV_W].reshape(N_KV_HEADS, 1, HEAD_DIM)
    bv = b_qkv[ATT_Q_W + ATT_KV_W:].reshape(N_KV_HEADS, 1, HEAD_DIM)
    bias = jnp.concatenate([b_qkv[:ATT_Q_W], jnp.zeros((P_KV - P_AZ,), F32),
                            jnp.concatenate([bk, bv], axis=1).reshape(-1)]).reshape(1, MAIN_W)

    proj, small = _inproj(x2, norm_w.reshape(1, d), w_t, w_kv_t, bias, w_small_t)
    proj3 = proj.reshape(b, s, MAIN_W)

    lane_pad = (DN_V_HEADS, SMALL_W - 2 * DN_V_HEADS)
    alog_row = jnp.pad(a_log, lane_pad).reshape(1, SMALL_W)
    dt_row = jnp.pad(dt_bias, lane_pad).reshape(1, SMALL_W)
    colp, rowp = _gates(small.reshape(b, s, SMALL_W), alog_row, dt_row)

    o_att = _attention(proj3, sinks)
    o_dn = _deltanet(proj3, colp, rowp, conv_w, dn_norm_w.reshape(1, DN_V_DIM))

    merged = _merge(o_att.reshape(m, ATT_W), o_dn.reshape(m, DN_V_W),
                    w_att_branch.astype(BF16), w_dn_branch.astype(BF16), proj)
    y = _outproj(merged, w_out.astype(BF16), x2, out_norm_w.reshape(1, d))
    return y.reshape(b, s, d)


def kernel(x, norm_w, w_in, b_qkv, sinks, conv_w, a_log, dt_bias, dn_norm_w,
           w_att_branch, w_dn_branch, w_out, final_norm_w):
    depth = norm_w.shape[0]
    assert depth == 1, "the final RMSNorm is fused into the single layer's output kernel"
    return _layer(x, norm_w[0], w_in[0], b_qkv[0], sinks[0], conv_w[0], a_log[0], dt_bias[0],
                  dn_norm_w[0], w_att_branch[0], w_dn_branch[0], w_out[0], final_norm_w)
```

```python
import jax
import jax.numpy as jnp
from jax import lax
from jax.experimental import pallas as pl
from jax.experimental.pallas import tpu as pltpu

F32 = jnp.float32
BF16 = jnp.bfloat16

D_MODEL = 2048
HEAD_DIM = 64
N_Q_HEADS = 32
N_KV_HEADS = 4
GROUP = N_Q_HEADS // N_KV_HEADS
WINDOW = 128
ATT_BLOCK = 128
ATT_Q_W = N_Q_HEADS * HEAD_DIM
ATT_KV_W = N_KV_HEADS * HEAD_DIM
ATT_W = ATT_Q_W

DN_K_HEADS = 16
DN_V_HEADS = 32
DN_K_DIM = 128
DN_V_DIM = 128
DN_QK_W = DN_K_HEADS * DN_K_DIM
DN_V_W = DN_V_HEADS * DN_V_DIM
DN_CONV_CH = 2 * DN_QK_W + DN_V_W
CONV_WIDTH = 4
CHUNK = 64
EPS = 1e-6

OFF_AQ = 0
OFF_AK = OFF_AQ + ATT_Q_W
OFF_AV = OFF_AK + ATT_KV_W
OFF_AZ = OFF_AV + ATT_KV_W
OFF_DQKV = OFF_AZ + ATT_W
OFF_DZ = OFF_DQKV + DN_CONV_CH
OFF_DB = OFF_DZ + DN_V_W
OFF_DA = OFF_DB + DN_V_HEADS
OFF_G = OFF_DA + DN_V_HEADS
IN_W = OFF_G + 2 * D_MODEL

MAIN_W = IN_W - 2 * DN_V_HEADS
P_AQ = 0
P_AZ = P_AQ + ATT_Q_W
P_DQ = P_AZ + ATT_W
P_DK = P_DQ + DN_QK_W
P_DV = P_DK + DN_QK_W
P_DZ = P_DV + DN_V_W
P_G = P_DZ + DN_V_W
P_KV = P_G + 2 * D_MODEL
SMALL_W = 128

V7X_VMEM_LIMIT = 56 * 1024 * 1024


def _sigmoid(v):
    return 0.5 * jnp.tanh(0.5 * v) + 0.5


def _silu(v):
    h = 0.5 * v
    return h * jnp.tanh(h) + h


INPROJ_TM = 2048
INPROJ_TN = 512
NORM_ROWS = 128


NT_DIMS = (((1,), (1,)), ((), ()))


def _inproj_kernel(x_ref, nw_ref, wt_ref, wkv_ref, b_ref, ws_ref, o_ref, os_ref, h_ref):
    j = pl.program_id(1)

    @pl.when(j == 0)
    def _():
        def body(r, carry):
            r0 = pl.multiple_of(r * NORM_ROWS, NORM_ROWS)
            xv = x_ref[pl.ds(r0, NORM_ROWS), :]
            ms = jnp.mean(xv * xv, axis=-1, keepdims=True)
            hv = (xv * lax.rsqrt(ms + EPS)) * nw_ref[...]
            h_ref[pl.ds(r0, NORM_ROWS), :] = hv.astype(BF16)
            return carry
        lax.fori_loop(0, INPROJ_TM // NORM_ROWS, body, 0)
        os_ref[...] = lax.dot_general(h_ref[...], ws_ref[...].astype(BF16), NT_DIMS,
                                      preferred_element_type=F32)

    def project(w_ref):
        acc = lax.dot_general(h_ref[...], w_ref[...].astype(BF16), NT_DIMS,
                              preferred_element_type=F32)
        o_ref[...] = (acc + b_ref[...]).astype(o_ref.dtype)

    @pl.when(j < DIRECT_BLOCKS)
    def _():
        project(wt_ref)

    @pl.when(j >= DIRECT_BLOCKS)
    def _():
        project(wkv_ref)


DIRECT_BLOCKS = P_KV // INPROJ_TN


def _inproj(x2, norm_w, w_t, w_kv_t, bias, w_small_t):
    m = x2.shape[0]
    grid = (m // INPROJ_TM, MAIN_W // INPROJ_TN)
    n_q, n_mid = ATT_Q_W // INPROJ_TN, P_G // INPROJ_TN

    def direct_rows(i, j):
        jj = jnp.minimum(j, DIRECT_BLOCKS - 1)
        row = jnp.where(jj < n_q, OFF_AQ + INPROJ_TN * jj,
                        jnp.where(jj < n_mid, OFF_AZ + INPROJ_TN * (jj - n_q),
                                  OFF_G + INPROJ_TN * (jj - n_mid)))
        return pl.multiple_of(row, HEAD_DIM), 0

    return pl.pallas_call(
        _inproj_kernel,
        grid=grid,
        in_specs=[
            pl.BlockSpec((INPROJ_TM, D_MODEL), lambda i, j: (i, 0), pipeline_mode=pl.Buffered(1)),
            pl.BlockSpec((1, D_MODEL), lambda i, j: (0, 0)),
            pl.BlockSpec((pl.Element(INPROJ_TN), pl.Element(D_MODEL)), direct_rows),
            pl.BlockSpec((INPROJ_TN, D_MODEL), lambda i, j: (0, 0), pipeline_mode=pl.Buffered(1)),
            pl.BlockSpec((1, INPROJ_TN), lambda i, j: (0, j)),
            pl.BlockSpec((SMALL_W, D_MODEL), lambda i, j: (0, 0), pipeline_mode=pl.Buffered(1)),
        ],
        out_specs=[
            pl.BlockSpec((INPROJ_TM, INPROJ_TN), lambda i, j: (i, j)),
            pl.BlockSpec((INPROJ_TM, SMALL_W), lambda i, j: (i, 0)),
        ],
        out_shape=[
            jax.ShapeDtypeStruct((m, MAIN_W), BF16),
            jax.ShapeDtypeStruct((m, SMALL_W), F32),
        ],
        scratch_shapes=[pltpu.VMEM((INPROJ_TM, D_MODEL), BF16)],
        compiler_params=pltpu.CompilerParams(
            dimension_semantics=("parallel", "arbitrary"),
            vmem_limit_bytes=V7X_VMEM_LIMIT),
        name="inproj",
    )(x2, norm_w, w_t, w_kv_t, bias, w_small_t)


GATE_ROWS = 2 * CHUNK
DN_HG = 4
GATE_GROUPS = DN_K_HEADS // DN_HG
GROUP_VH = DN_V_HEADS // GATE_GROUPS


def _gates_kernel(s_ref, alog_ref, dt_ref, col_ref, row_ref):
    seq = s_ref.shape[1]
    row = lax.broadcasted_iota(jnp.int32, (CHUNK, SMALL_W), 0)
    lane = lax.broadcasted_iota(jnp.int32, (GATE_ROWS, SMALL_W), 1)

    def body(blk, carry):
        r0 = pl.multiple_of(blk * GATE_ROWS, GATE_ROWS)
        xs = s_ref[0, pl.ds(r0, GATE_ROWS), :]
        beta = _sigmoid(xs)
        t = xs + dt_ref[...]
        softplus = jnp.maximum(t, 0.0) + jnp.log(1.0 + jnp.exp(-jnp.abs(t)))
        g = -jnp.exp(alog_ref[...]) * softplus
        halves = []
        for part in range(GATE_ROWS // CHUNK):
            gp = g[part * CHUNK:(part + 1) * CHUNK]
            shift = 1
            while shift < CHUNK:
                gp = gp + jnp.where(row >= shift, pltpu.roll(gp, shift, axis=0), 0.0)
                shift *= 2
            halves.append(gp)
        gcum = jnp.concatenate(halves, axis=0)
        for grp in range(GATE_GROUPS):
            b_g = pltpu.roll(beta, (SMALL_W - grp * GROUP_VH) % SMALL_W, axis=1)
            g_g = pltpu.roll(gcum, SMALL_W - DN_V_HEADS - grp * GROUP_VH + GROUP_VH, axis=1)
            col_ref[0, grp, pl.ds(r0, GATE_ROWS), :] = jnp.where(lane < GROUP_VH, b_g, g_g)
        row_ref[0, blk] = gcum.T[DN_V_HEADS:2 * DN_V_HEADS]
        return carry

    lax.fori_loop(0, seq // GATE_ROWS, body, 0)


def _gates(small3, alog_row, dt_row):
    b, s, _ = small3.shape
    return pl.pallas_call(
        _gates_kernel,
        grid=(b,),
        in_specs=[
            pl.BlockSpec((1, s, SMALL_W), lambda i: (i, 0, 0)),
            pl.BlockSpec((1, SMALL_W), lambda i: (0, 0)),
            pl.BlockSpec((1, SMALL_W), lambda i: (0, 0)),
        ],
        out_specs=[
            pl.BlockSpec((1, GATE_GROUPS, s, SMALL_W), lambda i: (i, 0, 0, 0)),
            pl.BlockSpec((1, s // GATE_ROWS, DN_V_HEADS, GATE_ROWS), lambda i: (i, 0, 0, 0)),
        ],
        out_shape=[
            jax.ShapeDtypeStruct((b, GATE_GROUPS, s, SMALL_W), F32),
            jax.ShapeDtypeStruct((b, s // GATE_ROWS, DN_V_HEADS, GATE_ROWS), F32),
        ],
        compiler_params=pltpu.CompilerParams(dimension_semantics=("parallel",)),
        name="gates",
    )(small3, alog_row, dt_row)


PAIR_W = 2 * HEAD_DIM
KVH_W = GROUP * HEAD_DIM


def _attn_kernel(sinks_ref, q_ref, kvp_ref, kvc_ref, z_ref, o_ref):
    n = pl.program_id(1)
    L = ATT_BLOCK
    lane = lax.broadcasted_iota(jnp.int32, (L, PAIR_W), 1)
    lo = lane < HEAD_DIM

    def prep(kv):
        kv = kv.astype(F32)
        sw = pltpu.roll(kv, HEAD_DIM, axis=1)
        kk = jnp.where(lo, kv, sw).astype(BF16)
        va = jnp.where(lo, sw, 0.0).astype(BF16)
        vb = jnp.where(lo, 0.0, kv).astype(BF16)
        return kk, va, vb

    assert WINDOW == ATT_BLOCK
    qi = lax.broadcasted_iota(jnp.int32, (L, L), 0)
    li = lax.broadcasted_iota(jnp.int32, (L, L), 1)
    from_prev = li > qi
    prev_bias = jnp.where(n > 0, 0.0, -jnp.inf)

    def kv_head(h):
        kcols = slice(h * PAIR_W, (h + 1) * PAIR_W)
        kk_p, va_p, vb_p = prep(kvp_ref[0, :, kcols])
        kk_c, va_c, vb_c = prep(kvc_ref[0, :, kcols])
        kk = jnp.concatenate([kk_p, kk_c], axis=0)
        va = jnp.concatenate([va_p, va_c], axis=0)
        vb = jnp.concatenate([vb_p, vb_c], axis=0)
        pair_cols = [slice(h * KVH_W + pr * PAIR_W, h * KVH_W + (pr + 1) * PAIR_W)
                     for pr in range(GROUP // 2)]
        scores = []
        for cols in pair_cols:
            q2 = q_ref[0, :, cols].astype(F32) * (HEAD_DIM ** -0.5)
            for qm in (jnp.where(lo, q2, 0.0), jnp.where(lo, 0.0, q2)):
                scores.append(lax.dot_general(qm.astype(BF16), kk, (((1,), (1,)), ((), ())),
                                              preferred_element_type=F32))
        probs, inv = [], []
        for i, s in enumerate(scores):
            sink = sinks_ref[h * GROUP + i]
            s = jnp.where(from_prev, s[:, :L] + prev_bias, s[:, L:])
            m = jnp.maximum(jnp.max(s, axis=-1, keepdims=True), sink)
            p = jnp.exp(s - m)
            inv.append(1.0 / (jnp.sum(p, axis=-1, keepdims=True) + jnp.exp(sink - m)))
            probs.append(jnp.concatenate([jnp.where(from_prev, p, 0.0),
                                          jnp.where(from_prev, 0.0, p)], axis=1).astype(BF16))
        for pr, cols in enumerate(pair_cols):
            o2 = (jnp.dot(probs[2 * pr], va, preferred_element_type=F32)
                  + jnp.dot(probs[2 * pr + 1], vb, preferred_element_type=F32))
            o2 = o2 * jnp.where(lo, inv[2 * pr], inv[2 * pr + 1])
            zz = z_ref[0, :, cols].astype(F32)
            o_ref[0, :, cols] = (o2 * _silu(zz)).astype(o_ref.dtype)

    for h in range(N_KV_HEADS):
        kv_head(h)


def _attention(proj3, sinks):
    b, s, _ = proj3.shape
    nb = s // ATT_BLOCK
    kvw = 2 * ATT_KV_W
    kv0 = P_KV // kvw
    z0 = P_AZ // ATT_W
    return pl.pallas_call(
        _attn_kernel,
        grid=(b, nb),
        in_specs=[
            pl.BlockSpec(memory_space=pltpu.SMEM),
            pl.BlockSpec((1, ATT_BLOCK, ATT_Q_W), lambda i, n: (i, n, 0)),
            pl.BlockSpec((1, ATT_BLOCK, kvw), lambda i, n: (i, jnp.maximum(n - 1, 0), kv0)),
            pl.BlockSpec((1, ATT_BLOCK, kvw), lambda i, n: (i, n, kv0)),
            pl.BlockSpec((1, ATT_BLOCK, ATT_W), lambda i, n: (i, n, z0)),
        ],
        out_specs=pl.BlockSpec((1, ATT_BLOCK, ATT_W), lambda i, n: (i, n, 0)),
        out_shape=jax.ShapeDtypeStruct((b, s, ATT_W), BF16),
        compiler_params=pltpu.CompilerParams(
            dimension_semantics=("parallel", "parallel")),
        name="attn",
    )(sinks, proj3, proj3, proj3, proj3)


V_PER_K = DN_V_HEADS // DN_K_HEADS
assert V_PER_K == 2 and V_PER_K * CHUNK == 128, "two value heads are packed side by side in 128 lanes"
VW = V_PER_K * DN_V_DIM
DN_TS = 2048
DN_A_CHUNKS = 2
assert DN_A_CHUNKS % (GATE_ROWS // CHUNK) == 0
PAIR_ROWS = DN_A_CHUNKS * CHUNK
RING = 2
A1_LEVELS = 3
RAW_W = 2 * DN_K_DIM + VW
HALO_BLK = 16


def _dnet_kernel(q_ref, k_ref, v_ref, z_ref, qh_ref, kh_ref, vh_ref, col_ref, row_ref,
                 cwq_ref, cwk_ref, cwv_ref, nw_ref, o_ref,
                 st_ref, u_ref, w_ref, qe_ref, kd_ref, qkd_ref, raw_ref, mid_ref):
    t = pl.program_id(2)
    C = CHUNK
    ri = lax.broadcasted_iota(jnp.int32, (C, 2 * C), 0)
    li = lax.broadcasted_iota(jnp.int32, (C, 2 * C), 1)
    lo = li < C
    ci = jnp.where(lo, li, li - C)
    tril = ri >= ci
    strict = ri > ci
    eye = (ri == ci).astype(F32)
    top = lax.broadcasted_iota(jnp.int32, (2 * C, 2 * C), 0) < C
    left = lax.broadcasted_iota(jnp.int32, (2 * C, 2 * C), 1) < C
    diag_blk = top == left

    @pl.when(t == 0)
    def _():
        st_ref[...] = jnp.zeros_like(st_ref)

    def head_cols(ref, hh):
        per_head = ref.shape[2] // DN_HG
        return slice(hh * per_head, (hh + 1) * per_head)

    raw_off = (0, DN_K_DIM, 2 * DN_K_DIM)
    raw_ref[0:C - HALO_BLK, :] = jnp.zeros((C - HALO_BLK, raw_ref.shape[1]), BF16)
    for ref, halo_ref, off in zip((q_ref, k_ref, v_ref), (qh_ref, kh_ref, vh_ref), raw_off):
        tail = halo_ref[0]
        tail = jnp.where(t > 0, tail, jnp.zeros_like(tail))
        for hh in range(DN_HG):
            hc = head_cols(ref, hh)
            raw_ref[C - HALO_BLK:C, hh * RAW_W + off:hh * RAW_W + off + hc.stop - hc.start] = tail[:, hc]

    def stage_rows(r, carry):
        r0 = pl.multiple_of(r * C, C)
        for ref, off in zip((q_ref, k_ref, v_ref), raw_off):
            for hh in range(DN_HG):
                hc = head_cols(ref, hh)
                raw_ref[pl.ds(r0 + C, C), hh * RAW_W + off:hh * RAW_W + off + hc.stop - hc.start] = (
                    ref[0, pl.ds(r0, C), hc])
        return carry

    lax.fori_loop(0, DN_TS // C, stage_rows, 0)

    sr = lax.broadcasted_iota(jnp.int32, ((CONV_WIDTH - 1) * C, 2 * C), 0)
    sc = lax.broadcasted_iota(jnp.int32, ((CONV_WIDTH - 1) * C, 2 * C), 1)
    shift_mat = (sc == C + jnp.bitwise_and(sr, C - 1)
                 - (jnp.right_shift(sr, C.bit_length() - 1) + 1)).astype(BF16)

    def conv_silu(hh, r0):
        rcols = slice(hh * RAW_W, (hh + 1) * RAW_W)
        w = jnp.concatenate([cwq_ref[:, head_cols(q_ref, hh)], cwk_ref[:, head_cols(k_ref, hh)],
                             cwv_ref[:, head_cols(v_ref, hh)]], axis=1)
        window = raw_ref[pl.ds(r0, 2 * C), rcols]
        taps = jnp.dot(shift_mat, window, preferred_element_type=F32)
        y = window[C:].astype(F32) * w[CONV_WIDTH - 1:CONV_WIDTH, :]
        for s in range(1, CONV_WIDTH):
            y = y + taps[(s - 1) * C:s * C] * w[CONV_WIDTH - 1 - s:CONV_WIDTH - s, :]
        return _silu(y)

    def l2n(v):
        return v * lax.rsqrt(jnp.sum(v * v, axis=-1, keepdims=True) + EPS)

    def bcast(col):
        return jnp.broadcast_to(col, (C, 2 * C))

    def block_diag(m):
        return jnp.where(diag_blk, jnp.concatenate([m, m], axis=0), 0.0).astype(BF16)

    vslices = [slice(j * DN_V_DIM, (j + 1) * DN_V_DIM) for j in range(V_PER_K)]
    lane_sel = [lo, jnp.logical_not(lo)]


    per_tile = GATE_ROWS // C
    unit_heads = [hh for _ in range(DN_A_CHUNKS) for hh in range(DN_HG)]
    unit_halves = [cc % per_tile for cc in range(DN_A_CHUNKS) for _ in range(DN_HG)]
    unit_tiles = [cc // per_tile for cc in range(DN_A_CHUNKS) for _ in range(DN_HG)]
    n_units = len(unit_heads)
    mid_cols, offset = {}, 0
    for name, width in (("p", 2 * C), ("x", 2 * C), ("q", DN_K_DIM), ("k", DN_K_DIM), ("v", VW)):
        mid_cols[name] = slice(offset, offset + width)
        offset += width

    def unit_rows(it):
        return [pl.multiple_of((it * DN_A_CHUNKS + cc) * C, C)
                for cc in range(DN_A_CHUNKS) for _ in range(DN_HG)]

    def ring_row(it, cc):
        return pl.multiple_of((it % RING) * PAIR_ROWS + cc * C, C)

    def unit_ring_rows(it):
        return [ring_row(it, cc) for cc in range(DN_A_CHUNKS) for _ in range(DN_HG)]

    def gate_terms(it, r0s):
        g_rows = [row_ref[0, it * (DN_A_CHUNKS // per_tile) + r] for r in range(DN_A_CHUNKS // per_tile)]
        beta, gc, gr, g_last = [], [], [], []
        for i in range(n_units):
            colp = col_ref[0, 0, pl.ds(r0s[i], C), :]
            vh = [unit_heads[i] * V_PER_K + j for j in range(V_PER_K)]
            beta.append([bcast(colp[:, h:h + 1]) for h in vh])
            gc.append([bcast(colp[:, GROUP_VH + h:GROUP_VH + h + 1]) for h in vh])
            tok = slice(unit_halves[i] * C, (unit_halves[i] + 1) * C)
            gr.append(jnp.concatenate([g_rows[unit_tiles[i]][h:h + 1, tok] for h in vh], axis=1))
            g_last.append([gr[i][:, (j + 1) * C - 1:(j + 1) * C] for j in range(V_PER_K)])
        return beta, gc, gr, g_last

    def neumann_level(p, x):
        for i in range(n_units):
            rhs = jnp.concatenate([block_diag(p[i]), block_diag(x[i])], axis=1)
            pp_px = jnp.dot(p[i].astype(BF16), rhs, preferred_element_type=F32)
            x[i] = x[i] + pp_px[:, 2 * C:]
            p[i] = pp_px[:, :2 * C]

    n_levels = C.bit_length() - 2

    def stage_a1(it):
        r0s = unit_rows(it)
        q, k, v = [], [], []
        for i in range(n_units):
            y = conv_silu(unit_heads[i], r0s[i])
            q.append(l2n(y[:, raw_off[0]:raw_off[1]]) * (DN_K_DIM ** -0.5))
            k.append(l2n(y[:, raw_off[1]:raw_off[2]]))
            v.append(y[:, raw_off[2]:])
        yield
        qk_kk = []
        for i in range(n_units):
            kb = k[i].astype(BF16)
            qk_kk.append(lax.dot_general(jnp.concatenate([q[i].astype(BF16), kb], axis=0),
                                         jnp.concatenate([kb, kb], axis=0),
                                         (((1,), (1,)), ((), ())), preferred_element_type=F32))
        yield
        beta, gc, gr, _ = gate_terms(it, r0s)
        ring = unit_ring_rows(it)
        x, p = [], []
        for i in range(n_units):
            beta_p = jnp.where(lo, beta[i][0], beta[i][1])
            gc_p = jnp.where(lo, gc[i][0], gc[i][1])
            decay = jnp.exp(jnp.where(tril, gc_p - gr[i], -jnp.inf))
            a = jnp.where(strict, beta_p * qk_kk[i][C:] * decay, 0.0)
            qkd_ref[unit_heads[i], pl.ds(ring[i], C), :] = (qk_kk[i][:C] * decay).astype(BF16)
            x.append(eye - a)
            p.append(jnp.dot(a.astype(BF16), block_diag(a), preferred_element_type=F32))
        yield
        for _ in range(A1_LEVELS):
            neumann_level(p, x)
            yield
        for i in range(n_units):
            mid_ref[it % 2, i] = jnp.concatenate([p[i], x[i], q[i], k[i], v[i]], axis=1)

    def stage_a2(it):
        r0s = unit_rows(it)
        ring = unit_ring_rows(it)
        mid = [mid_ref[it % 2, i] for i in range(n_units)]
        p = [m[:, mid_cols["p"]] for m in mid]
        x = [m[:, mid_cols["x"]] for m in mid]
        for _ in range(n_levels - A1_LEVELS):
            neumann_level(p, x)
            yield
        beta, gc, _, g_last = gate_terms(it, r0s)
        sols = []
        for i in range(n_units):
            q, k, v = (mid[i][:, mid_cols[name]] for name in ("q", "k", "v"))
            eg = [jnp.exp(gc[i][j]) for j in range(V_PER_K)]
            rhs = jnp.concatenate(
                [jnp.concatenate([v[:, vslices[j]] * beta[i][j],
                                  k * (beta[i][j] * eg[j])], axis=1) for j in range(V_PER_K)],
                axis=0).astype(BF16)
            sols.append([jnp.dot(jnp.where(lane_sel[j], x[i], 0.0).astype(BF16), rhs,
                                 preferred_element_type=F32) for j in range(V_PER_K)])
            hh, r0 = unit_heads[i], ring[i]
            qe_ref[hh, pl.ds(r0, C), :] = jnp.concatenate(
                [q * eg[j] for j in range(V_PER_K)], axis=1).astype(BF16)
            kd_ref[hh, pl.ds(pl.multiple_of(V_PER_K * r0, V_PER_K * C), V_PER_K * C), :] = jnp.concatenate(
                [k * jnp.exp(g_last[i][j] - gc[i][j]) for j in range(V_PER_K)],
                axis=0).astype(BF16)
        yield
        for i in range(n_units):
            hh, r0 = unit_heads[i], ring[i]
            u_ref[hh, pl.ds(r0, C), :] = jnp.concatenate(
                [s[:, :DN_V_DIM] for s in sols[i]], axis=1)
            w_ref[hh, pl.ds(r0, C), :] = jnp.concatenate(
                [s[:, DN_V_DIM:] for s in sols[i]], axis=1).astype(BF16)

    def stage_b(it, cc):
        r0 = pl.multiple_of((it * DN_A_CHUNKS + cc) * C, C)
        rr = ring_row(it, cc)
        zero_blk = jnp.zeros((DN_K_DIM, DN_V_DIM), BF16)
        first = lax.broadcasted_iota(jnp.int32, (C, VW), 1) < DN_V_DIM
        ws_qs, v_new = [], []
        for hh in range(DN_HG):
            s0 = st_ref[hh * V_PER_K].astype(BF16)
            s1 = st_ref[hh * V_PER_K + 1].astype(BF16)
            s_bd = jnp.concatenate([jnp.concatenate([s0, zero_blk], axis=1),
                                    jnp.concatenate([zero_blk, s1], axis=1)], axis=0)
            ws_qs.append(jnp.dot(jnp.concatenate([w_ref[hh, pl.ds(rr, C), :],
                                                  qe_ref[hh, pl.ds(rr, C), :]], axis=0),
                                 s_bd, preferred_element_type=F32))
            v_new.append((u_ref[hh, pl.ds(rr, C), :] - ws_qs[hh][:C]).astype(BF16))
        yield
        v_bd = []
        for hh in range(DN_HG):
            zeros = jnp.zeros_like(v_new[hh])
            v_bd.append(jnp.concatenate([jnp.where(first, v_new[hh], zeros),
                                         jnp.where(first, zeros, v_new[hh])], axis=0))
            kd = kd_ref[hh, pl.ds(pl.multiple_of(V_PER_K * rr, V_PER_K * C), V_PER_K * C), :]
            kv = lax.dot_general(kd, v_bd[hh], (((0,), (0,)), ((), ())),
                                 preferred_element_type=F32)
            for j in range(V_PER_K):
                gl = GROUP_VH + hh * V_PER_K + j
                g_last = col_ref[0, 0, pl.ds(r0 + C - 1, 1), gl:gl + 1]
                st_ref[hh * V_PER_K + j] = (st_ref[hh * V_PER_K + j] * jnp.exp(g_last)
                                            + kv[:, vslices[j]])
        yield
        for hh in range(DN_HG):
            o_both = ws_qs[hh][C:] + jnp.dot(qkd_ref[hh, pl.ds(rr, C), :], v_bd[hh],
                                             preferred_element_type=F32)
            for j in range(V_PER_K):
                o = o_both[:, vslices[j]]
                on = o * lax.rsqrt(jnp.mean(o * o, axis=-1, keepdims=True) + EPS) * nw_ref[...]
                ocols = slice(hh * VW + j * DN_V_DIM, hh * VW + (j + 1) * DN_V_DIM)
                zz = z_ref[0, pl.ds(r0, C), ocols].astype(F32)
                o_ref[0, pl.ds(r0, C), ocols] = (on * _silu(zz)).astype(o_ref.dtype)

    def recurrence(it):
        for cc in range(DN_A_CHUNKS):
            yield from stage_b(it, cc)
            yield

    def emit(main, side=(), main_per_side=1):
        side = iter(side)
        for step, _ in enumerate(main):
            if step % main_per_side == main_per_side - 1:
                next(side, None)
        for _ in side:
            pass

    def stage_a(it):
        yield from stage_a1(it)
        yield
        yield from stage_a2(it)

    n_iter = DN_TS // (C * DN_A_CHUNKS)
    emit(stage_a(0))

    def body(it, carry):
        emit(stage_a(it + 1), recurrence(it))
        return carry

    lax.fori_loop(0, n_iter - 1, body, 0)
    emit(recurrence(n_iter - 1))


def _deltanet(proj3, colp, rowp, conv_w, dn_norm_w):
    b, s, _ = proj3.shape
    kw = DN_HG * DN_K_DIM
    vw = DN_HG * VW
    q0, k0, v0, z0 = P_DQ // kw, P_DK // kw, P_DV // vw, P_DZ // vw
    ck0, cv0 = DN_QK_W // kw, 2 * DN_QK_W // vw
    hb = DN_TS // HALO_BLK

    def halo(t):
        return jnp.maximum(t * hb - 1, 0)

    return pl.pallas_call(
        _dnet_kernel,
        grid=(b, DN_K_HEADS // DN_HG, s // DN_TS),
        in_specs=[
            pl.BlockSpec((1, DN_TS, kw), lambda i, g, t: (i, t, q0 + g)),
            pl.BlockSpec((1, DN_TS, kw), lambda i, g, t: (i, t, k0 + g)),
            pl.BlockSpec((1, DN_TS, vw), lambda i, g, t: (i, t, v0 + g)),
            pl.BlockSpec((1, DN_TS, vw), lambda i, g, t: (i, t, z0 + g)),
            pl.BlockSpec((1, HALO_BLK, kw), lambda i, g, t: (i, halo(t), q0 + g)),
            pl.BlockSpec((1, HALO_BLK, kw), lambda i, g, t: (i, halo(t), k0 + g)),
            pl.BlockSpec((1, HALO_BLK, vw), lambda i, g, t: (i, halo(t), v0 + g)),
            pl.BlockSpec((1, 1, DN_TS, SMALL_W), lambda i, g, t: (i, g, t, 0)),
            pl.BlockSpec((1, DN_TS // GATE_ROWS, GROUP_VH, GATE_ROWS), lambda i, g, t: (i, t, g, 0)),
            pl.BlockSpec((CONV_WIDTH, kw), lambda i, g, t: (0, g)),
            pl.BlockSpec((CONV_WIDTH, kw), lambda i, g, t: (0, ck0 + g)),
            pl.BlockSpec((CONV_WIDTH, vw), lambda i, g, t: (0, cv0 + g)),
            pl.BlockSpec((1, DN_V_DIM), lambda i, g, t: (0, 0)),
        ],
        out_specs=pl.BlockSpec((1, DN_TS, vw), lambda i, g, t: (i, t, g)),
        out_shape=jax.ShapeDtypeStruct((b, s, DN_V_W), BF16),
        scratch_shapes=[
            pltpu.VMEM((DN_HG * V_PER_K, DN_K_DIM, DN_V_DIM), F32),
            pltpu.VMEM((DN_HG, RING * PAIR_ROWS, VW), F32),
            pltpu.VMEM((DN_HG, RING * PAIR_ROWS, VW), BF16),
            pltpu.VMEM((DN_HG, RING * PAIR_ROWS, VW), BF16),
            pltpu.VMEM((DN_HG, V_PER_K * RING * PAIR_ROWS, DN_K_DIM), BF16),
            pltpu.VMEM((DN_HG, RING * PAIR_ROWS, V_PER_K * CHUNK), BF16),
            pltpu.VMEM((CHUNK + DN_TS, DN_HG * RAW_W), BF16),
            pltpu.VMEM((2, DN_A_CHUNKS * DN_HG, CHUNK, 4 * CHUNK + 2 * DN_K_DIM + VW), F32),
        ],
        compiler_params=pltpu.CompilerParams(
            dimension_semantics=("parallel", "parallel", "arbitrary"),
            vmem_limit_bytes=V7X_VMEM_LIMIT),
        name="dnet",
    )(proj3, proj3, proj3, proj3, proj3, proj3, proj3, colp, rowp, conv_w, conv_w, conv_w,
      dn_norm_w)


MERGE_TM = 256
MERGE_TN = 2048


def _merge_kernel(oa_ref, od_ref, wa_ref, wd_ref, ga_ref, gd_ref, o_ref):
    ya = jnp.dot(oa_ref[...], wa_ref[...], preferred_element_type=F32)
    yd = jnp.dot(od_ref[...], wd_ref[...], preferred_element_type=F32)
    ga = _sigmoid(ga_ref[...].astype(F32))
    gd = _sigmoid(gd_ref[...].astype(F32))
    o_ref[...] = (ga * ya + gd * yd).astype(o_ref.dtype)


def _merge(oa, od, wa, wd, proj2):
    m = oa.shape[0]
    g0 = P_G // MERGE_TN
    g1 = (P_G + D_MODEL) // MERGE_TN
    return pl.pallas_call(
        _merge_kernel,
        grid=(m // MERGE_TM, D_MODEL // MERGE_TN),
        in_specs=[
            pl.BlockSpec((MERGE_TM, ATT_W), lambda i, j: (i, 0)),
            pl.BlockSpec((MERGE_TM, DN_V_W), lambda i, j: (i, 0)),
            pl.BlockSpec((ATT_W, MERGE_TN), lambda i, j: (0, j), pipeline_mode=pl.Buffered(1)),
            pl.BlockSpec((DN_V_W, MERGE_TN), lambda i, j: (0, j), pipeline_mode=pl.Buffered(1)),
            pl.BlockSpec((MERGE_TM, MERGE_TN), lambda i, j: (i, g0 + j)),
            pl.BlockSpec((MERGE_TM, MERGE_TN), lambda i, j: (i, g1 + j)),
        ],
        out_specs=pl.BlockSpec((MERGE_TM, MERGE_TN), lambda i, j: (i, j)),
        out_shape=jax.ShapeDtypeStruct((m, D_MODEL), BF16),
        compiler_params=pltpu.CompilerParams(
            dimension_semantics=("parallel", "parallel"),
            vmem_limit_bytes=V7X_VMEM_LIMIT),
        name="merge",
    )(oa, od, wa, wd, proj2, proj2)


OUT_TM = 512


def _out_kernel(m_ref, w_ref, x_ref, nw_ref, o_ref):
    y = x_ref[...] + jnp.dot(m_ref[...], w_ref[...], preferred_element_type=F32)
    ms = jnp.mean(y * y, axis=-1, keepdims=True)
    o_ref[...] = (y * lax.rsqrt(ms + EPS)) * nw_ref[...]


def _outproj(merged, w_out, x2, final_norm_w):
    m = x2.shape[0]
    return pl.pallas_call(
        _out_kernel,
        grid=(m // OUT_TM,),
        in_specs=[
            pl.BlockSpec((OUT_TM, D_MODEL), lambda i: (i, 0)),
            pl.BlockSpec((D_MODEL, D_MODEL), lambda i: (0, 0)),
            pl.BlockSpec((OUT_TM, D_MODEL), lambda i: (i, 0)),
            pl.BlockSpec((1, D_MODEL), lambda i: (0, 0)),
        ],
        out_specs=pl.BlockSpec((OUT_TM, D_MODEL), lambda i: (i, 0)),
        out_shape=jax.ShapeDtypeStruct((m, D_MODEL), F32),
        compiler_params=pltpu.CompilerParams(
            dimension_semantics=("parallel",),
            vmem_limit_bytes=V7X_VMEM_LIMIT),
        name="outproj",
    )(merged, w_out, x2, final_norm_w)


def _layer(x, norm_w, w_in, b_qkv, sinks, conv_w, a_log, dt_bias, dn_norm_w,
           w_att_branch, w_dn_branch, w_out, out_norm_w):
    b, s, d = x.shape
    m = b * s
    x2 = x.reshape(m, d)

    w_t = w_in.T
    wk = w_t[OFF_AK:OFF_AV].reshape(N_KV_HEADS, 1, HEAD_DIM, d)
    wv = w_t[OFF_AV:OFF_AZ].reshape(N_KV_HEADS, 1, HEAD_DIM, d)
    w_kv_t = jnp.concatenate([wk, wv], axis=1).reshape(2 * ATT_KV_W, d)
    w_small_t = jnp.pad(w_t[OFF_DB:OFF_G], ((0, SMALL_W - 2 * DN_V_HEADS), (0, 0)))
    bk = b_qkv[ATT_Q_W:ATT_Q_W + ATT_KV_W].reshape(N_KV_HEADS, 1, HEAD_DIM)
    bv = b_qkv[ATT_Q_W + ATT_KV_W:].reshape(N_KV_HEADS, 1, HEAD_DIM)
    bias = jnp.concatenate([b_qkv[:ATT_Q_W], jnp.zeros((P_KV - P_AZ,), F32),
                            jnp.concatenate([bk, bv], axis=1).reshape(-1)]).reshape(1, MAIN_W)

    proj, small = _inproj(x2, norm_w.reshape(1, d), w_t, w_kv_t, bias, w_small_t)
    proj3 = proj.reshape(b, s, MAIN_W)

    lane_pad = (DN_V_HEADS, SMALL_W - 2 * DN_V_HEADS)
    alog_row = jnp.pad(a_log, lane_pad).reshape(1, SMALL_W)
    dt_row = jnp.pad(dt_bias, lane_pad).reshape(1, SMALL_W)
    colp, rowp = _gates(small.reshape(b, s, SMALL_W), alog_row, dt_row)

    o_att = _attention(proj3, sinks)
    o_dn = _deltanet(proj3, colp, rowp, conv_w, dn_norm_w.reshape(1, DN_V_DIM))

    merged = _merge(o_att.reshape(m, ATT_W), o_dn.reshape(m, DN_V_W),
                    w_att_branch.astype(BF16), w_dn_branch.astype(BF16), proj)
    y = _outproj(merged, w_out.astype(BF16), x2, out_norm_w.reshape(1, d))
    return y.reshape(b, s, d)


def kernel(x, norm_w, w_in, b_qkv, sinks, conv_w, a_log, dt_bias, dn_norm_w,
           w_att_branch, w_dn_branch, w_out, final_norm_w):
    depth = norm_w.shape[0]
    assert depth == 1, "the final RMSNorm is fused into the single layer's output kernel"
    return _layer(x, norm_w[0], w_in[0], b_qkv[0], sinks[0], conv_w[0], a_log[0], dt_bias[0],
                  dn_norm_w[0], w_att_branch[0], w_dn_branch[0], w_out[0], final_norm_w)
```
